```python
import jax, jax.numpy as jnp
from jax import lax
import numpy as np

D_MODEL = 1024
BATCH = 32
SEQ = 2048
DEPTH = 4
DEC_BATCH = 16
DEC_SEQ = 2048
PAST_LEN = 128

MIX_WIDTH = D_MODEL
FOURIER_WIDTH = D_MODEL // 4
N_FGROUPS = 4
FGROUP_DIM = FOURIER_WIDTH // N_FGROUPS
HEAD_DIM = 64
ATTN_WIDTH = MIX_WIDTH - FOURIER_WIDTH
N_Q_HEADS = ATTN_WIDTH // HEAD_DIM
N_KV_HEADS = 4
Q_PER_KV = N_Q_HEADS // N_KV_HEADS
KV_WIDTH = N_KV_HEADS * HEAD_DIM
IN_WIDTH = FOURIER_WIDTH + ATTN_WIDTH + 2 * KV_WIDTH
WINDOW = 128
BLOCK = 128
ROPE_THETA = 500000.0
ROT_DIM = HEAD_DIM // 4
N_EXPERTS = 32
TOP_K = 4
D_FF = D_MODEL // 2
EXPERT_BLOCK = 512
SWIGLU_LIMIT = 7.0
SWIGLU_ALPHA = 1.702
NORM_EPS = 1e-5

kernel_name = "hymba_fnet_swa_moe_adaln_encoder"


def rms_norm(x, g):
    x32 = x.astype(jnp.float32)
    y = x32 * lax.rsqrt(jnp.mean(x32 * x32, axis=-1, keepdims=True) + NORM_EPS)
    return (y * g.astype(jnp.float32)).astype(x.dtype)


def partial_rotary(x, pos):
    half = ROT_DIM // 2
    inv_freq = jnp.power(ROPE_THETA, -jnp.arange(0, ROT_DIM, 2, dtype=jnp.float32) / ROT_DIM)
    ang = pos.astype(jnp.float32)[:, None] * inv_freq[None, :]
    cos = jnp.cos(ang)[None, :, None, :]
    sin = jnp.sin(ang)[None, :, None, :]
    xr = x[..., :ROT_DIM].astype(jnp.float32)
    x1, x2 = xr[..., :half], xr[..., half:]
    rot = jnp.concatenate([x1 * cos - x2 * sin, x2 * cos + x1 * sin], axis=-1).astype(x.dtype)
    return jnp.concatenate([rot, x[..., ROT_DIM:]], axis=-1)


def fourier_mix(f, w_fmix):
    B, S = f.shape[:2]
    fg = f.reshape(B, S, N_FGROUPS, FGROUP_DIM).astype(jnp.float32)
    spec = jnp.fft.fft2(fg, axes=(1, 3), norm="ortho").real.astype(f.dtype)
    out = jnp.einsum('bsgc,gcd->bsgd', spec, w_fmix)
    return out.reshape(B, S, FOURIER_WIDTH)


def windowed_attention(q, k, v, sinks):
    B, S = q.shape[:2]
    nb = S // BLOCK
    qb = q.reshape(B, nb, BLOCK, N_KV_HEADS, Q_PER_KV, HEAD_DIM)

    def windows(t):
        tp = jnp.pad(t, ((0, 0), (BLOCK, BLOCK), (0, 0), (0, 0)))
        tp = tp.reshape(B, nb + 2, BLOCK, N_KV_HEADS, HEAD_DIM)
        return jnp.concatenate([tp[:, :-2], tp[:, 1:-1], tp[:, 2:]], axis=2)

    kw, vw = windows(k), windows(v)
    s = jnp.einsum('bnqhgd,bnkhd->bnhgqk', qb, kw).astype(jnp.float32) * (HEAD_DIM ** -0.5)
    qpos = jnp.arange(nb)[:, None] * BLOCK + jnp.arange(BLOCK)[None, :]
    kpos = (jnp.arange(nb)[:, None] - 1) * BLOCK + jnp.arange(3 * BLOCK)[None, :]
    valid = ((jnp.abs(kpos[:, None, :] - qpos[:, :, None]) <= WINDOW)
             & (kpos >= 0)[:, None, :] & (kpos < S)[:, None, :])
    s = jnp.where(valid[None, :, None, None], s, -jnp.inf)
    sink = sinks.astype(jnp.float32).reshape(N_KV_HEADS, Q_PER_KV)[None, None, :, :, None, None]
    m = jnp.maximum(jnp.max(s, axis=-1, keepdims=True), sink)
    p = jnp.exp(s - m)
    p = (p / (jnp.sum(p, axis=-1, keepdims=True) + jnp.exp(sink - m))).astype(v.dtype)
    o = jnp.einsum('bnhgqk,bnkhd->bnqhgd', p, vw)
    return o.reshape(B, S, N_Q_HEADS * HEAD_DIM)


def moe_ffn(h, w_router, b_router, w_gu, b_gu, w_down, b_down):
    shp = h.shape
    t = h.reshape(-1, D_MODEL)
    T = t.shape[0]
    A = T * TOP_K
    NB = -(-A // EXPERT_BLOCK) + N_EXPERTS
    P = NB * EXPERT_BLOCK
    logits = (t @ w_router + b_router).astype(jnp.float32)
    vals, idx = lax.top_k(logits, TOP_K)
    wts = jax.nn.softmax(vals, axis=-1)
    flat_e = idx.reshape(-1)
    flat_tok = jnp.repeat(jnp.arange(T, dtype=jnp.int32), TOP_K)
    flat_w = wts.reshape(-1)
    order = jnp.argsort(flat_e, stable=True)
    se = flat_e[order]
    counts = jnp.zeros((N_EXPERTS,), jnp.int32).at[flat_e].add(1)
    padded = ((counts + EXPERT_BLOCK - 1) // EXPERT_BLOCK) * EXPERT_BLOCK
    start = jnp.cumsum(counts) - counts
    pad_end = jnp.cumsum(padded)
    pad_start = pad_end - padded
    dest = pad_start[se] + (jnp.arange(A, dtype=jnp.int32) - start[se])
    row_tok = jnp.zeros((P,), jnp.int32).at[dest].set(flat_tok[order])
    row_w = jnp.zeros((P,), jnp.float32).at[dest].set(flat_w[order]).astype(t.dtype)
    blk_start = jnp.arange(NB, dtype=jnp.int32) * EXPERT_BLOCK
    bexp = jnp.minimum(jnp.sum(pad_end[None, :] <= blk_start[:, None], axis=1), N_EXPERTS - 1)
    xs = t[row_tok].reshape(NB, EXPERT_BLOCK, D_MODEL)
    gu = jnp.einsum('nrd,ndf->nrf', xs, w_gu[bexp]) + b_gu[bexp][:, None, :]
    g = jnp.minimum(gu[..., :D_FF], SWIGLU_LIMIT)
    u = jnp.clip(gu[..., D_FF:], -SWIGLU_LIMIT, SWIGLU_LIMIT)
    act = (u + 1) * (g * jax.nn.sigmoid(SWIGLU_ALPHA * g))
    y = jnp.einsum('nrf,nfd->nrd', act, w_down[bexp]) + b_down[bexp][:, None, :]
    y = y.reshape(P, D_MODEL) * row_w[:, None]
    out = jnp.zeros_like(t).at[row_tok].add(y)
    return out.reshape(shp)


def encoder_layer(x, c, norm1_g, ada_w, ada_b, w_in, b_in, w_fmix, sinks, w_o, b_o,
                  norm2_g, w_router, b_router, w_gu, b_gu, w_down, b_down):
    B, S = x.shape[:2]
    mod = (jax.nn.silu(c) @ ada_w + ada_b)[:, None, :]
    sh1, sc1, g1, sh2, sc2, g2 = jnp.split(mod, 6, axis=-1)
    h = rms_norm(x, norm1_g) * (1 + sc1) + sh1
    z = h @ w_in + b_in
    f, q, k, v = jnp.split(z, [FOURIER_WIDTH, FOURIER_WIDTH + ATTN_WIDTH,
                               FOURIER_WIDTH + ATTN_WIDTH + KV_WIDTH], axis=-1)
    pos = jnp.arange(S)
    q = partial_rotary(q.reshape(B, S, N_Q_HEADS, HEAD_DIM), pos)
    k = partial_rotary(k.reshape(B, S, N_KV_HEADS, HEAD_DIM), pos)
    v = v.reshape(B, S, N_KV_HEADS, HEAD_DIM)
    mixed = jnp.concatenate([fourier_mix(f, w_fmix), windowed_attention(q, k, v, sinks)], axis=-1)
    x = x + g1 * (mixed @ w_o + b_o)
    h = rms_norm(x, norm2_g) * (1 + sc2) + sh2
    x = x + g2 * moe_ffn(h, w_router, b_router, w_gu, b_gu, w_down, b_down)
    return x


def trunk(x, c, norm1_g, ada_w, ada_b, w_in, b_in, w_fmix, sinks, w_o, b_o,
          norm2_g, w_router, b_router, w_gu, b_gu, w_down, b_down, final_g):
    for l in range(DEPTH):
        x = encoder_layer(x, c, norm1_g[l], ada_w[l], ada_b[l], w_in[l], b_in[l], w_fmix[l],
                          sinks[l], w_o[l], b_o[l], norm2_g[l], w_router[l], b_router[l],
                          w_gu[l], b_gu[l], w_down[l], b_down[l])
    return rms_norm(x, final_g)


def setup_inputs(seed: int = 0) -> dict:
    key = jax.random.key(seed)
    ks = jax.random.split(key, 24)
    f32 = jnp.float32
    nrm = lambda k, shape, s: jax.random.normal(k, shape, f32) * s
    return {
        "x_prompt": nrm(ks[0], (BATCH, SEQ, D_MODEL), 1.0),
        "x_sample": nrm(ks[1], (DEC_BATCH, DEC_SEQ, D_MODEL), 1.0),
        "c_prompt": nrm(ks[2], (BATCH, D_MODEL), 1.0),
        "c_sample": nrm(ks[3], (DEC_BATCH, D_MODEL), 1.0),
        "norm1_g": 1.0 + nrm(ks[4], (DEPTH, D_MODEL), 0.02),
        "ada_w": nrm(ks[5], (DEPTH, D_MODEL, 6 * D_MODEL), 0.5 * D_MODEL ** -0.5),
        "ada_b": nrm(ks[6], (DEPTH, 6 * D_MODEL), 0.01),
        "w_in": nrm(ks[7], (DEPTH, D_MODEL, IN_WIDTH), D_MODEL ** -0.5),
        "b_in": nrm(ks[8], (DEPTH, IN_WIDTH), 0.01),
        "w_fmix": nrm(ks[9], (DEPTH, N_FGROUPS, FGROUP_DIM, FGROUP_DIM), FGROUP_DIM ** -0.5),
        "sinks": nrm(ks[10], (DEPTH, N_Q_HEADS), 0.5),
        "w_o": nrm(ks[11], (DEPTH, MIX_WIDTH, D_MODEL), MIX_WIDTH ** -0.5),
        "b_o": nrm(ks[12], (DEPTH, D_MODEL), 0.01),
        "norm2_g": 1.0 + nrm(ks[13], (DEPTH, D_MODEL), 0.02),
        "w_router": nrm(ks[14], (DEPTH, D_MODEL, N_EXPERTS), D_MODEL ** -0.5),
        "b_router": nrm(ks[15], (DEPTH, N_EXPERTS), 0.01),
        "w_gu": nrm(ks[16], (DEPTH, N_EXPERTS, D_MODEL, 2 * D_FF), D_MODEL ** -0.5),
        "b_gu": nrm(ks[17], (DEPTH, N_EXPERTS, 2 * D_FF), 0.01),
        "w_down": nrm(ks[18], (DEPTH, N_EXPERTS, D_FF, D_MODEL), D_FF ** -0.5),
        "b_down": nrm(ks[19], (DEPTH, N_EXPERTS, D_MODEL), 0.01),
        "final_g": 1.0 + nrm(ks[20], (D_MODEL,), 0.02),
    }


def reference(x_prompt, x_sample, c_prompt, c_sample, norm1_g, ada_w, ada_b, w_in, b_in, w_fmix,
              sinks, w_o, b_o, norm2_g, w_router, b_router, w_gu, b_gu, w_down, b_down, final_g):
    y_prompt = trunk(x_prompt, c_prompt, norm1_g, ada_w, ada_b, w_in, b_in, w_fmix, sinks, w_o, b_o,
                     norm2_g, w_router, b_router, w_gu, b_gu, w_down, b_down, final_g)
    y_sample = trunk(x_sample, c_sample, norm1_g, ada_w, ada_b, w_in, b_in, w_fmix, sinks, w_o, b_o,
                     norm2_g, w_router, b_router, w_gu, b_gu, w_down, b_down, final_g)
    return (y_prompt, y_sample)
```

```python
import functools

import jax
import jax.numpy as jnp
from jax import lax
from jax.experimental import pallas as pl
from jax.experimental.pallas import tpu as pltpu
from jax.experimental.pallas import tpu_sc as plsc

D_MODEL = 1024
DEPTH = 4
FOURIER_WIDTH = 256
N_FGROUPS = 4
FGROUP_DIM = 64
HEAD_DIM = 64
N_Q_HEADS = 12
N_KV_HEADS = 4
Q_PER_KV = 3
ATTN_WIDTH = N_Q_HEADS * HEAD_DIM
KV_WIDTH = N_KV_HEADS * HEAD_DIM
IN_WIDTH = FOURIER_WIDTH + ATTN_WIDTH + 2 * KV_WIDTH
WINDOW = 128
ROPE_THETA = 500000.0
ROT_DIM = 16
N_EXPERTS = 32
TOP_K = 4
D_FF = 512
EXPERT_BLOCK = 512
SWIGLU_LIMIT = 7.0
SWIGLU_ALPHA = 1.702
NORM_EPS = 1e-5

LANES = 128
HALF = D_MODEL // 2
TOKEN_TILE = 512
Q_TILE = 256
KEY_SPAN = Q_TILE + 2 * WINDOW
SC_CHUNK = 128
SC_WORKERS = 32
VMEM_LIMIT = 56 * 1024 * 1024
NEG_BIG = -1e30
HI_MASK = -65536

_f32 = jnp.float32
_bf16 = jnp.bfloat16


def _pack_halves(a, b):
    ua = lax.bitcast_convert_type(a.astype(_bf16).astype(_f32), jnp.int32)
    ub = lax.bitcast_convert_type(b.astype(_bf16).astype(_f32), jnp.int32)
    return ua | lax.shift_right_logical(ub, 16)


def _unpack_halves(p):
    a = lax.bitcast_convert_type(p & HI_MASK, _f32)
    b = lax.bitcast_convert_type(lax.shift_left(p, 16), _f32)
    return a, b


def _rms(x, g):
    ms = jnp.mean(x * x, axis=-1, keepdims=True)
    return x * lax.rsqrt(ms + NORM_EPS) * g


def _mod_kernel(c_ref, w_ref, b_ref, o_ref):
    c = c_ref[...]
    s = c * (1.0 / (1.0 + jnp.exp(-c)))
    o_ref[0] = jnp.dot(s.astype(_bf16), w_ref[0].astype(_bf16), preferred_element_type=_f32) + b_ref[0]


def _modulation(c_all, ada_w, ada_b):
    nb = c_all.shape[0]
    ncol = ada_w.shape[2] // D_MODEL
    return pl.pallas_call(
        _mod_kernel,
        out_shape=jax.ShapeDtypeStruct((DEPTH, nb, 6 * D_MODEL), _f32),
        grid=(DEPTH, ncol),
        in_specs=[
            pl.BlockSpec((nb, D_MODEL), lambda l, j: (0, 0)),
            pl.BlockSpec((1, D_MODEL, D_MODEL), lambda l, j: (l, 0, j)),
            pl.BlockSpec((1, 1, D_MODEL), lambda l, j: (l, 0, j)),
        ],
        out_specs=pl.BlockSpec((1, nb, D_MODEL), lambda l, j: (l, 0, j)),
        name="modulation",
    )(c_all, ada_w, ada_b.reshape(DEPTH, 1, 6 * D_MODEL))


def _moe_combine(yk_ref, rt_ref):
    rt = rt_ref[...]
    acc_a = None
    acc_b = None
    for k in range(TOP_K):
        w = lax.bitcast_convert_type(rt[:, 2 * TOP_K + k:2 * TOP_K + k + 1], _f32)
        a, b = _unpack_halves(yk_ref[k])
        acc_a = w * a if acc_a is None else acc_a + w * a
        acc_b = w * b if acc_b is None else acc_b + w * b
    return jnp.concatenate([acc_a, acc_b], axis=1)


def _inproj_body(x, sh_ref, sc_ref, ng_ref, w_ref, b_ref, ab_ref, rc_ref, rs1_ref, rs2_ref,
                 y_ref, q_ref, k_ref, v_ref):
    h = _rms(x, ng_ref[...]) * (1.0 + sc_ref[0]) + sh_ref[0]
    z = jnp.dot(h.astype(_bf16), w_ref[...], preferred_element_type=_f32) + b_ref[...]
    f = z[:, :FOURIER_WIDTH]
    y_ref[...] = jnp.dot(f.astype(_bf16), ab_ref[...], preferred_element_type=_f32).astype(_bf16)
    rc = rc_ref[...]
    rs1 = rs1_ref[...]
    rs2 = rs2_ref[...]

    def rope(t):
        return t * rc + pltpu.roll(t, LANES - ROT_DIM // 2, 1) * rs1 + pltpu.roll(t, ROT_DIM // 2, 1) * rs2

    q0 = FOURIER_WIDTH
    for c in range(ATTN_WIDTH // LANES):
        t = z[:, q0 + c * LANES:q0 + (c + 1) * LANES]
        q_ref[:, c * LANES:(c + 1) * LANES] = (rope(t) * (HEAD_DIM ** -0.5)).astype(_bf16)
    k0 = q0 + ATTN_WIDTH
    for c in range(KV_WIDTH // LANES):
        t = z[:, k0 + c * LANES:k0 + (c + 1) * LANES]
        k_ref[:, c * LANES:(c + 1) * LANES] = rope(t).astype(_bf16)
    v_ref[...] = z[:, k0 + KV_WIDTH:].astype(_bf16)


def _inproj_first_kernel(x_ref, *refs):
    _inproj_body(x_ref[...], *refs)


def _inproj_combine_kernel(x1_ref, yk_ref, rt_ref, g2_ref, *refs):
    x = x1_ref[...] + g2_ref[0] * _moe_combine(yk_ref, rt_ref)
    refs[-1][...] = x
    _inproj_body(x, *refs[:-1])


def _inproj(seq, x, moe, sh1, sc1, norm_g, w_in, b_in, ab, rope_tabs):
    t_tokens = x.shape[0]
    tm = TOKEN_TILE
    tiles_per_seq = seq // tm
    row = lambda i: (i, 0)
    per_seq = lambda i: (i // tiles_per_seq, 0, 0)
    const = lambda i: (0, 0)
    pos = lambda i: (i % tiles_per_seq, 0)
    in_specs = [pl.BlockSpec((tm, D_MODEL), row)]
    args = [x]
    if moe is not None:
        yk, route, g2 = moe
        in_specs += [pl.BlockSpec((TOP_K, tm, HALF), lambda i: (0, i, 0)),
                     pl.BlockSpec((tm, LANES), row),
                     pl.BlockSpec((1, 1, D_MODEL), per_seq)]
        args += [yk, route, g2]
    in_specs += [pl.BlockSpec((1, 1, D_MODEL), per_seq), pl.BlockSpec((1, 1, D_MODEL), per_seq),
                 pl.BlockSpec((1, D_MODEL), const),
                 pl.BlockSpec((D_MODEL, IN_WIDTH), const), pl.BlockSpec((1, IN_WIDTH), const),
                 pl.BlockSpec((FOURIER_WIDTH, 2 * FOURIER_WIDTH), const),
                 pl.BlockSpec((tm, LANES), pos), pl.BlockSpec((tm, LANES), pos), pl.BlockSpec((tm, LANES), pos)]
    args += [sh1, sc1, norm_g, w_in, b_in, ab, *rope_tabs]
    out_shape = [jax.ShapeDtypeStruct((t_tokens, 2 * FOURIER_WIDTH), _bf16),
                 jax.ShapeDtypeStruct((t_tokens, ATTN_WIDTH), _bf16),
                 jax.ShapeDtypeStruct((t_tokens, KV_WIDTH), _bf16),
                 jax.ShapeDtypeStruct((t_tokens, KV_WIDTH), _bf16)]
    out_specs = [pl.BlockSpec((tm, 2 * FOURIER_WIDTH), row), pl.BlockSpec((tm, ATTN_WIDTH), row),
                 pl.BlockSpec((tm, KV_WIDTH), row), pl.BlockSpec((tm, KV_WIDTH), row)]
    if moe is not None:
        out_shape.append(jax.ShapeDtypeStruct((t_tokens, D_MODEL), _f32))
        out_specs.append(pl.BlockSpec((tm, D_MODEL), row))
    return pl.pallas_call(
        _inproj_first_kernel if moe is None else _inproj_combine_kernel,
        out_shape=out_shape,
        grid=(t_tokens // tm,),
        in_specs=in_specs,
        out_specs=out_specs,
        compiler_params=pltpu.CompilerParams(dimension_semantics=("parallel",), vmem_limit_bytes=VMEM_LIMIT),
        name="inproj",
    )(*args)


def _fourier_kernel(y_ref, ce_ref, se_ref, co_ref, so_ref, o_ref):
    half = y_ref.shape[0] // 2
    lo = y_ref[:half, :].astype(_f32)
    hi = y_ref[half:, :].astype(_f32)
    ye = (lo + hi).astype(_bf16)
    yo = (lo - hi).astype(_bf16)
    w = FOURIER_WIDTH
    even = (jnp.dot(ce_ref[...], ye[:, :w], preferred_element_type=_f32)
            + jnp.dot(se_ref[...], ye[:, w:], preferred_element_type=_f32))
    odd = (jnp.dot(co_ref[...], yo[:, :w], preferred_element_type=_f32)
           + jnp.dot(so_ref[...], yo[:, w:], preferred_element_type=_f32))
    for c in range(FOURIER_WIDTH // LANES):
        o_ref[c, pl.ds(0, half, stride=2), :] = even[:, c * LANES:(c + 1) * LANES]
        o_ref[c, pl.ds(1, half, stride=2), :] = odd[:, c * LANES:(c + 1) * LANES]


def _fourier(seq, y, dft):
    t_tokens = y.shape[0]
    half = seq // 2
    const = lambda b: (0, 0)
    return pl.pallas_call(
        _fourier_kernel,
        out_shape=jax.ShapeDtypeStruct((FOURIER_WIDTH // LANES, t_tokens, LANES), _f32),
        grid=(t_tokens // seq,),
        in_specs=[pl.BlockSpec((seq, 2 * FOURIER_WIDTH), lambda b: (b, 0))]
        + [pl.BlockSpec((half, half), const)] * 4,
        out_specs=pl.BlockSpec((FOURIER_WIDTH // LANES, seq, LANES), lambda b: (0, b, 0)),
        compiler_params=pltpu.CompilerParams(dimension_semantics=("parallel",), vmem_limit_bytes=VMEM_LIMIT),
        name="fourier",
    )(y, *dft)


def _dft_tables(seq):
    half = seq // 2
    j = jnp.arange(half, dtype=jnp.int32)[:, None]
    k = jnp.arange(half, dtype=jnp.int32)[None, :]
    ang_e = ((2 * j * k) % seq).astype(_f32) * (2.0 * jnp.pi / seq)
    ang_o = (((2 * j + 1) * k) % seq).astype(_f32) * (2.0 * jnp.pi / seq)
    return (jnp.cos(ang_e).astype(_bf16), jnp.sin(ang_e).astype(_bf16),
            jnp.cos(ang_o).astype(_bf16), jnp.sin(ang_o).astype(_bf16))


def _channel_dft_fold(w_fmix, seq):
    c = jnp.arange(FGROUP_DIM, dtype=jnp.int32)
    ang = ((c[:, None] * c[None, :]) % FGROUP_DIM).astype(_f32) * (2.0 * jnp.pi / FGROUP_DIM)
    scale = (seq * FGROUP_DIM) ** -0.5
    hp = lax.Precision.HIGHEST
    a = jnp.einsum('cm,gmd->gcd', jnp.cos(ang) * scale, w_fmix, precision=hp)
    b = jnp.einsum('cm,gmd->gcd', -jnp.sin(ang) * scale, w_fmix, precision=hp)
    eye = jnp.eye(N_FGROUPS, dtype=_f32)
    bd = lambda m: jnp.einsum('gcd,gh->gchd', m, eye).reshape(FOURIER_WIDTH, FOURIER_WIDTH)
    return jnp.concatenate([bd(a), bd(b)], axis=1).astype(_bf16)


def _rope_tables(seq):
    half = ROT_DIM // 2
    inv_freq = jnp.power(ROPE_THETA, -jnp.arange(0, ROT_DIM, 2, dtype=_f32) / ROT_DIM)
    ang = jnp.arange(seq, dtype=_f32)[:, None] * inv_freq[None, :]
    cos, sin = jnp.cos(ang), jnp.sin(ang)
    ones = jnp.ones((seq, HEAD_DIM - ROT_DIM), _f32)
    zeros = jnp.zeros((seq, HEAD_DIM - ROT_DIM), _f32)
    zh = jnp.zeros((seq, half), _f32)
    c1 = jnp.concatenate([cos, cos, ones], axis=1)
    s1 = jnp.concatenate([-sin, zh, zeros], axis=1)
    s2 = jnp.concatenate([zh, sin, zeros], axis=1)
    rep = LANES // HEAD_DIM
    return tuple(jnp.tile(t, (1, rep)) for t in (c1, s1, s2))


def _attn_kernel(sink_ref, q_ref, k_ref, v_ref, bias_ref, o_ref):
    seq = k_ref.shape[0]
    nqb = seq // Q_TILE
    qb = pl.program_id(1)
    start = pl.multiple_of(jnp.clip(qb * Q_TILE - WINDOW, 0, seq - KEY_SPAN), WINDOW)
    variant = jnp.where(qb == 0, 0, jnp.where(qb == nqb - 1, 2, 1))
    bias = bias_ref[variant]
    for h in range(N_KV_HEADS):
        kh = k_ref[pl.ds(start, KEY_SPAN), h * HEAD_DIM:(h + 1) * HEAD_DIM]
        vh = v_ref[pl.ds(start, KEY_SPAN), h * HEAD_DIM:(h + 1) * HEAD_DIM]
        q3 = jnp.concatenate(
            [q_ref[:, (Q_PER_KV * h + i) * HEAD_DIM:(Q_PER_KV * h + i + 1) * HEAD_DIM] for i in range(Q_PER_KV)],
            axis=0)
        s = lax.dot_general(q3, kh, (((1,), (1,)), ((), ())), preferred_element_type=_f32)
        ps = []
        ls = []
        for i in range(Q_PER_KV):
            sink = sink_ref[Q_PER_KV * h + i]
            si = s[i * Q_TILE:(i + 1) * Q_TILE] + bias
            m = jnp.maximum(jnp.max(si, axis=1, keepdims=True), sink)
            p = jnp.exp(si - m)
            ls.append(jnp.sum(p, axis=1, keepdims=True) + jnp.exp(sink - m))
            ps.append(p.astype(_bf16))
        o = jnp.dot(jnp.concatenate(ps, axis=0), vh, preferred_element_type=_f32)
        for i in range(Q_PER_KV):
            g = Q_PER_KV * h + i
            o_ref[:, g * HEAD_DIM:(g + 1) * HEAD_DIM] = (o[i * Q_TILE:(i + 1) * Q_TILE] / ls[i]).astype(_bf16)


def _band_bias():
    qi = jnp.arange(Q_TILE, dtype=jnp.int32)[:, None]
    kj = jnp.arange(KEY_SPAN, dtype=jnp.int32)[None, :]
    tabs = [jnp.where(jnp.abs(kj - off - qi) <= WINDOW, 0.0, NEG_BIG) for off in (0, WINDOW, 2 * WINDOW)]
    return jnp.stack(tabs).astype(_f32)


def _attention(seq, q, k, v, sinks, bias):
    t_tokens = q.shape[0]
    nqb = seq // Q_TILE
    return pl.pallas_call(
        _attn_kernel,
        out_shape=jax.ShapeDtypeStruct((t_tokens, ATTN_WIDTH), _bf16),
        grid=(t_tokens // seq, nqb),
        in_specs=[pl.BlockSpec(memory_space=pltpu.SMEM),
                  pl.BlockSpec((Q_TILE, ATTN_WIDTH), lambda b, j: (b * nqb + j, 0)),
                  pl.BlockSpec((seq, KV_WIDTH), lambda b, j: (b, 0)),
                  pl.BlockSpec((seq, KV_WIDTH), lambda b, j: (b, 0)),
                  pl.BlockSpec((3, Q_TILE, KEY_SPAN), lambda b, j: (0, 0, 0))],
        out_specs=pl.BlockSpec((Q_TILE, ATTN_WIDTH), lambda b, j: (b * nqb + j, 0)),
        compiler_params=pltpu.CompilerParams(dimension_semantics=("parallel", "parallel"),
                                             vmem_limit_bytes=VMEM_LIMIT),
        name="attention",
    )(sinks, q, k, v, bias)


def _outproj_kernel(x_ref, mf_ref, at_ref, g1_ref, sh_ref, sc_ref, ng_ref, wof_ref, woa_ref, bo_ref,
                    wr_ref, br_ref, tri_ref, x1_ref, hp_ref, rt_ref, cnt_ref, carry_ref):
    i = pl.program_id(0)

    @pl.when(i == 0)
    def _():
        carry_ref[...] = jnp.zeros_like(carry_ref)

    mf = jnp.concatenate([mf_ref[c] for c in range(FOURIER_WIDTH // LANES)], axis=1)
    mix = (jnp.dot(mf.astype(_bf16), wof_ref[...], preferred_element_type=_f32)
           + jnp.dot(at_ref[...], woa_ref[...], preferred_element_type=_f32) + bo_ref[...])
    x1 = x_ref[...] + g1_ref[0] * mix
    x1_ref[...] = x1
    h = _rms(x1, ng_ref[...]) * (1.0 + sc_ref[0]) + sh_ref[0]
    hp_ref[...] = _pack_halves(h[:, :HALF], h[:, HALF:])
    logits = jnp.dot(h.astype(_bf16), wr_ref[...], preferred_element_type=_f32) + br_ref[...]

    lane = lax.broadcasted_iota(jnp.int32, logits.shape, 1)
    work = logits
    hots, vals, idxs = [], [], []
    for _k in range(TOP_K):
        mx = jnp.max(work, axis=1, keepdims=True)
        ix = jnp.min(jnp.where(work == mx, lane, LANES), axis=1, keepdims=True)
        hot = lane == ix
        work = jnp.where(hot, -jnp.inf, work)
        hots.append(hot)
        vals.append(mx)
        idxs.append(ix)
    es = [jnp.exp(v - vals[0]) for v in vals]
    den = es[0] + es[1] + es[2] + es[3]
    member = jnp.zeros(logits.shape, _f32)
    for hot in hots:
        member = member + hot.astype(_f32)
    before = jnp.dot(tri_ref[...], member.astype(_bf16), preferred_element_type=_f32) + carry_ref[...]
    slab = jnp.zeros(logits.shape, jnp.int32)
    for k in range(TOP_K):
        rank = jnp.sum(jnp.where(hots[k], before, 0.0), axis=1, keepdims=True).astype(jnp.int32)
        wbits = lax.bitcast_convert_type(es[k] / den, jnp.int32)
        slab = jnp.where(lane == k, idxs[k], slab)
        slab = jnp.where(lane == TOP_K + k, rank, slab)
        slab = jnp.where(lane == 2 * TOP_K + k, wbits, slab)
    rt_ref[...] = slab
    carry_ref[...] = carry_ref[...] + jnp.sum(member, axis=0, keepdims=True)
    cnt_ref[...] = carry_ref[...].astype(jnp.int32)


def _outproj(seq, x, mf, attn, g1, sh2, sc2, norm_g, wo_f, wo_a, b_o, w_r, b_r, tri):
    t_tokens = x.shape[0]
    tm = TOKEN_TILE
    tiles_per_seq = seq // tm
    row = lambda i: (i, 0)
    per_seq = lambda i: (i // tiles_per_seq, 0, 0)
    const = lambda i: (0, 0)
    return pl.pallas_call(
        _outproj_kernel,
        out_shape=[jax.ShapeDtypeStruct((t_tokens, D_MODEL), _f32),
                   jax.ShapeDtypeStruct((t_tokens, HALF), jnp.int32),
                   jax.ShapeDtypeStruct((t_tokens, LANES), jnp.int32),
                   jax.ShapeDtypeStruct((1, LANES), jnp.int32)],
        grid=(t_tokens // tm,),
        in_specs=[pl.BlockSpec((tm, D_MODEL), row),
                  pl.BlockSpec((FOURIER_WIDTH // LANES, tm, LANES), lambda i: (0, i, 0)),
                  pl.BlockSpec((tm, ATTN_WIDTH), row),
                  pl.BlockSpec((1, 1, D_MODEL), per_seq), pl.BlockSpec((1, 1, D_MODEL), per_seq),
                  pl.BlockSpec((1, 1, D_MODEL), per_seq),
                  pl.BlockSpec((1, D_MODEL), const),
                  pl.BlockSpec((FOURIER_WIDTH, D_MODEL), const), pl.BlockSpec((ATTN_WIDTH, D_MODEL), const),
                  pl.BlockSpec((1, D_MODEL), const),
                  pl.BlockSpec((D_MODEL, LANES), const), pl.BlockSpec((1, LANES), const),
                  pl.BlockSpec((tm, tm), const)],
        out_specs=[pl.BlockSpec((tm, D_MODEL), row), pl.BlockSpec((tm, HALF), row),
                   pl.BlockSpec((tm, LANES), row), pl.BlockSpec((1, LANES), const)],
        scratch_shapes=[pltpu.VMEM((1, LANES), _f32)],
        compiler_params=pltpu.CompilerParams(dimension_semantics=("arbitrary",), vmem_limit_bytes=VMEM_LIMIT),
        name="outproj",
    )(x, mf, attn, g1, sh2, sc2, norm_g, wo_f, wo_a, b_o, w_r, b_r, tri)


def _expert_kernel(blk_ref, bexp_ref, nact_ref, x_ref, wgu_ref, bgu_ref, wd_ref, bd_ref, y_ref, wgu_bf, wd_bf):
    i = pl.program_id(0)
    prev = bexp_ref[jnp.maximum(i - 1, 0)]

    @pl.when((i == 0) | (bexp_ref[i] != prev))
    def _():
        wgu_bf[...] = wgu_ref[0].astype(_bf16)
        wd_bf[...] = wd_ref[0].astype(_bf16)

    @pl.when(i < nact_ref[0])
    def _():
        xa, xb = _unpack_halves(x_ref[...])
        gu = (jnp.dot(xa.astype(_bf16), wgu_bf[:HALF, :], preferred_element_type=_f32)
              + jnp.dot(xb.astype(_bf16), wgu_bf[HALF:, :], preferred_element_type=_f32) + bgu_ref[0])
        g = jnp.minimum(gu[:, :D_FF], SWIGLU_LIMIT)
        u = jnp.clip(gu[:, D_FF:], -SWIGLU_LIMIT, SWIGLU_LIMIT)
        act = (u + 1.0) * (g * (1.0 / (1.0 + jnp.exp(-SWIGLU_ALPHA * g))))
        y = jnp.dot(act.astype(_bf16), wd_bf[...], preferred_element_type=_f32) + bd_ref[0]
        y_ref[...] = _pack_halves(y[:, :HALF], y[:, HALF:])


def _experts(xs, blk, bexp, nact, w_gu, b_gu, w_down, b_down):
    p_rows = xs.shape[0]
    nblk = p_rows // EXPERT_BLOCK
    grid_spec = pltpu.PrefetchScalarGridSpec(
        num_scalar_prefetch=3,
        grid=(nblk,),
        in_specs=[pl.BlockSpec((EXPERT_BLOCK, HALF), lambda i, blk, be, na: (blk[i], 0)),
                  pl.BlockSpec((1, D_MODEL, 2 * D_FF), lambda i, blk, be, na: (be[i], 0, 0)),
                  pl.BlockSpec((1, 1, 2 * D_FF), lambda i, blk, be, na: (be[i], 0, 0)),
                  pl.BlockSpec((1, D_FF, D_MODEL), lambda i, blk, be, na: (be[i], 0, 0)),
                  pl.BlockSpec((1, 1, D_MODEL), lambda i, blk, be, na: (be[i], 0, 0))],
        out_specs=pl.BlockSpec((EXPERT_BLOCK, HALF), lambda i, blk, be, na: (blk[i], 0)),
        scratch_shapes=[pltpu.VMEM((D_MODEL, 2 * D_FF), _bf16), pltpu.VMEM((D_FF, D_MODEL), _bf16)])
    return pl.pallas_call(
        _expert_kernel,
        out_shape=jax.ShapeDtypeStruct((p_rows, HALF), jnp.int32),
        grid_spec=grid_spec,
        compiler_params=pltpu.CompilerParams(dimension_semantics=("arbitrary",), vmem_limit_bytes=VMEM_LIMIT),
        name="experts",
    )(blk, bexp, nact, xs, w_gu, b_gu.reshape(N_EXPERTS, 1, 2 * D_FF), w_down,
      b_down.reshape(N_EXPERTS, 1, D_MODEL))


def _sc_mesh():
    return plsc.VectorSubcoreMesh(core_axis_name="c", subcore_axis_name="s")


def _sc_worker():
    return lax.axis_index("s") * 2 + lax.axis_index("c")


def _dispatch(hp, dest_t, p_rows):
    n_chunks = dest_t.shape[0]
    per_worker = n_chunks // SC_WORKERS

    @functools.partial(
        pl.kernel, mesh=_sc_mesh(),
        out_type=jax.ShapeDtypeStruct((p_rows, HALF), jnp.int32),
        scratch_types=[pltpu.VMEM((TOP_K, SC_CHUNK), jnp.int32),
                       pltpu.VMEM((SC_CHUNK, HALF), jnp.int32),
                       pltpu.SemaphoreType.DMA],
        name="dispatch")
    def run(hp_hbm, dest_hbm, xs_hbm, idx_v, rows_v, sem):
        base = _sc_worker() * per_worker

        @pl.loop(0, per_worker)
        def _(j):
            chunk = base + j
            pltpu.sync_copy(dest_hbm.at[chunk], idx_v)
            pltpu.sync_copy(hp_hbm.at[pl.ds(chunk * SC_CHUNK, SC_CHUNK)], rows_v)
            copies = [pltpu.async_copy(rows_v, xs_hbm.at[idx_v.at[k]], sem) for k in range(TOP_K)]
            for cp in copies:
                cp.wait()

    return run(hp, dest_t)


def _gather_rows(y, dest_t, t_tokens):
    n_chunks = dest_t.shape[0]
    per_worker = n_chunks // SC_WORKERS

    @functools.partial(
        pl.kernel, mesh=_sc_mesh(),
        out_type=jax.ShapeDtypeStruct((TOP_K, t_tokens, HALF), jnp.int32),
        scratch_types=[pltpu.VMEM((TOP_K, SC_CHUNK), jnp.int32),
                       pltpu.VMEM((SC_CHUNK, HALF), jnp.int32),
                       pltpu.SemaphoreType.DMA],
        name="gather_rows")
    def run(y_hbm, dest_hbm, yk_hbm, idx_v, rows_v, sem):
        base = _sc_worker() * per_worker

        @pl.loop(0, per_worker)
        def _(j):
            chunk = base + j
            pltpu.sync_copy(dest_hbm.at[chunk], idx_v)
            for k in range(TOP_K):
                pltpu.async_copy(y_hbm.at[idx_v.at[k]], rows_v, sem).wait()
                pltpu.sync_copy(rows_v, yk_hbm.at[k, pl.ds(chunk * SC_CHUNK, SC_CHUNK)])

    return run(y, dest_t)


def _final_kernel(x1_ref, yk_ref, rt_ref, g2_ref, ng_ref, o_ref):
    x = x1_ref[...] + g2_ref[0] * _moe_combine(yk_ref, rt_ref)
    o_ref[...] = _rms(x, ng_ref[...])


def _final(seq, x1, yk, route, g2, final_g):
    t_tokens = x1.shape[0]
    tm = TOKEN_TILE
    tiles_per_seq = seq // tm
    row = lambda i: (i, 0)
    return pl.pallas_call(
        _final_kernel,
        out_shape=jax.ShapeDtypeStruct((t_tokens, D_MODEL), _f32),
        grid=(t_tokens // tm,),
        in_specs=[pl.BlockSpec((tm, D_MODEL), row),
                  pl.BlockSpec((TOP_K, tm, HALF), lambda i: (0, i, 0)),
                  pl.BlockSpec((tm, LANES), row),
                  pl.BlockSpec((1, 1, D_MODEL), lambda i: (i // tiles_per_seq, 0, 0)),
                  pl.BlockSpec((1, D_MODEL), lambda i: (0, 0))],
        out_specs=pl.BlockSpec((tm, D_MODEL), row),
        compiler_params=pltpu.CompilerParams(dimension_semantics=("parallel",), vmem_limit_bytes=VMEM_LIMIT),
        name="final",
    )(x1, yk, route, g2, final_g)


def _plan(route, counts, nblk):
    cnt = counts[0, :N_EXPERTS]
    padded = ((cnt + EXPERT_BLOCK - 1) // EXPERT_BLOCK) * EXPERT_BLOCK
    pad_end = jnp.cumsum(padded)
    pad_start = pad_end - padded
    idx = route[:, :TOP_K]
    rank = route[:, TOP_K:2 * TOP_K]
    dest = jnp.take(pad_start, idx, axis=0) + rank
    nact = jnp.maximum(pad_end[-1] // EXPERT_BLOCK, 1)
    blk = jnp.minimum(jnp.arange(nblk, dtype=jnp.int32), nact - 1)
    bexp = jnp.minimum(jnp.sum(pad_end[None, :] <= (blk * EXPERT_BLOCK)[:, None], axis=1), N_EXPERTS - 1)
    t_tokens = route.shape[0]
    dest_t = dest.reshape(t_tokens // SC_CHUNK, SC_CHUNK, TOP_K).transpose(0, 2, 1)
    return dest_t, blk.astype(jnp.int32), bexp.astype(jnp.int32), nact.reshape(1).astype(jnp.int32)


def _trunk(x_all, c_all, norm1_g, ada_w, ada_b, w_in, b_in, w_fmix, sinks, w_o, b_o,
           norm2_g, w_router, b_router, w_gu, b_gu, w_down, b_down, final_g):
    nb, seq, _ = x_all.shape
    t_tokens = nb * seq
    assert seq % TOKEN_TILE == 0 and seq % Q_TILE == 0 and seq >= KEY_SPAN
    assert t_tokens % (SC_CHUNK * SC_WORKERS) == 0
    nblk = (t_tokens * TOP_K) // EXPERT_BLOCK + N_EXPERTS
    p_rows = nblk * EXPERT_BLOCK

    mod = _modulation(c_all, ada_w, ada_b)
    mod = mod.reshape(DEPTH, nb, 6, 1, D_MODEL)
    rope_tabs = _rope_tables(seq)
    dft = _dft_tables(seq)
    bias = _band_bias()
    tri = jnp.tril(jnp.ones((TOKEN_TILE, TOKEN_TILE), _f32), -1).astype(_bf16)

    x = x_all.reshape(t_tokens, D_MODEL)
    moe = None
    for l in range(DEPTH):
        sh1, sc1, g1, sh2, sc2, g2 = (mod[l, :, j] for j in range(6))
        ab = _channel_dft_fold(w_fmix[l], seq)
        outs = _inproj(seq, x, moe, sh1, sc1, norm1_g[l].reshape(1, D_MODEL), w_in[l].astype(_bf16),
                       b_in[l].reshape(1, IN_WIDTH), ab, rope_tabs)
        if moe is None:
            y, q, k, v = outs
        else:
            y, q, k, v, x = outs
        mf = _fourier(seq, y, dft)
        attn = _attention(seq, q, k, v, sinks[l], bias)
        w_o_bf = w_o[l].astype(_bf16)
        w_r = jnp.pad(w_router[l], ((0, 0), (0, LANES - N_EXPERTS))).astype(_bf16)
        b_r = jnp.pad(b_router[l], (0, LANES - N_EXPERTS), constant_values=NEG_BIG).reshape(1, LANES)
        x1, hp, route, counts = _outproj(seq, x, mf, attn, g1, sh2, sc2, norm2_g[l].reshape(1, D_MODEL),
                                         w_o_bf[:FOURIER_WIDTH], w_o_bf[FOURIER_WIDTH:],
                                         b_o[l].reshape(1, D_MODEL), w_r, b_r, tri)
        dest_t, blk, bexp, nact = _plan(route, counts, nblk)
        xs = _dispatch(hp, dest_t, p_rows)
        ys = _experts(xs, blk, bexp, nact, w_gu[l], b_gu[l], w_down[l], b_down[l])
        yk = _gather_rows(ys, dest_t, t_tokens)
        x = x1
        moe = (yk, route, g2)
    out = _final(seq, x, moe[0], moe[1], moe[2], final_g.reshape(1, D_MODEL))
    return out.reshape(nb, seq, D_MODEL)


def kernel(x_prompt, x_sample, c_prompt, c_sample, norm1_g, ada_w, ada_b, w_in, b_in, w_fmix, sinks, w_o, b_o,
           norm2_g, w_router, b_router, w_gu, b_gu, w_down, b_down, final_g):
    nbp = x_prompt.shape[0]
    x_all = jnp.concatenate([x_prompt, x_sample], axis=0)
    c_all = jnp.concatenate([c_prompt, c_sample], axis=0)
    out = _trunk(x_all, c_all, norm1_g, ada_w, ada_b, w_in, b_in, w_fmix, sinks, w_o, b_o,
                 norm2_g, w_router, b_router, w_gu, b_gu, w_down, b_down, final_g)
    return out[:nbp], out[nbp:]
```

```python
import functools

import jax
import jax.numpy as jnp
from jax import lax
from jax.experimental import pallas as pl
from jax.experimental.pallas import tpu as pltpu
from jax.experimental.pallas import tpu_sc as plsc

D_MODEL = 1024
DEPTH = 4
FOURIER_WIDTH = 256
N_FGROUPS = 4
FGROUP_DIM = 64
HEAD_DIM = 64
N_Q_HEADS = 12
N_KV_HEADS = 4
Q_PER_KV = 3
ATTN_WIDTH = N_Q_HEADS * HEAD_DIM
KV_WIDTH = N_KV_HEADS * HEAD_DIM
IN_WIDTH = FOURIER_WIDTH + ATTN_WIDTH + 2 * KV_WIDTH
WINDOW = 128
ROPE_THETA = 500000.0
ROT_DIM = 16
N_EXPERTS = 32
TOP_K = 4
D_FF = 512
EXPERT_BLOCK = 1024
SWIGLU_LIMIT = 7.0
SWIGLU_ALPHA = 1.702
NORM_EPS = 1e-5

LANES = 128
HALF = D_MODEL // 2
TOKEN_TILE = 512
Q_TILE = 256
KEY_SPAN = Q_TILE + 2 * WINDOW
SC_CHUNK = 128
SC_WORKERS = 32
VMEM_LIMIT = 56 * 1024 * 1024
NEG_BIG = -1e30
HI_MASK = -65536
ROUTE_ROWS = 16

_f32 = jnp.float32
_bf16 = jnp.bfloat16


def _pack_halves(a, b):
    ua = lax.bitcast_convert_type(a.astype(_bf16).astype(_f32), jnp.int32)
    ub = lax.bitcast_convert_type(b.astype(_bf16).astype(_f32), jnp.int32)
    return ua | lax.shift_right_logical(ub, 16)


def _unpack_halves(p):
    a = lax.bitcast_convert_type(p & HI_MASK, _f32)
    b = lax.bitcast_convert_type(lax.shift_left(p, 16), _f32)
    return a, b


def _rms(x, g):
    ms = jnp.mean(x * x, axis=-1, keepdims=True)
    return x * lax.rsqrt(ms + NORM_EPS) * g


def _mod_kernel(c_ref, w_ref, b_ref, o_ref):
    c = c_ref[...]
    s = c * (1.0 / (1.0 + jnp.exp(-c)))
    o_ref[0] = jnp.dot(s.astype(_bf16), w_ref[0].astype(_bf16), preferred_element_type=_f32) + b_ref[0]


def _modulation(c_all, ada_w, ada_b):
    nb = c_all.shape[0]
    ncol = ada_w.shape[2] // D_MODEL
    return pl.pallas_call(
        _mod_kernel,
        out_shape=jax.ShapeDtypeStruct((DEPTH, nb, 6 * D_MODEL), _f32),
        grid=(DEPTH, ncol),
        in_specs=[
            pl.BlockSpec((nb, D_MODEL), lambda l, j: (0, 0)),
            pl.BlockSpec((1, D_MODEL, D_MODEL), lambda l, j: (l, 0, j)),
            pl.BlockSpec((1, 1, D_MODEL), lambda l, j: (l, 0, j)),
        ],
        out_specs=pl.BlockSpec((1, nb, D_MODEL), lambda l, j: (l, 0, j)),
        name="modulation",
    )(c_all, ada_w, ada_b.reshape(DEPTH, 1, 6 * D_MODEL))


def _moe_combine(yk_ref, wc_ref):
    wc = wc_ref[...]
    acc_a = None
    acc_b = None
    for k in range(TOP_K):
        w = wc[:, k:k + 1]
        a, b = _unpack_halves(yk_ref[k])
        acc_a = w * a if acc_a is None else acc_a + w * a
        acc_b = w * b if acc_b is None else acc_b + w * b
    return jnp.concatenate([acc_a, acc_b], axis=1)


def _inproj_body(x, sh_ref, sc_ref, ng_ref, w_ref, b_ref, ab_ref, rc_ref, rs1_ref, rs2_ref,
                 y_ref, q_ref, k_ref, v_ref):
    h = _rms(x, ng_ref[...]) * (1.0 + sc_ref[0]) + sh_ref[0]
    z = jnp.dot(h.astype(_bf16), w_ref[...], preferred_element_type=_f32) + b_ref[...]
    f = z[:, :FOURIER_WIDTH]
    y_ref[...] = jnp.dot(f.astype(_bf16), ab_ref[...], preferred_element_type=_f32).astype(_bf16)
    rc = rc_ref[...]
    rs1 = rs1_ref[...]
    rs2 = rs2_ref[...]

    def rope(t):
        return t * rc + pltpu.roll(t, LANES - ROT_DIM // 2, 1) * rs1 + pltpu.roll(t, ROT_DIM // 2, 1) * rs2

    q0 = FOURIER_WIDTH
    for c in range(ATTN_WIDTH // LANES):
        t = z[:, q0 + c * LANES:q0 + (c + 1) * LANES]
        q_ref[:, c * LANES:(c + 1) * LANES] = (rope(t) * (HEAD_DIM ** -0.5)).astype(_bf16)
    k0 = q0 + ATTN_WIDTH
    for c in range(KV_WIDTH // LANES):
        t = z[:, k0 + c * LANES:k0 + (c + 1) * LANES]
        k_ref[:, c * LANES:(c + 1) * LANES] = rope(t).astype(_bf16)
    v_ref[...] = z[:, k0 + KV_WIDTH:].astype(_bf16)


def _inproj_first_kernel(x_ref, *refs):
    _inproj_body(x_ref[...], *refs)


def _inproj_combine_kernel(x1_ref, yk_ref, wc_ref, g2_ref, *refs):
    x = x1_ref[...] + g2_ref[0] * _moe_combine(yk_ref, wc_ref)
    refs[-1][...] = x
    _inproj_body(x, *refs[:-1])


def _inproj(seq, x, moe, sh1, sc1, norm_g, w_in, b_in, ab, rope_tabs):
    t_tokens = x.shape[0]
    tm = TOKEN_TILE
    tiles_per_seq = seq // tm
    row = lambda i: (i, 0)
    per_seq = lambda i: (i // tiles_per_seq, 0, 0)
    const = lambda i: (0, 0)
    pos = lambda i: (i % tiles_per_seq, 0)
    in_specs = [pl.BlockSpec((tm, D_MODEL), row)]
    args = [x]
    if moe is not None:
        yk, wcol, g2 = moe
        in_specs += [pl.BlockSpec((TOP_K, tm, HALF), lambda i: (0, i, 0)),
                     pl.BlockSpec((tm, LANES), row),
                     pl.BlockSpec((1, 1, D_MODEL), per_seq)]
        args += [yk, wcol, g2]
    in_specs += [pl.BlockSpec((1, 1, D_MODEL), per_seq), pl.BlockSpec((1, 1, D_MODEL), per_seq),
                 pl.BlockSpec((1, D_MODEL), const),
                 pl.BlockSpec((D_MODEL, IN_WIDTH), const), pl.BlockSpec((1, IN_WIDTH), const),
                 pl.BlockSpec((FOURIER_WIDTH, 2 * FOURIER_WIDTH), const),
                 pl.BlockSpec((tm, LANES), pos), pl.BlockSpec((tm, LANES), pos), pl.BlockSpec((tm, LANES), pos)]
    args += [sh1, sc1, norm_g, w_in, b_in, ab, *rope_tabs]
    out_shape = [jax.ShapeDtypeStruct((t_tokens, 2 * FOURIER_WIDTH), _bf16),
                 jax.ShapeDtypeStruct((t_tokens, ATTN_WIDTH), _bf16),
                 jax.ShapeDtypeStruct((t_tokens, KV_WIDTH), _bf16),
                 jax.ShapeDtypeStruct((t_tokens, KV_WIDTH), _bf16)]
    out_specs = [pl.BlockSpec((tm, 2 * FOURIER_WIDTH), row), pl.BlockSpec((tm, ATTN_WIDTH), row),
                 pl.BlockSpec((tm, KV_WIDTH), row), pl.BlockSpec((tm, KV_WIDTH), row)]
    if moe is not None:
        out_shape.append(jax.ShapeDtypeStruct((t_tokens, D_MODEL), _f32))
        out_specs.append(pl.BlockSpec((tm, D_MODEL), row))
    return pl.pallas_call(
        _inproj_first_kernel if moe is None else _inproj_combine_kernel,
        out_shape=out_shape,
        grid=(t_tokens // tm,),
        in_specs=in_specs,
        out_specs=out_specs,
        compiler_params=pltpu.CompilerParams(dimension_semantics=("parallel",), vmem_limit_bytes=VMEM_LIMIT),
        name="inproj",
    )(*args)


def _fourier_kernel(y_ref, ce_ref, se_ref, co_ref, so_ref, o_ref):
    half = y_ref.shape[0] // 2
    lo = y_ref[:half, :].astype(_f32)
    hi = y_ref[half:, :].astype(_f32)
    ye = (lo + hi).astype(_bf16)
    yo = (lo - hi).astype(_bf16)
    w = FOURIER_WIDTH
    even = (jnp.dot(ce_ref[...], ye[:, :w], preferred_element_type=_f32)
            + jnp.dot(se_ref[...], ye[:, w:], preferred_element_type=_f32))
    odd = (jnp.dot(co_ref[...], yo[:, :w], preferred_element_type=_f32)
           + jnp.dot(so_ref[...], yo[:, w:], preferred_element_type=_f32))
    for c in range(FOURIER_WIDTH // LANES):
        o_ref[c, pl.ds(0, half, stride=2), :] = even[:, c * LANES:(c + 1) * LANES]
        o_ref[c, pl.ds(1, half, stride=2), :] = odd[:, c * LANES:(c + 1) * LANES]


def _fourier(seq, y, dft):
    t_tokens = y.shape[0]
    half = seq // 2
    const = lambda b: (0, 0)
    return pl.pallas_call(
        _fourier_kernel,
        out_shape=jax.ShapeDtypeStruct((FOURIER_WIDTH // LANES, t_tokens, LANES), _f32),
        grid=(t_tokens // seq,),
        in_specs=[pl.BlockSpec((seq, 2 * FOURIER_WIDTH), lambda b: (b, 0))]
        + [pl.BlockSpec((half, half), const)] * 4,
        out_specs=pl.BlockSpec((FOURIER_WIDTH // LANES, seq, LANES), lambda b: (0, b, 0)),
        compiler_params=pltpu.CompilerParams(dimension_semantics=("parallel",), vmem_limit_bytes=VMEM_LIMIT),
        name="fourier",
    )(y, *dft)


def _dft_tables(seq):
    half = seq // 2
    j = jnp.arange(half, dtype=jnp.int32)[:, None]
    k = jnp.arange(half, dtype=jnp.int32)[None, :]
    ang_e = ((2 * j * k) % seq).astype(_f32) * (2.0 * jnp.pi / seq)
    ang_o = (((2 * j + 1) * k) % seq).astype(_f32) * (2.0 * jnp.pi / seq)
    return (jnp.cos(ang_e).astype(_bf16), jnp.sin(ang_e).astype(_bf16),
            jnp.cos(ang_o).astype(_bf16), jnp.sin(ang_o).astype(_bf16))


def _channel_dft_fold(w_fmix, seq):
    c = jnp.arange(FGROUP_DIM, dtype=jnp.int32)
    ang = ((c[:, None] * c[None, :]) % FGROUP_DIM).astype(_f32) * (2.0 * jnp.pi / FGROUP_DIM)
    scale = (seq * FGROUP_DIM) ** -0.5
    hp = lax.Precision.HIGHEST
    a = jnp.einsum('cm,gmd->gcd', jnp.cos(ang) * scale, w_fmix, precision=hp)
    b = jnp.einsum('cm,gmd->gcd', -jnp.sin(ang) * scale, w_fmix, precision=hp)
    eye = jnp.eye(N_FGROUPS, dtype=_f32)
    bd = lambda m: jnp.einsum('gcd,gh->gchd', m, eye).reshape(FOURIER_WIDTH, FOURIER_WIDTH)
    return jnp.concatenate([bd(a), bd(b)], axis=1).astype(_bf16)


def _rope_tables(seq):
    half = ROT_DIM // 2
    inv_freq = jnp.power(ROPE_THETA, -jnp.arange(0, ROT_DIM, 2, dtype=_f32) / ROT_DIM)
    ang = jnp.arange(seq, dtype=_f32)[:, None] * inv_freq[None, :]
    cos, sin = jnp.cos(ang), jnp.sin(ang)
    ones = jnp.ones((seq, HEAD_DIM - ROT_DIM), _f32)
    zeros = jnp.zeros((seq, HEAD_DIM - ROT_DIM), _f32)
    zh = jnp.zeros((seq, half), _f32)
    c1 = jnp.concatenate([cos, cos, ones], axis=1)
    s1 = jnp.concatenate([-sin, zh, zeros], axis=1)
    s2 = jnp.concatenate([zh, sin, zeros], axis=1)
    rep = LANES // HEAD_DIM
    return tuple(jnp.tile(t, (1, rep)) for t in (c1, s1, s2))


def _attn_kernel(sink_ref, q_ref, k_ref, v_ref, o_ref, kpad, vtb, ot):
    seq = k_ref.shape[0]
    nqb = seq // Q_TILE
    nkb = seq // WINDOW
    qb = pl.program_id(1)

    @pl.when(qb == 0)
    def _():
        zk = jnp.zeros((WINDOW, HEAD_DIM), _bf16)
        for h in range(N_KV_HEADS):
            kpad[h, :WINDOW, :] = zk
            kpad[h, WINDOW + seq:, :] = zk
            kpad[h, WINDOW:WINDOW + seq, :] = k_ref[:, h * HEAD_DIM:(h + 1) * HEAD_DIM]
        zv = jnp.zeros((KV_WIDTH, WINDOW), _bf16)
        vtb[0] = zv
        vtb[nkb + 1] = zv
        for j in range(nkb):
            vtb[j + 1] = v_ref[j * WINDOW:(j + 1) * WINDOW, :].astype(_f32).T.astype(_bf16)

    r0 = pl.multiple_of(qb * Q_TILE, Q_TILE)
    kb0 = qb * (Q_TILE // WINDOW)
    ka = lax.broadcasted_iota(jnp.int32, (WINDOW, WINDOW), 0)
    qc = lax.broadcasted_iota(jnp.int32, (WINDOW, WINDOW), 1)
    tri_ge = jnp.where(ka >= qc, 0.0, NEG_BIG).astype(_f32)
    tri_le = jnp.where(ka <= qc, 0.0, NEG_BIG).astype(_f32)
    bias_first = jnp.where(qb == 0, NEG_BIG, tri_ge)
    bias_last = jnp.where(qb == nqb - 1, NEG_BIG, tri_le)
    zero_blk = jnp.zeros((WINDOW, WINDOW), _bf16)
    vwin = jnp.concatenate([vtb[kb0 + j] for j in range(KEY_SPAN // WINDOW)], axis=1)

    def softmax_col(blocks, sink):
        m = blocks[0]
        for b in blocks[1:]:
            m = jnp.maximum(m, b)
        m = jnp.maximum(jnp.max(m, axis=0, keepdims=True), sink)
        ps = [jnp.exp(b - m) for b in blocks]
        tot = ps[0]
        for p in ps[1:]:
            tot = tot + p
        return ps, jnp.sum(tot, axis=0, keepdims=True) + jnp.exp(sink - m)

    w = WINDOW

    def scores(g):
        kh = kpad[g // Q_PER_KV, pl.ds(r0, KEY_SPAN), :]
        qg = q_ref[:, g * HEAD_DIM:(g + 1) * HEAD_DIM]
        return lax.dot_general(kh, qg, (((1,), (1,)), ((), ())), preferred_element_type=_f32)

    s = scores(0)
    for g in range(N_Q_HEADS):
        s_next = scores(g + 1) if g + 1 < N_Q_HEADS else None
        sink = sink_ref[g]
        h = g // Q_PER_KV
        p0, l0 = softmax_col([s[0:w, 0:w] + bias_first, s[w:2 * w, 0:w], s[2 * w:3 * w, 0:w] + tri_le], sink)
        p1, l1 = softmax_col([s[w:2 * w, w:] + tri_ge, s[2 * w:3 * w, w:], s[3 * w:, w:] + bias_last], sink)
        col0 = jnp.concatenate([p.astype(_bf16) for p in p0] + [zero_blk], axis=0)
        col1 = jnp.concatenate([zero_blk] + [p.astype(_bf16) for p in p1], axis=0)
        pt = jnp.concatenate([col0, col1], axis=1)
        o_t = jnp.dot(vwin[h * HEAD_DIM:(h + 1) * HEAD_DIM, :], pt, preferred_element_type=_f32)
        ot[g * HEAD_DIM:(g + 1) * HEAD_DIM, :] = o_t / jnp.concatenate([l0, l1], axis=1)
        s = s_next
    for c in range(ATTN_WIDTH // LANES):
        o_ref[:, c * LANES:(c + 1) * LANES] = ot[c * LANES:(c + 1) * LANES, :].T.astype(_bf16)


def _attention(seq, q, k, v, sinks):
    t_tokens = q.shape[0]
    nqb = seq // Q_TILE
    return pl.pallas_call(
        _attn_kernel,
        out_shape=jax.ShapeDtypeStruct((t_tokens, ATTN_WIDTH), _bf16),
        grid=(t_tokens // seq, nqb),
        in_specs=[pl.BlockSpec(memory_space=pltpu.SMEM),
                  pl.BlockSpec((Q_TILE, ATTN_WIDTH), lambda b, j: (b * nqb + j, 0)),
                  pl.BlockSpec((seq, KV_WIDTH), lambda b, j: (b, 0)),
                  pl.BlockSpec((seq, KV_WIDTH), lambda b, j: (b, 0))],
        out_specs=pl.BlockSpec((Q_TILE, ATTN_WIDTH), lambda b, j: (b * nqb + j, 0)),
        scratch_shapes=[pltpu.VMEM((N_KV_HEADS, seq + 2 * WINDOW, HEAD_DIM), _bf16),
                        pltpu.VMEM((seq // WINDOW + 2, KV_WIDTH, WINDOW), _bf16),
                        pltpu.VMEM((ATTN_WIDTH, Q_TILE), _f32)],
        compiler_params=pltpu.CompilerParams(dimension_semantics=("parallel", "arbitrary"),
                                             vmem_limit_bytes=VMEM_LIMIT),
        name="attention",
    )(sinks, q, k, v)


def _outproj_kernel(x_ref, mf_ref, at_ref, g1_ref, sh_ref, sc_ref, ng_ref, wof_ref, woa_ref, bo_ref,
                    wr_ref, br_ref, tri_ref, x1_ref, hp_ref, rt_ref, wc_ref, cnt_ref, carry_ref):
    i = pl.program_id(0)

    @pl.when(i == 0)
    def _():
        carry_ref[...] = jnp.zeros_like(carry_ref)

    mf = jnp.concatenate([mf_ref[c] for c in range(FOURIER_WIDTH // LANES)], axis=1)
    mix = (jnp.dot(mf.astype(_bf16), wof_ref[...], preferred_element_type=_f32)
           + jnp.dot(at_ref[...], woa_ref[...], preferred_element_type=_f32) + bo_ref[...])
    x1 = x_ref[...] + g1_ref[0] * mix
    x1_ref[...] = x1
    h = _rms(x1, ng_ref[...]) * (1.0 + sc_ref[0]) + sh_ref[0]
    hp_ref[...] = _pack_halves(h[:, :HALF], h[:, HALF:])
    logits = lax.dot_general(wr_ref[...], h.astype(_bf16), (((1,), (1,)), ((), ())),
                             preferred_element_type=_f32) + br_ref[...]
    tm = logits.shape[1]
    erow = lax.broadcasted_iota(jnp.int32, logits.shape, 0)
    work = logits
    hots, vals, idxs = [], [], []
    for _k in range(TOP_K):
        mx = jnp.max(work, axis=0, keepdims=True)
        ix = jnp.min(jnp.where(work == mx, erow, N_EXPERTS), axis=0, keepdims=True)
        hot = erow == ix
        work = jnp.where(hot, -jnp.inf, work)
        hots.append(hot)
        vals.append(mx)
        idxs.append(ix)
    es = [jnp.exp(v - vals[0]) for v in vals]
    den = es[0] + es[1] + es[2] + es[3]
    member = jnp.zeros(logits.shape, _f32)
    for hot in hots:
        member = member + hot.astype(_f32)
    carry = carry_ref[...]
    before = (jnp.dot(member.astype(_bf16), tri_ref[...], preferred_element_type=_f32)
              + jnp.concatenate([carry] * (tm // LANES), axis=1))
    r16 = lax.broadcasted_iota(jnp.int32, (ROUTE_ROWS, tm), 0)
    r128 = lax.broadcasted_iota(jnp.int32, (LANES, tm), 0)
    slab = jnp.zeros((ROUTE_ROWS, tm), jnp.int32)
    wrows = jnp.zeros((LANES, tm), _f32)
    for k in range(TOP_K):
        rank = jnp.sum(jnp.where(hots[k], before, 0.0), axis=0, keepdims=True).astype(jnp.int32)
        slab = jnp.where(r16 == k, idxs[k], slab)
        slab = jnp.where(r16 == TOP_K + k, rank, slab)
        wrows = jnp.where(r128 == k, es[k] / den, wrows)
    rt_ref[...] = slab
    wc_ref[...] = wrows.T
    carry = carry + jnp.broadcast_to(jnp.sum(member, axis=1, keepdims=True), carry.shape)
    carry_ref[...] = carry
    cnt_ref[...] = carry.astype(jnp.int32)


def _outproj(seq, x, mf, attn, g1, sh2, sc2, norm_g, wo_f, wo_a, b_o, w_r, b_r, tri):
    t_tokens = x.shape[0]
    tm = TOKEN_TILE
    tiles_per_seq = seq // tm
    row = lambda i: (i, 0)
    per_seq = lambda i: (i // tiles_per_seq, 0, 0)
    const = lambda i: (0, 0)
    return pl.pallas_call(
        _outproj_kernel,
        out_shape=[jax.ShapeDtypeStruct((t_tokens, D_MODEL), _f32),
                   jax.ShapeDtypeStruct((t_tokens, HALF), jnp.int32),
                   jax.ShapeDtypeStruct((ROUTE_ROWS, t_tokens), jnp.int32),
                   jax.ShapeDtypeStruct((t_tokens, LANES), _f32),
                   jax.ShapeDtypeStruct((N_EXPERTS, LANES), jnp.int32)],
        grid=(t_tokens // tm,),
        in_specs=[pl.BlockSpec((tm, D_MODEL), row),
                  pl.BlockSpec((FOURIER_WIDTH // LANES, tm, LANES), lambda i: (0, i, 0)),
                  pl.BlockSpec((tm, ATTN_WIDTH), row),
                  pl.BlockSpec((1, 1, D_MODEL), per_seq), pl.BlockSpec((1, 1, D_MODEL), per_seq),
                  pl.BlockSpec((1, 1, D_MODEL), per_seq),
                  pl.BlockSpec((1, D_MODEL), const),
                  pl.BlockSpec((FOURIER_WIDTH, D_MODEL), const), pl.BlockSpec((ATTN_WIDTH, D_MODEL), const),
                  pl.BlockSpec((1, D_MODEL), const),
                  pl.BlockSpec((N_EXPERTS, D_MODEL), const), pl.BlockSpec((N_EXPERTS, tm), const),
                  pl.BlockSpec((tm, tm), const)],
        out_specs=[pl.BlockSpec((tm, D_MODEL), row), pl.BlockSpec((tm, HALF), row),
                   pl.BlockSpec((ROUTE_ROWS, tm), lambda i: (0, i)), pl.BlockSpec((tm, LANES), row),
                   pl.BlockSpec((N_EXPERTS, LANES), const)],
        scratch_shapes=[pltpu.VMEM((N_EXPERTS, LANES), _f32)],
        compiler_params=pltpu.CompilerParams(dimension_semantics=("arbitrary",), vmem_limit_bytes=VMEM_LIMIT),
        name="outproj",
    )(x, mf, attn, g1, sh2, sc2, norm_g, wo_f, wo_a, b_o, w_r, b_r, tri)


def _expert_kernel(blk_ref, bexp_ref, nact_ref, x_ref, wgu_ref, bgu_ref, wd_ref, bd_ref, y_ref, wgu_bf, wd_bf):
    i = pl.program_id(0)
    prev = bexp_ref[jnp.maximum(i - 1, 0)]

    @pl.when((i == 0) | (bexp_ref[i] != prev))
    def _():
        wgu_bf[...] = wgu_ref[0].astype(_bf16)
        wd_bf[...] = wd_ref[0].astype(_bf16)

    @pl.when(i < nact_ref[0])
    def _():
        xa, xb = _unpack_halves(x_ref[...])
        gu = (jnp.dot(xa.astype(_bf16), wgu_bf[:HALF, :], preferred_element_type=_f32)
              + jnp.dot(xb.astype(_bf16), wgu_bf[HALF:, :], preferred_element_type=_f32) + bgu_ref[0])
        g = jnp.minimum(gu[:, :D_FF], SWIGLU_LIMIT)
        u = jnp.clip(gu[:, D_FF:], -SWIGLU_LIMIT, SWIGLU_LIMIT)
        act = (u + 1.0) * (g * (1.0 / (1.0 + jnp.exp(-SWIGLU_ALPHA * g))))
        y = jnp.dot(act.astype(_bf16), wd_bf[...], preferred_element_type=_f32) + bd_ref[0]
        y_ref[...] = _pack_halves(y[:, :HALF], y[:, HALF:])


def _experts(xs, blk, bexp, nact, w_gu, b_gu, w_down, b_down):
    p_rows = xs.shape[0]
    nblk = p_rows // EXPERT_BLOCK
    grid_spec = pltpu.PrefetchScalarGridSpec(
        num_scalar_prefetch=3,
        grid=(nblk,),
        in_specs=[pl.BlockSpec((EXPERT_BLOCK, HALF), lambda i, blk, be, na: (blk[i], 0)),
                  pl.BlockSpec((1, D_MODEL, 2 * D_FF), lambda i, blk, be, na: (be[i], 0, 0)),
                  pl.BlockSpec((1, 1, 2 * D_FF), lambda i, blk, be, na: (be[i], 0, 0)),
                  pl.BlockSpec((1, D_FF, D_MODEL), lambda i, blk, be, na: (be[i], 0, 0)),
                  pl.BlockSpec((1, 1, D_MODEL), lambda i, blk, be, na: (be[i], 0, 0))],
        out_specs=pl.BlockSpec((EXPERT_BLOCK, HALF), lambda i, blk, be, na: (blk[i], 0)),
        scratch_shapes=[pltpu.VMEM((D_MODEL, 2 * D_FF), _bf16), pltpu.VMEM((D_FF, D_MODEL), _bf16)])
    return pl.pallas_call(
        _expert_kernel,
        out_shape=jax.ShapeDtypeStruct((p_rows, HALF), jnp.int32),
        grid_spec=grid_spec,
        compiler_params=pltpu.CompilerParams(dimension_semantics=("arbitrary",), vmem_limit_bytes=VMEM_LIMIT),
        name="experts",
    )(blk, bexp, nact, xs, w_gu, b_gu.reshape(N_EXPERTS, 1, 2 * D_FF), w_down,
      b_down.reshape(N_EXPERTS, 1, D_MODEL))


def _sc_mesh():
    return plsc.VectorSubcoreMesh(core_axis_name="c", subcore_axis_name="s")


def _sc_worker():
    return lax.axis_index("s") * 2 + lax.axis_index("c")


def _dispatch(hp, dest_t, p_rows):
    n_chunks = dest_t.shape[0]
    per_worker = n_chunks // SC_WORKERS

    @functools.partial(
        pl.kernel, mesh=_sc_mesh(),
        out_type=jax.ShapeDtypeStruct((p_rows, HALF), jnp.int32),
        scratch_types=[pltpu.VMEM((TOP_K, SC_CHUNK), jnp.int32),
                       pltpu.VMEM((SC_CHUNK, HALF), jnp.int32),
                       pltpu.SemaphoreType.DMA],
        name="dispatch")
    def run(hp_hbm, dest_hbm, xs_hbm, idx_v, rows_v, sem):
        base = _sc_worker() * per_worker

        @pl.loop(0, per_worker)
        def _(j):
            chunk = base + j
            pltpu.sync_copy(dest_hbm.at[chunk], idx_v)
            pltpu.sync_copy(hp_hbm.at[pl.ds(chunk * SC_CHUNK, SC_CHUNK)], rows_v)
            copies = [pltpu.async_copy(rows_v, xs_hbm.at[idx_v.at[k]], sem) for k in range(TOP_K)]
            for cp in copies:
                cp.wait()

    return run(hp, dest_t)


def _gather_rows(y, dest_t, t_tokens):
    n_chunks = dest_t.shape[0]
    per_worker = n_chunks // SC_WORKERS

    @functools.partial(
        pl.kernel, mesh=_sc_mesh(),
        out_type=jax.ShapeDtypeStruct((TOP_K, t_tokens, HALF), jnp.int32),
        scratch_types=[pltpu.VMEM((TOP_K, SC_CHUNK), jnp.int32),
                       pltpu.VMEM((SC_CHUNK, HALF), jnp.int32),
                       pltpu.SemaphoreType.DMA],
        name="gather_rows")
    def run(y_hbm, dest_hbm, yk_hbm, idx_v, rows_v, sem):
        base = _sc_worker() * per_worker

        @pl.loop(0, per_worker)
        def _(j):
            chunk = base + j
            pltpu.sync_copy(dest_hbm.at[chunk], idx_v)
            for k in range(TOP_K):
                pltpu.async_copy(y_hbm.at[idx_v.at[k]], rows_v, sem).wait()
                pltpu.sync_copy(rows_v, yk_hbm.at[k, pl.ds(chunk * SC_CHUNK, SC_CHUNK)])

    return run(y, dest_t)


def _final_kernel(x1_ref, yk_ref, wc_ref, g2_ref, ng_ref, o_ref):
    x = x1_ref[...] + g2_ref[0] * _moe_combine(yk_ref, wc_ref)
    o_ref[...] = _rms(x, ng_ref[...])


def _final(seq, x1, yk, wcol, g2, final_g):
    t_tokens = x1.shape[0]
    tm = TOKEN_TILE
    tiles_per_seq = seq // tm
    row = lambda i: (i, 0)
    return pl.pallas_call(
        _final_kernel,
        out_shape=jax.ShapeDtypeStruct((t_tokens, D_MODEL), _f32),
        grid=(t_tokens // tm,),
        in_specs=[pl.BlockSpec((tm, D_MODEL), row),
                  pl.BlockSpec((TOP_K, tm, HALF), lambda i: (0, i, 0)),
                  pl.BlockSpec((tm, LANES), row),
                  pl.BlockSpec((1, 1, D_MODEL), lambda i: (i // tiles_per_seq, 0, 0)),
                  pl.BlockSpec((1, D_MODEL), lambda i: (0, 0))],
        out_specs=pl.BlockSpec((tm, D_MODEL), row),
        compiler_params=pltpu.CompilerParams(dimension_semantics=("parallel",), vmem_limit_bytes=VMEM_LIMIT),
        name="final",
    )(x1, yk, wcol, g2, final_g)


def _plan(route, counts, nblk):
    cnt = counts[:, 0]
    padded = ((cnt + EXPERT_BLOCK - 1) // EXPERT_BLOCK) * EXPERT_BLOCK
    pad_end = jnp.cumsum(padded)
    pad_start = pad_end - padded
    dest = jnp.take(pad_start, route[:TOP_K], axis=0) + route[TOP_K:2 * TOP_K]
    nact = jnp.maximum(pad_end[-1] // EXPERT_BLOCK, 1)
    blk = jnp.minimum(jnp.arange(nblk, dtype=jnp.int32), nact - 1)
    bexp = jnp.minimum(jnp.sum(pad_end[None, :] <= (blk * EXPERT_BLOCK)[:, None], axis=1), N_EXPERTS - 1)
    t_tokens = route.shape[1]
    dest_t = dest.reshape(TOP_K, t_tokens // SC_CHUNK, SC_CHUNK).transpose(1, 0, 2)
    return dest_t, blk.astype(jnp.int32), bexp.astype(jnp.int32), nact.reshape(1).astype(jnp.int32)


def _trunk(x_all, c_all, norm1_g, ada_w, ada_b, w_in, b_in, w_fmix, sinks, w_o, b_o,
           norm2_g, w_router, b_router, w_gu, b_gu, w_down, b_down, final_g):
    nb, seq, _ = x_all.shape
    t_tokens = nb * seq
    assert seq % TOKEN_TILE == 0 and seq % Q_TILE == 0 and seq >= KEY_SPAN
    assert t_tokens % (SC_CHUNK * SC_WORKERS) == 0
    nblk = (t_tokens * TOP_K) // EXPERT_BLOCK + N_EXPERTS
    p_rows = nblk * EXPERT_BLOCK

    mod = _modulation(c_all, ada_w, ada_b)
    mod = mod.reshape(DEPTH, nb, 6, 1, D_MODEL)
    rope_tabs = _rope_tables(seq)
    dft = _dft_tables(seq)
    tri = jnp.triu(jnp.ones((TOKEN_TILE, TOKEN_TILE), _f32), 1).astype(_bf16)

    x = x_all.reshape(t_tokens, D_MODEL)
    moe = None
    for l in range(DEPTH):
        sh1, sc1, g1, sh2, sc2, g2 = (mod[l, :, j] for j in range(6))
        ab = _channel_dft_fold(w_fmix[l], seq)
        outs = _inproj(seq, x, moe, sh1, sc1, norm1_g[l].reshape(1, D_MODEL), w_in[l].astype(_bf16),
                       b_in[l].reshape(1, IN_WIDTH), ab, rope_tabs)
        if moe is None:
            y, q, k, v = outs
        else:
            y, q, k, v, x = outs
        mf = _fourier(seq, y, dft)
        attn = _attention(seq, q, k, v, sinks[l])
        w_o_bf = w_o[l].astype(_bf16)
        w_r = w_router[l].T.astype(_bf16)
        b_r = jnp.broadcast_to(b_router[l][:, None], (N_EXPERTS, TOKEN_TILE))
        x1, hp, route, wcol, counts = _outproj(seq, x, mf, attn, g1, sh2, sc2, norm2_g[l].reshape(1, D_MODEL),
                                               w_o_bf[:FOURIER_WIDTH], w_o_bf[FOURIER_WIDTH:],
                                               b_o[l].reshape(1, D_MODEL), w_r, b_r, tri)
        dest_t, blk, bexp, nact = _plan(route, counts, nblk)
        xs = _dispatch(hp, dest_t, p_rows)
        ys = _experts(xs, blk, bexp, nact, w_gu[l], b_gu[l], w_down[l], b_down[l])
        yk = _gather_rows(ys, dest_t, t_tokens)
        x = x1
        moe = (yk, wcol, g2)
    out = _final(seq, x, moe[0], moe[1], moe[2], final_g.reshape(1, D_MODEL))
    return out.reshape(nb, seq, D_MODEL)


def kernel(x_prompt, x_sample, c_prompt, c_sample, norm1_g, ada_w, ada_b, w_in, b_in, w_fmix, sinks, w_o, b_o,
           norm2_g, w_router, b_router, w_gu, b_gu, w_down, b_down, final_g):
    ws = (norm1_g, ada_w, ada_b, w_in, b_in, w_fmix, sinks, w_o, b_o,
          norm2_g, w_router, b_router, w_gu, b_gu, w_down, b_down, final_g)
    return _trunk(x_prompt, c_prompt, *ws), _trunk(x_sample, c_sample, *ws)
```

```python
import functools

import jax
import jax.numpy as jnp
from jax import lax
from jax.experimental import pallas as pl
from jax.experimental.pallas import tpu as pltpu
from jax.experimental.pallas import tpu_sc as plsc

D_MODEL = 1024
DEPTH = 4
FOURIER_WIDTH = 256
N_FGROUPS = 4
FGROUP_DIM = 64
HEAD_DIM = 64
N_Q_HEADS = 12
N_KV_HEADS = 4
Q_PER_KV = 3
ATTN_WIDTH = N_Q_HEADS * HEAD_DIM
KV_WIDTH = N_KV_HEADS * HEAD_DIM
IN_WIDTH = FOURIER_WIDTH + ATTN_WIDTH + 2 * KV_WIDTH
WINDOW = 128
ROPE_THETA = 500000.0
ROT_DIM = 16
N_EXPERTS = 32
TOP_K = 4
D_FF = 512
EXPERT_BLOCK = 1024
EXPERT_SUB = 512
SWIGLU_LIMIT = 7.0
SWIGLU_ALPHA = 1.702
NORM_EPS = 1e-5

LANES = 128
HALF = D_MODEL // 2
TOKEN_TILE = 512
SUB_TILE = 256
Q_TILE = 256
KEY_SPAN = Q_TILE + 2 * WINDOW
SC_CHUNK = 128
SC_WORKERS = 32
VMEM_LIMIT = 56 * 1024 * 1024
NEG_BIG = -1e30
HI_MASK = -65536
ROUTE_ROWS = 16

_f32 = jnp.float32
_bf16 = jnp.bfloat16


def _pack_halves(a, b):
    ua = lax.bitcast_convert_type(a.astype(_bf16).astype(_f32), jnp.int32)
    ub = lax.bitcast_convert_type(b.astype(_bf16).astype(_f32), jnp.int32)
    return ua | lax.shift_right_logical(ub, 16)


def _unpack_halves(p):
    a = lax.bitcast_convert_type(p & HI_MASK, _f32)
    b = lax.bitcast_convert_type(lax.shift_left(p, 16), _f32)
    return a, b


def _rms(x, g):
    ms = jnp.mean(x * x, axis=-1, keepdims=True)
    return x * lax.rsqrt(ms + NORM_EPS) * g


def _mod_kernel(c_ref, w_ref, b_ref, o_ref):
    c = c_ref[...]
    s = c * (1.0 / (1.0 + jnp.exp(-c)))
    o_ref[0] = jnp.dot(s.astype(_bf16), w_ref[0].astype(_bf16), preferred_element_type=_f32) + b_ref[0]


def _modulation(c_all, ada_w, ada_b):
    nb = c_all.shape[0]
    ncol = ada_w.shape[2] // D_MODEL
    return pl.pallas_call(
        _mod_kernel,
        out_shape=jax.ShapeDtypeStruct((DEPTH, nb, 6 * D_MODEL), _f32),
        grid=(DEPTH, ncol),
        in_specs=[
            pl.BlockSpec((nb, D_MODEL), lambda l, j: (0, 0)),
            pl.BlockSpec((1, D_MODEL, D_MODEL), lambda l, j: (l, 0, j)),
            pl.BlockSpec((1, 1, D_MODEL), lambda l, j: (l, 0, j)),
        ],
        out_specs=pl.BlockSpec((1, nb, D_MODEL), lambda l, j: (l, 0, j)),
        name="modulation",
    )(c_all, ada_w, ada_b.reshape(DEPTH, 1, 6 * D_MODEL))


def _moe_combine(yk_ref, wc_ref, rows):
    wc = wc_ref[rows, :]
    acc_a = None
    acc_b = None
    for k in range(TOP_K):
        w = wc[:, k:k + 1]
        a, b = _unpack_halves(yk_ref[k, rows, :])
        acc_a = w * a if acc_a is None else acc_a + w * a
        acc_b = w * b if acc_b is None else acc_b + w * b
    return jnp.concatenate([acc_a, acc_b], axis=1)


def _inproj_body(rows, x, sh_ref, sc_ref, ng_ref, w_ref, b_ref, ab_ref, rc_ref, rs1_ref, rs2_ref,
                 y_ref, q_ref, k_ref, v_ref):
    h = _rms(x, ng_ref[...]) * (1.0 + sc_ref[0]) + sh_ref[0]
    z = jnp.dot(h.astype(_bf16), w_ref[...], preferred_element_type=_f32) + b_ref[...]
    f = z[:, :FOURIER_WIDTH]
    y_ref[rows, :] = jnp.dot(f.astype(_bf16), ab_ref[...], preferred_element_type=_f32).astype(_bf16)
    rc = rc_ref[rows, :]
    rs1 = rs1_ref[rows, :]
    rs2 = rs2_ref[rows, :]

    def rope(t):
        return t * rc + pltpu.roll(t, LANES - ROT_DIM // 2, 1) * rs1 + pltpu.roll(t, ROT_DIM // 2, 1) * rs2

    q0 = FOURIER_WIDTH
    for c in range(ATTN_WIDTH // LANES):
        t = z[:, q0 + c * LANES:q0 + (c + 1) * LANES]
        q_ref[rows, c * LANES:(c + 1) * LANES] = (rope(t) * (HEAD_DIM ** -0.5)).astype(_bf16)
    k0 = q0 + ATTN_WIDTH
    for c in range(KV_WIDTH // LANES):
        t = z[:, k0 + c * LANES:k0 + (c + 1) * LANES]
        k_ref[rows, c * LANES:(c + 1) * LANES] = rope(t).astype(_bf16)
    v_ref[rows, :] = z[:, k0 + KV_WIDTH:].astype(_bf16)


def _sub_tiles(n_rows):
    return [pl.ds(r, SUB_TILE) for r in range(0, n_rows, SUB_TILE)]


def _inproj_first_kernel(x_ref, *refs):
    for rows in _sub_tiles(x_ref.shape[0]):
        _inproj_body(rows, x_ref[rows, :], *refs)


def _inproj_combine_kernel(x1_ref, yk_ref, wc_ref, g2_ref, *refs):
    for rows in _sub_tiles(x1_ref.shape[0]):
        x = x1_ref[rows, :] + g2_ref[0] * _moe_combine(yk_ref, wc_ref, rows)
        refs[-1][rows, :] = x
        _inproj_body(rows, x, *refs[:-1])


def _inproj(seq, x, moe, sh1, sc1, norm_g, w_in, b_in, ab, rope_tabs):
    t_tokens = x.shape[0]
    tm = TOKEN_TILE
    tiles_per_seq = seq // tm
    row = lambda i: (i, 0)
    per_seq = lambda i: (i // tiles_per_seq, 0, 0)
    const = lambda i: (0, 0)
    pos = lambda i: (i % tiles_per_seq, 0)
    in_specs = [pl.BlockSpec((tm, D_MODEL), row)]
    args = [x]
    if moe is not None:
        yk, wcol, g2 = moe
        in_specs += [pl.BlockSpec((TOP_K, tm, HALF), lambda i: (0, i, 0)),
                     pl.BlockSpec((tm, LANES), row),
                     pl.BlockSpec((1, 1, D_MODEL), per_seq)]
        args += [yk, wcol, g2]
    in_specs += [pl.BlockSpec((1, 1, D_MODEL), per_seq), pl.BlockSpec((1, 1, D_MODEL), per_seq),
                 pl.BlockSpec((1, D_MODEL), const),
                 pl.BlockSpec((D_MODEL, IN_WIDTH), const), pl.BlockSpec((1, IN_WIDTH), const),
                 pl.BlockSpec((FOURIER_WIDTH, 2 * FOURIER_WIDTH), const),
                 pl.BlockSpec((tm, LANES), pos), pl.BlockSpec((tm, LANES), pos), pl.BlockSpec((tm, LANES), pos)]
    args += [sh1, sc1, norm_g, w_in, b_in, ab, *rope_tabs]
    out_shape = [jax.ShapeDtypeStruct((t_tokens, 2 * FOURIER_WIDTH), _bf16),
                 jax.ShapeDtypeStruct((t_tokens, ATTN_WIDTH), _bf16),
                 jax.ShapeDtypeStruct((t_tokens, KV_WIDTH), _bf16),
                 jax.ShapeDtypeStruct((t_tokens, KV_WIDTH), _bf16)]
    out_specs = [pl.BlockSpec((tm, 2 * FOURIER_WIDTH), row), pl.BlockSpec((tm, ATTN_WIDTH), row),
                 pl.BlockSpec((tm, KV_WIDTH), row), pl.BlockSpec((tm, KV_WIDTH), row)]
    if moe is not None:
        out_shape.append(jax.ShapeDtypeStruct((t_tokens, D_MODEL), _f32))
        out_specs.append(pl.BlockSpec((tm, D_MODEL), row))
    return pl.pallas_call(
        _inproj_first_kernel if moe is None else _inproj_combine_kernel,
        out_shape=out_shape,
        grid=(t_tokens // tm,),
        in_specs=in_specs,
        out_specs=out_specs,
        compiler_params=pltpu.CompilerParams(dimension_semantics=("parallel",), vmem_limit_bytes=VMEM_LIMIT),
        name="inproj",
    )(*args)


def _fourier_kernel(y_ref, ce_ref, se_ref, co_ref, so_ref, o_ref):
    half = y_ref.shape[0] // 2
    lo = y_ref[:half, :].astype(_f32)
    hi = y_ref[half:, :].astype(_f32)
    ye = (lo + hi).astype(_bf16)
    yo = (lo - hi).astype(_bf16)
    w = FOURIER_WIDTH
    even = (jnp.dot(ce_ref[...], ye[:, :w], preferred_element_type=_f32)
            + jnp.dot(se_ref[...], ye[:, w:], preferred_element_type=_f32))
    odd = (jnp.dot(co_ref[...], yo[:, :w], preferred_element_type=_f32)
           + jnp.dot(so_ref[...], yo[:, w:], preferred_element_type=_f32))
    for c in range(FOURIER_WIDTH // LANES):
        o_ref[c, pl.ds(0, half, stride=2), :] = even[:, c * LANES:(c + 1) * LANES]
        o_ref[c, pl.ds(1, half, stride=2), :] = odd[:, c * LANES:(c + 1) * LANES]


def _fourier(seq, y, dft):
    t_tokens = y.shape[0]
    half = seq // 2
    const = lambda b: (0, 0)
    return pl.pallas_call(
        _fourier_kernel,
        out_shape=jax.ShapeDtypeStruct((FOURIER_WIDTH // LANES, t_tokens, LANES), _f32),
        grid=(t_tokens // seq,),
        in_specs=[pl.BlockSpec((seq, 2 * FOURIER_WIDTH), lambda b: (b, 0))]
        + [pl.BlockSpec((half, half), const)] * 4,
        out_specs=pl.BlockSpec((FOURIER_WIDTH // LANES, seq, LANES), lambda b: (0, b, 0)),
        compiler_params=pltpu.CompilerParams(dimension_semantics=("parallel",), vmem_limit_bytes=VMEM_LIMIT),
        name="fourier",
    )(y, *dft)


def _dft_tables(seq):
    half = seq // 2
    j = jnp.arange(half, dtype=jnp.int32)[:, None]
    k = jnp.arange(half, dtype=jnp.int32)[None, :]
    ang_e = ((2 * j * k) % seq).astype(_f32) * (2.0 * jnp.pi / seq)
    ang_o = (((2 * j + 1) * k) % seq).astype(_f32) * (2.0 * jnp.pi / seq)
    return (jnp.cos(ang_e).astype(_bf16), jnp.sin(ang_e).astype(_bf16),
            jnp.cos(ang_o).astype(_bf16), jnp.sin(ang_o).astype(_bf16))


def _channel_dft_fold(w_fmix, seq):
    c = jnp.arange(FGROUP_DIM, dtype=jnp.int32)
    ang = ((c[:, None] * c[None, :]) % FGROUP_DIM).astype(_f32) * (2.0 * jnp.pi / FGROUP_DIM)
    scale = (seq * FGROUP_DIM) ** -0.5
    hp = lax.Precision.HIGHEST
    a = jnp.einsum('cm,gmd->gcd', jnp.cos(ang) * scale, w_fmix, precision=hp)
    b = jnp.einsum('cm,gmd->gcd', -jnp.sin(ang) * scale, w_fmix, precision=hp)
    eye = jnp.eye(N_FGROUPS, dtype=_f32)
    bd = lambda m: jnp.einsum('gcd,gh->gchd', m, eye).reshape(FOURIER_WIDTH, FOURIER_WIDTH)
    return jnp.concatenate([bd(a), bd(b)], axis=1).astype(_bf16)


def _rope_tables(seq):
    half = ROT_DIM // 2
    inv_freq = jnp.power(ROPE_THETA, -jnp.arange(0, ROT_DIM, 2, dtype=_f32) / ROT_DIM)
    ang = jnp.arange(seq, dtype=_f32)[:, None] * inv_freq[None, :]
    cos, sin = jnp.cos(ang), jnp.sin(ang)
    ones = jnp.ones((seq, HEAD_DIM - ROT_DIM), _f32)
    zeros = jnp.zeros((seq, HEAD_DIM - ROT_DIM), _f32)
    zh = jnp.zeros((seq, half), _f32)
    c1 = jnp.concatenate([cos, cos, ones], axis=1)
    s1 = jnp.concatenate([-sin, zh, zeros], axis=1)
    s2 = jnp.concatenate([zh, sin, zeros], axis=1)
    rep = LANES // HEAD_DIM
    return tuple(jnp.tile(t, (1, rep)) for t in (c1, s1, s2))


def _attn_kernel(sink_ref, q_ref, k_ref, v_ref, o_ref, kpad, vtb, ot):
    seq = k_ref.shape[0]
    nqb = seq // Q_TILE
    nkb = seq // WINDOW
    qb = pl.program_id(1)

    @pl.when(qb == 0)
    def _():
        zk = jnp.zeros((WINDOW, HEAD_DIM), _bf16)
        for h in range(N_KV_HEADS):
            kpad[h, :WINDOW, :] = zk
            kpad[h, WINDOW + seq:, :] = zk
            kpad[h, WINDOW:WINDOW + seq, :] = k_ref[:, h * HEAD_DIM:(h + 1) * HEAD_DIM]
        zv = jnp.zeros((KV_WIDTH, WINDOW), _bf16)
        vtb[0] = zv
        vtb[nkb + 1] = zv
        for j in range(nkb):
            vtb[j + 1] = v_ref[j * WINDOW:(j + 1) * WINDOW, :].astype(_f32).T.astype(_bf16)

    r0 = pl.multiple_of(qb * Q_TILE, Q_TILE)
    kb0 = qb * (Q_TILE // WINDOW)
    ka = lax.broadcasted_iota(jnp.int32, (WINDOW, WINDOW), 0)
    qc = lax.broadcasted_iota(jnp.int32, (WINDOW, WINDOW), 1)
    tri_ge = jnp.where(ka >= qc, 0.0, NEG_BIG).astype(_f32)
    tri_le = jnp.where(ka <= qc, 0.0, NEG_BIG).astype(_f32)
    bias_first = jnp.where(qb == 0, NEG_BIG, tri_ge)
    bias_last = jnp.where(qb == nqb - 1, NEG_BIG, tri_le)
    zero_blk = jnp.zeros((WINDOW, WINDOW), _bf16)
    vwin = jnp.concatenate([vtb[kb0 + j] for j in range(KEY_SPAN // WINDOW)], axis=1)

    def softmax_col(blocks, sink):
        m = blocks[0]
        for b in blocks[1:]:
            m = jnp.maximum(m, b)
        m = jnp.maximum(jnp.max(m, axis=0, keepdims=True), sink)
        ps = [jnp.exp(b - m) for b in blocks]
        tot = ps[0]
        for p in ps[1:]:
            tot = tot + p
        return ps, jnp.sum(tot, axis=0, keepdims=True) + jnp.exp(sink - m)

    w = WINDOW

    def scores(g):
        kh = kpad[g // Q_PER_KV, pl.ds(r0, KEY_SPAN), :]
        qg = q_ref[:, g * HEAD_DIM:(g + 1) * HEAD_DIM]
        return lax.dot_general(kh, qg, (((1,), (1,)), ((), ())), preferred_element_type=_f32)

    s = scores(0)
    for g in range(N_Q_HEADS):
        s_next = scores(g + 1) if g + 1 < N_Q_HEADS else None
        sink = sink_ref[g]
        h = g // Q_PER_KV
        p0, l0 = softmax_col([s[0:w, 0:w] + bias_first, s[w:2 * w, 0:w], s[2 * w:3 * w, 0:w] + tri_le], sink)
        p1, l1 = softmax_col([s[w:2 * w, w:] + tri_ge, s[2 * w:3 * w, w:], s[3 * w:, w:] + bias_last], sink)
        col0 = jnp.concatenate([p.astype(_bf16) for p in p0] + [zero_blk], axis=0)
        col1 = jnp.concatenate([zero_blk] + [p.astype(_bf16) for p in p1], axis=0)
        pt = jnp.concatenate([col0, col1], axis=1)
        o_t = jnp.dot(vwin[h * HEAD_DIM:(h + 1) * HEAD_DIM, :], pt, preferred_element_type=_f32)
        ot[g * HEAD_DIM:(g + 1) * HEAD_DIM, :] = o_t / jnp.concatenate([l0, l1], axis=1)
        s = s_next
    for c in range(ATTN_WIDTH // LANES):
        o_ref[:, c * LANES:(c + 1) * LANES] = ot[c * LANES:(c + 1) * LANES, :].T.astype(_bf16)


def _attention(seq, q, k, v, sinks):
    t_tokens = q.shape[0]
    nqb = seq // Q_TILE
    return pl.pallas_call(
        _attn_kernel,
        out_shape=jax.ShapeDtypeStruct((t_tokens, ATTN_WIDTH), _bf16),
        grid=(t_tokens // seq, nqb),
        in_specs=[pl.BlockSpec(memory_space=pltpu.SMEM),
                  pl.BlockSpec((Q_TILE, ATTN_WIDTH), lambda b, j: (b * nqb + j, 0)),
                  pl.BlockSpec((seq, KV_WIDTH), lambda b, j: (b, 0)),
                  pl.BlockSpec((seq, KV_WIDTH), lambda b, j: (b, 0))],
        out_specs=pl.BlockSpec((Q_TILE, ATTN_WIDTH), lambda b, j: (b * nqb + j, 0)),
        scratch_shapes=[pltpu.VMEM((N_KV_HEADS, seq + 2 * WINDOW, HEAD_DIM), _bf16),
                        pltpu.VMEM((seq // WINDOW + 2, KV_WIDTH, WINDOW), _bf16),
                        pltpu.VMEM((ATTN_WIDTH, Q_TILE), _f32)],
        compiler_params=pltpu.CompilerParams(dimension_semantics=("parallel", "arbitrary"),
                                             vmem_limit_bytes=VMEM_LIMIT),
        name="attention",
    )(sinks, q, k, v)


def _outproj_kernel(x_ref, mf_ref, at_ref, g1_ref, sh_ref, sc_ref, ng_ref, wof_ref, woa_ref, bo_ref,
                    wr_ref, br_ref, tri_ref, x1_ref, hp_ref, rt_ref, wc_ref, cnt_ref, carry_ref):
    i = pl.program_id(0)

    @pl.when(i == 0)
    def _():
        carry_ref[...] = jnp.zeros_like(carry_ref)

    mf = jnp.concatenate([mf_ref[c] for c in range(FOURIER_WIDTH // LANES)], axis=1)
    mix = (jnp.dot(mf.astype(_bf16), wof_ref[...], preferred_element_type=_f32)
           + jnp.dot(at_ref[...], woa_ref[...], preferred_element_type=_f32) + bo_ref[...])
    x1 = x_ref[...] + g1_ref[0] * mix
    x1_ref[...] = x1
    h = _rms(x1, ng_ref[...]) * (1.0 + sc_ref[0]) + sh_ref[0]
    hp_ref[...] = _pack_halves(h[:, :HALF], h[:, HALF:])
    logits = lax.dot_general(wr_ref[...], h.astype(_bf16), (((1,), (1,)), ((), ())),
                             preferred_element_type=_f32) + br_ref[...]
    tm = logits.shape[1]
    erow = lax.broadcasted_iota(jnp.int32, logits.shape, 0)
    work = logits
    hots, vals, idxs = [], [], []
    for _k in range(TOP_K):
        mx = jnp.max(work, axis=0, keepdims=True)
        ix = jnp.min(jnp.where(work == mx, erow, N_EXPERTS), axis=0, keepdims=True)
        hot = erow == ix
        work = jnp.where(hot, -jnp.inf, work)
        hots.append(hot)
        vals.append(mx)
        idxs.append(ix)
    es = [jnp.exp(v - vals[0]) for v in vals]
    den = es[0] + es[1] + es[2] + es[3]
    member = jnp.zeros(logits.shape, _f32)
    for hot in hots:
        member = member + hot.astype(_f32)
    carry = carry_ref[...]
    before = (jnp.dot(member.astype(_bf16), tri_ref[...], preferred_element_type=_f32)
              + jnp.concatenate([carry] * (tm // LANES), axis=1))
    r16 = lax.broadcasted_iota(jnp.int32, (ROUTE_ROWS, tm), 0)
    r128 = lax.broadcasted_iota(jnp.int32, (LANES, tm), 0)
    slab = jnp.zeros((ROUTE_ROWS, tm), jnp.int32)
    wrows = jnp.zeros((LANES, tm), _f32)
    for k in range(TOP_K):
        rank = jnp.sum(jnp.where(hots[k], before, 0.0), axis=0, keepdims=True).astype(jnp.int32)
        slab = jnp.where(r16 == k, idxs[k], slab)
        slab = jnp.where(r16 == TOP_K + k, rank, slab)
        wrows = jnp.where(r128 == k, es[k] / den, wrows)
    rt_ref[...] = slab
    wc_ref[...] = wrows.T
    carry = carry + jnp.broadcast_to(jnp.sum(member, axis=1, keepdims=True), carry.shape)
    carry_ref[...] = carry
    cnt_ref[...] = carry.astype(jnp.int32)


def _outproj(seq, x, mf, attn, g1, sh2, sc2, norm_g, wo_f, wo_a, b_o, w_r, b_r, tri):
    t_tokens = x.shape[0]
    tm = TOKEN_TILE
    tiles_per_seq = seq // tm
    row = lambda i: (i, 0)
    per_seq = lambda i: (i // tiles_per_seq, 0, 0)
    const = lambda i: (0, 0)
    return pl.pallas_call(
        _outproj_kernel,
        out_shape=[jax.ShapeDtypeStruct((t_tokens, D_MODEL), _f32),
                   jax.ShapeDtypeStruct((t_tokens, HALF), jnp.int32),
                   jax.ShapeDtypeStruct((ROUTE_ROWS, t_tokens), jnp.int32),
                   jax.ShapeDtypeStruct((t_tokens, LANES), _f32),
                   jax.ShapeDtypeStruct((N_EXPERTS, LANES), jnp.int32)],
        grid=(t_tokens // tm,),
        in_specs=[pl.BlockSpec((tm, D_MODEL), row),
                  pl.BlockSpec((FOURIER_WIDTH // LANES, tm, LANES), lambda i: (0, i, 0)),
                  pl.BlockSpec((tm, ATTN_WIDTH), row),
                  pl.BlockSpec((1, 1, D_MODEL), per_seq), pl.BlockSpec((1, 1, D_MODEL), per_seq),
                  pl.BlockSpec((1, 1, D_MODEL), per_seq),
                  pl.BlockSpec((1, D_MODEL), const),
                  pl.BlockSpec((FOURIER_WIDTH, D_MODEL), const), pl.BlockSpec((ATTN_WIDTH, D_MODEL), const),
                  pl.BlockSpec((1, D_MODEL), const),
                  pl.BlockSpec((N_EXPERTS, D_MODEL), const), pl.BlockSpec((N_EXPERTS, tm), const),
                  pl.BlockSpec((tm, tm), const)],
        out_specs=[pl.BlockSpec((tm, D_MODEL), row), pl.BlockSpec((tm, HALF), row),
                   pl.BlockSpec((ROUTE_ROWS, tm), lambda i: (0, i)), pl.BlockSpec((tm, LANES), row),
                   pl.BlockSpec((N_EXPERTS, LANES), const)],
        scratch_shapes=[pltpu.VMEM((N_EXPERTS, LANES), _f32)],
        compiler_params=pltpu.CompilerParams(dimension_semantics=("arbitrary",), vmem_limit_bytes=VMEM_LIMIT),
        name="outproj",
    )(x, mf, attn, g1, sh2, sc2, norm_g, wo_f, wo_a, b_o, w_r, b_r, tri)


def _expert_kernel(blk_ref, bexp_ref, nact_ref, x_ref, wgu_ref, bgu_ref, wd_ref, bd_ref, y_ref, wgu_bf, wd_bf):
    i = pl.program_id(0)
    prev = bexp_ref[jnp.maximum(i - 1, 0)]

    @pl.when((i == 0) | (bexp_ref[i] != prev))
    def _():
        wgu_bf[...] = wgu_ref[...].astype(_bf16)
        wd_bf[...] = wd_ref[...].astype(_bf16)

    def gate_up(rows):
        xa, xb = _unpack_halves(x_ref[rows, :])
        return (jnp.dot(xa.astype(_bf16), wgu_bf[:HALF, :], preferred_element_type=_f32)
                + jnp.dot(xb.astype(_bf16), wgu_bf[HALF:, :], preferred_element_type=_f32) + bgu_ref[...])

    def activate(gu):
        g = jnp.minimum(gu[:, :D_FF], SWIGLU_LIMIT)
        u = jnp.clip(gu[:, D_FF:], -SWIGLU_LIMIT, SWIGLU_LIMIT)
        return ((u + 1.0) * (g * (1.0 / (1.0 + jnp.exp(-SWIGLU_ALPHA * g))))).astype(_bf16)

    def down(rows, act):
        y = jnp.dot(act, wd_bf[...], preferred_element_type=_f32) + bd_ref[...]
        y_ref[rows, :] = _pack_halves(y[:, :HALF], y[:, HALF:])

    @pl.when(i < nact_ref[0])
    def _():
        tiles = [pl.ds(r, EXPERT_SUB) for r in range(0, EXPERT_BLOCK, EXPERT_SUB)]
        gu = gate_up(tiles[0])
        for n, rows in enumerate(tiles):
            gu_next = gate_up(tiles[n + 1]) if n + 1 < len(tiles) else None
            down(rows, activate(gu))
            gu = gu_next


def _experts(layer, xs, blk, bexp, nact, w_gu, b_gu, w_down, b_down):
    p_rows = xs.shape[0]
    nblk = p_rows // EXPERT_BLOCK
    grid_spec = pltpu.PrefetchScalarGridSpec(
        num_scalar_prefetch=3,
        grid=(nblk,),
        in_specs=[pl.BlockSpec((EXPERT_BLOCK, HALF), lambda i, blk, be, na: (blk[i], 0)),
                  pl.BlockSpec((None, None, D_MODEL, 2 * D_FF), lambda i, blk, be, na: (layer, be[i], 0, 0)),
                  pl.BlockSpec((None, 1, 2 * D_FF), lambda i, blk, be, na: (be[i], 0, 0)),
                  pl.BlockSpec((None, None, D_FF, D_MODEL), lambda i, blk, be, na: (layer, be[i], 0, 0)),
                  pl.BlockSpec((None, 1, D_MODEL), lambda i, blk, be, na: (be[i], 0, 0))],
        out_specs=pl.BlockSpec((EXPERT_BLOCK, HALF), lambda i, blk, be, na: (blk[i], 0)),
        scratch_shapes=[pltpu.VMEM((D_MODEL, 2 * D_FF), _bf16), pltpu.VMEM((D_FF, D_MODEL), _bf16)])
    return pl.pallas_call(
        _expert_kernel,
        out_shape=jax.ShapeDtypeStruct((p_rows, HALF), jnp.int32),
        grid_spec=grid_spec,
        compiler_params=pltpu.CompilerParams(dimension_semantics=("arbitrary",), vmem_limit_bytes=VMEM_LIMIT),
        name="experts",
    )(blk, bexp, nact, xs, w_gu, b_gu.reshape(N_EXPERTS, 1, 2 * D_FF), w_down,
      b_down.reshape(N_EXPERTS, 1, D_MODEL))


def _sc_mesh():
    return plsc.VectorSubcoreMesh(core_axis_name="c", subcore_axis_name="s")


def _sc_worker():
    return lax.axis_index("s") * 2 + lax.axis_index("c")


def _dispatch(hp, dest_t, p_rows):
    n_chunks = dest_t.shape[0]
    per_worker = n_chunks // SC_WORKERS

    @functools.partial(
        pl.kernel, mesh=_sc_mesh(),
        out_type=jax.ShapeDtypeStruct((p_rows, HALF), jnp.int32),
        scratch_types=[pltpu.VMEM((TOP_K, SC_CHUNK), jnp.int32),
                       pltpu.VMEM((SC_CHUNK, HALF), jnp.int32),
                       pltpu.SemaphoreType.DMA],
        name="dispatch")
    def run(hp_hbm, dest_hbm, xs_hbm, idx_v, rows_v, sem):
        base = _sc_worker() * per_worker

        @pl.loop(0, per_worker)
        def _(j):
            chunk = base + j
            pltpu.sync_copy(dest_hbm.at[chunk], idx_v)
            pltpu.sync_copy(hp_hbm.at[pl.ds(chunk * SC_CHUNK, SC_CHUNK)], rows_v)
            copies = [pltpu.async_copy(rows_v, xs_hbm.at[idx_v.at[k]], sem) for k in range(TOP_K)]
            for cp in copies:
                cp.wait()

    return run(hp, dest_t)


def _gather_rows(y, dest_t, t_tokens):
    n_chunks = dest_t.shape[0]
    per_worker = n_chunks // SC_WORKERS

    @functools.partial(
        pl.kernel, mesh=_sc_mesh(),
        out_type=jax.ShapeDtypeStruct((TOP_K, t_tokens, HALF), jnp.int32),
        scratch_types=[pltpu.VMEM((TOP_K, SC_CHUNK), jnp.int32),
                       pltpu.VMEM((SC_CHUNK, HALF), jnp.int32),
                       pltpu.SemaphoreType.DMA],
        name="gather_rows")
    def run(y_hbm, dest_hbm, yk_hbm, idx_v, rows_v, sem):
        base = _sc_worker() * per_worker

        @pl.loop(0, per_worker)
        def _(j):
            chunk = base + j
            pltpu.sync_copy(dest_hbm.at[chunk], idx_v)
            for k in range(TOP_K):
                pltpu.async_copy(y_hbm.at[idx_v.at[k]], rows_v, sem).wait()
                pltpu.sync_copy(rows_v, yk_hbm.at[k, pl.ds(chunk * SC_CHUNK, SC_CHUNK)])

    return run(y, dest_t)


def _final_kernel(x1_ref, yk_ref, wc_ref, g2_ref, ng_ref, o_ref):
    for rows in _sub_tiles(x1_ref.shape[0]):
        x = x1_ref[rows, :] + g2_ref[0] * _moe_combine(yk_ref, wc_ref, rows)
        o_ref[rows, :] = _rms(x, ng_ref[...])


def _final(seq, x1, yk, wcol, g2, final_g):
    t_tokens = x1.shape[0]
    tm = TOKEN_TILE
    tiles_per_seq = seq // tm
    row = lambda i: (i, 0)
    return pl.pallas_call(
        _final_kernel,
        out_shape=jax.ShapeDtypeStruct((t_tokens, D_MODEL), _f32),
        grid=(t_tokens // tm,),
        in_specs=[pl.BlockSpec((tm, D_MODEL), row),
                  pl.BlockSpec((TOP_K, tm, HALF), lambda i: (0, i, 0)),
                  pl.BlockSpec((tm, LANES), row),
                  pl.BlockSpec((1, 1, D_MODEL), lambda i: (i // tiles_per_seq, 0, 0)),
                  pl.BlockSpec((1, D_MODEL), lambda i: (0, 0))],
        out_specs=pl.BlockSpec((tm, D_MODEL), row),
        compiler_params=pltpu.CompilerParams(dimension_semantics=("parallel",), vmem_limit_bytes=VMEM_LIMIT),
        name="final",
    )(x1, yk, wcol, g2, final_g)


def _plan(route, counts, nblk):
    cnt = counts[:, 0]
    padded = ((cnt + EXPERT_BLOCK - 1) // EXPERT_BLOCK) * EXPERT_BLOCK
    pad_end = jnp.cumsum(padded)
    pad_start = pad_end - padded
    expert = jnp.arange(N_EXPERTS, dtype=jnp.int32)[:, None, None]
    start_of = jnp.sum(jnp.where(route[None, :TOP_K] == expert, pad_start[:, None, None], 0), axis=0)
    dest = start_of + route[TOP_K:2 * TOP_K]
    nact = jnp.maximum(pad_end[-1] // EXPERT_BLOCK, 1)
    blk = jnp.minimum(jnp.arange(nblk, dtype=jnp.int32), nact - 1)
    bexp = jnp.minimum(jnp.sum(pad_end[None, :] <= (blk * EXPERT_BLOCK)[:, None], axis=1), N_EXPERTS - 1)
    t_tokens = route.shape[1]
    dest_t = dest.reshape(TOP_K, t_tokens // SC_CHUNK, SC_CHUNK).transpose(1, 0, 2)
    return dest_t, blk.astype(jnp.int32), bexp.astype(jnp.int32), nact.reshape(1).astype(jnp.int32)


def _trunk(x_all, c_all, norm1_g, ada_w, ada_b, w_in, b_in, w_fmix, sinks, w_o, b_o,
           norm2_g, w_router, b_router, w_gu, b_gu, w_down, b_down, final_g):
    nb, seq, _ = x_all.shape
    t_tokens = nb * seq
    assert seq % TOKEN_TILE == 0 and seq % Q_TILE == 0 and seq >= KEY_SPAN
    assert t_tokens % (SC_CHUNK * SC_WORKERS) == 0
    nblk = (t_tokens * TOP_K) // EXPERT_BLOCK + N_EXPERTS
    p_rows = nblk * EXPERT_BLOCK

    mod = _modulation(c_all, ada_w, ada_b)
    mod = mod.reshape(DEPTH, nb, 6, 1, D_MODEL)
    rope_tabs = _rope_tables(seq)
    dft = _dft_tables(seq)
    tri = jnp.triu(jnp.ones((TOKEN_TILE, TOKEN_TILE), _f32), 1).astype(_bf16)

    x = x_all.reshape(t_tokens, D_MODEL)
    moe = None
    for l in range(DEPTH):
        sh1, sc1, g1, sh2, sc2, g2 = (mod[l, :, j] for j in range(6))
        ab = _channel_dft_fold(w_fmix[l], seq)
        outs = _inproj(seq, x, moe, sh1, sc1, norm1_g[l].reshape(1, D_MODEL), w_in[l].astype(_bf16),
                       b_in[l].reshape(1, IN_WIDTH), ab, rope_tabs)
        if moe is None:
            y, q, k, v = outs
        else:
            y, q, k, v, x = outs
        mf = _fourier(seq, y, dft)
        attn = _attention(seq, q, k, v, sinks[l])
        w_o_bf = w_o[l].astype(_bf16)
        w_r = w_router[l].T.astype(_bf16)
        b_r = jnp.broadcast_to(b_router[l][:, None], (N_EXPERTS, TOKEN_TILE))
        x1, hp, route, wcol, counts = _outproj(seq, x, mf, attn, g1, sh2, sc2, norm2_g[l].reshape(1, D_MODEL),
                                               w_o_bf[:FOURIER_WIDTH], w_o_bf[FOURIER_WIDTH:],
                                               b_o[l].reshape(1, D_MODEL), w_r, b_r, tri)
        dest_t, blk, bexp, nact = _plan(route, counts, nblk)
        xs = _dispatch(hp, dest_t, p_rows)
        ys = _experts(l, xs, blk, bexp, nact, w_gu, b_gu[l], w_down, b_down[l])
        yk = _gather_rows(ys, dest_t, t_tokens)
        x = x1
        moe = (yk, wcol, g2)
    out = _final(seq, x, moe[0], moe[1], moe[2], final_g.reshape(1, D_MODEL))
    return out.reshape(nb, seq, D_MODEL)


def kernel(x_prompt, x_sample, c_prompt, c_sample, norm1_g, ada_w, ada_b, w_in, b_in, w_fmix, sinks, w_o, b_o,
           norm2_g, w_router, b_router, w_gu, b_gu, w_down, b_down, final_g):
    ws = (norm1_g, ada_w, ada_b, w_in, b_in, w_fmix, sinks, w_o, b_o,
          norm2_g, w_router, b_router, w_gu, b_gu, w_down, b_down, final_g)
    return _trunk(x_prompt, c_prompt, *ws), _trunk(x_sample, c_sample, *ws)
```

```python
import functools

import jax
import jax.numpy as jnp
from jax import lax
from jax.experimental import pallas as pl
from jax.experimental.pallas import tpu as pltpu
from jax.experimental.pallas import tpu_sc as plsc

D_MODEL = 1024
DEPTH = 4
FOURIER_WIDTH = 256
N_FGROUPS = 4
FGROUP_DIM = 64
HEAD_DIM = 64
N_Q_HEADS = 12
N_KV_HEADS = 4
Q_PER_KV = 3
ATTN_WIDTH = N_Q_HEADS * HEAD_DIM
KV_WIDTH = N_KV_HEADS * HEAD_DIM
IN_WIDTH = FOURIER_WIDTH + ATTN_WIDTH + 2 * KV_WIDTH
WINDOW = 128
ROPE_THETA = 500000.0
ROT_DIM = 16
N_EXPERTS = 32
TOP_K = 4
D_FF = 512
EXPERT_BLOCK = 1024
EXPERT_SUB = 512
SWIGLU_LIMIT = 7.0
SWIGLU_ALPHA = 1.702
NORM_EPS = 1e-5

LANES = 128
HALF = D_MODEL // 2
TOKEN_TILE = 512
SUB_TILE = 256
Q_TILE = 256
KEY_SPAN = Q_TILE + 2 * WINDOW
SC_CHUNK = 128
SC_WORKERS = 32
VMEM_LIMIT = 56 * 1024 * 1024
NEG_BIG = -1e30
LOG2E = 1.4426950408889634
Q_SCALE = HEAD_DIM ** -0.5 * LOG2E
HI_MASK = -65536
ROUTE_ROWS = 16

_f32 = jnp.float32
_bf16 = jnp.bfloat16


def _pack_halves(a, b):
    ua = lax.bitcast_convert_type(a.astype(_bf16).astype(_f32), jnp.int32)
    ub = lax.bitcast_convert_type(b.astype(_bf16).astype(_f32), jnp.int32)
    return ua | lax.shift_right_logical(ub, 16)


def _unpack_halves(p):
    a = lax.bitcast_convert_type(p & HI_MASK, _f32)
    b = lax.bitcast_convert_type(lax.shift_left(p, 16), _f32)
    return a, b


def _rms(x, g):
    ms = jnp.mean(x * x, axis=-1, keepdims=True)
    return x * lax.rsqrt(ms + NORM_EPS) * g


def _mod_kernel(c_ref, w_ref, b_ref, o_ref):
    c = c_ref[...]
    s = c * (1.0 / (1.0 + jnp.exp(-c)))
    o_ref[0] = jnp.dot(s.astype(_bf16), w_ref[0].astype(_bf16), preferred_element_type=_f32) + b_ref[0]


def _modulation(c_all, ada_w, ada_b):
    nb = c_all.shape[0]
    ncol = ada_w.shape[2] // D_MODEL
    return pl.pallas_call(
        _mod_kernel,
        out_shape=jax.ShapeDtypeStruct((DEPTH, nb, 6 * D_MODEL), _f32),
        grid=(DEPTH, ncol),
        in_specs=[
            pl.BlockSpec((nb, D_MODEL), lambda l, j: (0, 0)),
            pl.BlockSpec((1, D_MODEL, D_MODEL), lambda l, j: (l, 0, j)),
            pl.BlockSpec((1, 1, D_MODEL), lambda l, j: (l, 0, j)),
        ],
        out_specs=pl.BlockSpec((1, nb, D_MODEL), lambda l, j: (l, 0, j)),
        name="modulation",
    )(c_all, ada_w, ada_b.reshape(DEPTH, 1, 6 * D_MODEL))


def _moe_combine(yk_ref, wc_ref, rows):
    wc = wc_ref[rows, :]
    acc_a = None
    acc_b = None
    for k in range(TOP_K):
        w = wc[:, k:k + 1]
        a, b = _unpack_halves(yk_ref[k, rows, :])
        acc_a = w * a if acc_a is None else acc_a + w * a
        acc_b = w * b if acc_b is None else acc_b + w * b
    return jnp.concatenate([acc_a, acc_b], axis=1)


def _inproj_matmul(x, sh_ref, sc_ref, ng_ref, w_ref, b_ref, *_):
    h = _rms(x, ng_ref[...]) * (1.0 + sc_ref[0]) + sh_ref[0]
    return jnp.dot(h.astype(_bf16), w_ref[...], preferred_element_type=_f32) + b_ref[...]


def _inproj_finish(rows, z, sh_ref, sc_ref, ng_ref, w_ref, b_ref, ab_ref, rc_ref, rs1_ref, rs2_ref,
                   y_ref, q_ref, k_ref, v_ref):
    f = z[:, :FOURIER_WIDTH]
    y_ref[rows, :] = jnp.dot(f.astype(_bf16), ab_ref[...], preferred_element_type=_f32).astype(_bf16)
    rc = rc_ref[rows, :]
    rs1 = rs1_ref[rows, :]
    rs2 = rs2_ref[rows, :]

    def rope(t):
        return t * rc + pltpu.roll(t, LANES - ROT_DIM // 2, 1) * rs1 + pltpu.roll(t, ROT_DIM // 2, 1) * rs2

    q0 = FOURIER_WIDTH
    for c in range(ATTN_WIDTH // LANES):
        t = z[:, q0 + c * LANES:q0 + (c + 1) * LANES]
        q_ref[rows, c * LANES:(c + 1) * LANES] = (rope(t) * Q_SCALE).astype(_bf16)
    k0 = q0 + ATTN_WIDTH
    for c in range(KV_WIDTH // LANES):
        t = z[:, k0 + c * LANES:k0 + (c + 1) * LANES]
        k_ref[rows, c * LANES:(c + 1) * LANES] = rope(t).astype(_bf16)
    v_ref[rows, :] = z[:, k0 + KV_WIDTH:].astype(_bf16)


def _sub_tiles(n_rows):
    return [pl.ds(r, SUB_TILE) for r in range(0, n_rows, SUB_TILE)]


def _inproj_pipeline(load_x, n_rows, refs):
    tiles = _sub_tiles(n_rows)
    z = _inproj_matmul(load_x(tiles[0]), *refs)
    for n, rows in enumerate(tiles):
        z_next = _inproj_matmul(load_x(tiles[n + 1]), *refs) if n + 1 < len(tiles) else None
        _inproj_finish(rows, z, *refs)
        z = z_next


def _inproj_first_kernel(x_ref, *refs):
    _inproj_pipeline(lambda rows: x_ref[rows, :], x_ref.shape[0], refs)


def _inproj_combine_kernel(x1_ref, yk_ref, wc_ref, g2_ref, *refs):
    x2_ref = refs[-1]

    def load_x(rows):
        x = x1_ref[rows, :] + g2_ref[0] * _moe_combine(yk_ref, wc_ref, rows)
        x2_ref[rows, :] = x
        return x

    _inproj_pipeline(load_x, x1_ref.shape[0], refs[:-1])


def _inproj(seq, x, moe, sh1, sc1, norm_g, w_in, b_in, ab, rope_tabs):
    t_tokens = x.shape[0]
    tm = TOKEN_TILE
    tiles_per_seq = seq // tm
    row = lambda i: (i, 0)
    per_seq = lambda i: (i // tiles_per_seq, 0, 0)
    const = lambda i: (0, 0)
    pos = lambda i: (i % tiles_per_seq, 0)
    in_specs = [pl.BlockSpec((tm, D_MODEL), row)]
    args = [x]
    if moe is not None:
        yk, wcol, g2 = moe
        in_specs += [pl.BlockSpec((TOP_K, tm, HALF), lambda i: (0, i, 0)),
                     pl.BlockSpec((tm, LANES), row),
                     pl.BlockSpec((1, 1, D_MODEL), per_seq)]
        args += [yk, wcol, g2]
    in_specs += [pl.BlockSpec((1, 1, D_MODEL), per_seq), pl.BlockSpec((1, 1, D_MODEL), per_seq),
                 pl.BlockSpec((1, D_MODEL), const),
                 pl.BlockSpec((D_MODEL, IN_WIDTH), const), pl.BlockSpec((1, IN_WIDTH), const),
                 pl.BlockSpec((FOURIER_WIDTH, 2 * FOURIER_WIDTH), const),
                 pl.BlockSpec((tm, LANES), pos), pl.BlockSpec((tm, LANES), pos), pl.BlockSpec((tm, LANES), pos)]
    args += [sh1, sc1, norm_g, w_in, b_in, ab, *rope_tabs]
    out_shape = [jax.ShapeDtypeStruct((t_tokens, 2 * FOURIER_WIDTH), _bf16),
                 jax.ShapeDtypeStruct((t_tokens, ATTN_WIDTH), _bf16),
                 jax.ShapeDtypeStruct((t_tokens, KV_WIDTH), _bf16),
                 jax.ShapeDtypeStruct((t_tokens, KV_WIDTH), _bf16)]
    out_specs = [pl.BlockSpec((tm, 2 * FOURIER_WIDTH), row), pl.BlockSpec((tm, ATTN_WIDTH), row),
                 pl.BlockSpec((tm, KV_WIDTH), row), pl.BlockSpec((tm, KV_WIDTH), row)]
    if moe is not None:
        out_shape.append(jax.ShapeDtypeStruct((t_tokens, D_MODEL), _f32))
        out_specs.append(pl.BlockSpec((tm, D_MODEL), row))
    return pl.pallas_call(
        _inproj_first_kernel if moe is None else _inproj_combine_kernel,
        out_shape=out_shape,
        grid=(t_tokens // tm,),
        in_specs=in_specs,
        out_specs=out_specs,
        compiler_params=pltpu.CompilerParams(dimension_semantics=("parallel",), vmem_limit_bytes=VMEM_LIMIT),
        name="inproj",
    )(*args)


def _fourier_kernel(y_ref, ce_ref, se_ref, co_ref, so_ref, o_ref):
    half = y_ref.shape[0] // 2
    lo = y_ref[:half, :].astype(_f32)
    hi = y_ref[half:, :].astype(_f32)
    ye = (lo + hi).astype(_bf16)
    yo = (lo - hi).astype(_bf16)
    w = FOURIER_WIDTH
    even = (jnp.dot(ce_ref[...], ye[:, :w], preferred_element_type=_f32)
            + jnp.dot(se_ref[...], ye[:, w:], preferred_element_type=_f32))
    odd = (jnp.dot(co_ref[...], yo[:, :w], preferred_element_type=_f32)
           + jnp.dot(so_ref[...], yo[:, w:], preferred_element_type=_f32))
    for c in range(FOURIER_WIDTH // LANES):
        o_ref[c, pl.ds(0, half, stride=2), :] = even[:, c * LANES:(c + 1) * LANES]
        o_ref[c, pl.ds(1, half, stride=2), :] = odd[:, c * LANES:(c + 1) * LANES]


def _fourier(seq, y, dft):
    t_tokens = y.shape[0]
    half = seq // 2
    const = lambda b: (0, 0)
    return pl.pallas_call(
        _fourier_kernel,
        out_shape=jax.ShapeDtypeStruct((FOURIER_WIDTH // LANES, t_tokens, LANES), _f32),
        grid=(t_tokens // seq,),
        in_specs=[pl.BlockSpec((seq, 2 * FOURIER_WIDTH), lambda b: (b, 0))]
        + [pl.BlockSpec((half, half), const)] * 4,
        out_specs=pl.BlockSpec((FOURIER_WIDTH // LANES, seq, LANES), lambda b: (0, b, 0)),
        compiler_params=pltpu.CompilerParams(dimension_semantics=("parallel",), vmem_limit_bytes=VMEM_LIMIT),
        name="fourier",
    )(y, *dft)


def _dft_tables(seq):
    half = seq // 2
    j = jnp.arange(half, dtype=jnp.int32)[:, None]
    k = jnp.arange(half, dtype=jnp.int32)[None, :]
    ang_e = ((2 * j * k) % seq).astype(_f32) * (2.0 * jnp.pi / seq)
    ang_o = (((2 * j + 1) * k) % seq).astype(_f32) * (2.0 * jnp.pi / seq)
    return (jnp.cos(ang_e).astype(_bf16), jnp.sin(ang_e).astype(_bf16),
            jnp.cos(ang_o).astype(_bf16), jnp.sin(ang_o).astype(_bf16))


def _channel_dft_fold(w_fmix, seq):
    c = jnp.arange(FGROUP_DIM, dtype=jnp.int32)
    ang = ((c[:, None] * c[None, :]) % FGROUP_DIM).astype(_f32) * (2.0 * jnp.pi / FGROUP_DIM)
    scale = (seq * FGROUP_DIM) ** -0.5
    hp = lax.Precision.HIGHEST
    a = jnp.einsum('cm,gmd->gcd', jnp.cos(ang) * scale, w_fmix, precision=hp)
    b = jnp.einsum('cm,gmd->gcd', -jnp.sin(ang) * scale, w_fmix, precision=hp)
    eye = jnp.eye(N_FGROUPS, dtype=_f32)
    bd = lambda m: jnp.einsum('gcd,gh->gchd', m, eye).reshape(FOURIER_WIDTH, FOURIER_WIDTH)
    return jnp.concatenate([bd(a), bd(b)], axis=1).astype(_bf16)


def _rope_tables(seq):
    half = ROT_DIM // 2
    inv_freq = jnp.power(ROPE_THETA, -jnp.arange(0, ROT_DIM, 2, dtype=_f32) / ROT_DIM)
    ang = jnp.arange(seq, dtype=_f32)[:, None] * inv_freq[None, :]
    cos, sin = jnp.cos(ang), jnp.sin(ang)
    ones = jnp.ones((seq, HEAD_DIM - ROT_DIM), _f32)
    zeros = jnp.zeros((seq, HEAD_DIM - ROT_DIM), _f32)
    zh = jnp.zeros((seq, half), _f32)
    c1 = jnp.concatenate([cos, cos, ones], axis=1)
    s1 = jnp.concatenate([-sin, zh, zeros], axis=1)
    s2 = jnp.concatenate([zh, sin, zeros], axis=1)
    rep = LANES // HEAD_DIM
    return tuple(jnp.tile(t, (1, rep)) for t in (c1, s1, s2))


def _attn_kernel(sink_ref, q_ref, k_ref, v_ref, o_ref, kpad, vtb, ot):
    seq = k_ref.shape[0]
    nqb = seq // Q_TILE
    nkb = seq // WINDOW
    qb = pl.program_id(1)

    @pl.when(qb == 0)
    def _():
        zk = jnp.zeros((WINDOW, HEAD_DIM), _bf16)
        for h in range(N_KV_HEADS):
            kpad[h, :WINDOW, :] = zk
            kpad[h, WINDOW + seq:, :] = zk
            kpad[h, WINDOW:WINDOW + seq, :] = k_ref[:, h * HEAD_DIM:(h + 1) * HEAD_DIM]
        zv = jnp.zeros((KV_WIDTH, WINDOW), _bf16)
        vtb[0] = zv
        vtb[nkb + 1] = zv
        for j in range(nkb):
            vtb[j + 1] = v_ref[j * WINDOW:(j + 1) * WINDOW, :].astype(_f32).T.astype(_bf16)

    r0 = pl.multiple_of(qb * Q_TILE, Q_TILE)
    kb0 = qb * (Q_TILE // WINDOW)
    ka = lax.broadcasted_iota(jnp.int32, (WINDOW, WINDOW), 0)
    qc = lax.broadcasted_iota(jnp.int32, (WINDOW, WINDOW), 1)
    tri_ge = jnp.where(ka >= qc, 0.0, NEG_BIG).astype(_f32)
    tri_le = jnp.where(ka <= qc, 0.0, NEG_BIG).astype(_f32)
    bias_first = jnp.where(qb == 0, NEG_BIG, tri_ge)
    bias_last = jnp.where(qb == nqb - 1, NEG_BIG, tri_le)
    zero_blk = jnp.zeros((WINDOW, WINDOW), _bf16)
    vwin = jnp.concatenate([vtb[kb0 + j] for j in range(KEY_SPAN // WINDOW)], axis=1)

    def softmax_col(blocks, sink):
        m = blocks[0]
        for b in blocks[1:]:
            m = jnp.maximum(m, b)
        m = jnp.maximum(jnp.max(m, axis=0, keepdims=True), sink)
        ps = [jnp.exp2(b - m) for b in blocks]
        tot = ps[0]
        for p in ps[1:]:
            tot = tot + p
        return ps, jnp.sum(tot, axis=0, keepdims=True) + jnp.exp2(sink - m)

    w = WINDOW

    def scores(h):
        kh = kpad[h, pl.ds(r0, KEY_SPAN), :]
        q3 = jnp.concatenate([q_ref[:, g * HEAD_DIM:(g + 1) * HEAD_DIM]
                              for g in range(Q_PER_KV * h, Q_PER_KV * (h + 1))], axis=0)
        return lax.dot_general(kh, q3, (((1,), (1,)), ((), ())), preferred_element_type=_f32)

    s = scores(0)
    for h in range(N_KV_HEADS):
        s_next = scores(h + 1) if h + 1 < N_KV_HEADS else None
        cols, dens = [], []
        for i in range(Q_PER_KV):
            sink = sink_ref[Q_PER_KV * h + i] * LOG2E
            c0 = 2 * i * w
            c1 = c0 + w
            p0, l0 = softmax_col([s[0:w, c0:c1] + bias_first, s[w:2 * w, c0:c1], s[2 * w:3 * w, c0:c1] + tri_le],
                                 sink)
            p1, l1 = softmax_col([s[w:2 * w, c1:c1 + w] + tri_ge, s[2 * w:3 * w, c1:c1 + w],
                                  s[3 * w:, c1:c1 + w] + bias_last], sink)
            cols.append(jnp.concatenate([p.astype(_bf16) for p in p0] + [zero_blk], axis=0))
            cols.append(jnp.concatenate([zero_blk] + [p.astype(_bf16) for p in p1], axis=0))
            dens += [l0, l1]
        pt = jnp.concatenate(cols, axis=1)
        o_t = jnp.dot(vwin[h * HEAD_DIM:(h + 1) * HEAD_DIM, :], pt, preferred_element_type=_f32)
        o_t = o_t / jnp.concatenate(dens, axis=1)
        for i in range(Q_PER_KV):
            g = Q_PER_KV * h + i
            ot[g * HEAD_DIM:(g + 1) * HEAD_DIM, :] = o_t[:, i * Q_TILE:(i + 1) * Q_TILE]
        s = s_next
    for c in range(ATTN_WIDTH // LANES):
        o_ref[:, c * LANES:(c + 1) * LANES] = ot[c * LANES:(c + 1) * LANES, :].T.astype(_bf16)


def _attention(seq, q, k, v, sinks):
    t_tokens = q.shape[0]
    nqb = seq // Q_TILE
    return pl.pallas_call(
        _attn_kernel,
        out_shape=jax.ShapeDtypeStruct((t_tokens, ATTN_WIDTH), _bf16),
        grid=(t_tokens // seq, nqb),
        in_specs=[pl.BlockSpec(memory_space=pltpu.SMEM),
                  pl.BlockSpec((Q_TILE, ATTN_WIDTH), lambda b, j: (b * nqb + j, 0)),
                  pl.BlockSpec((seq, KV_WIDTH), lambda b, j: (b, 0)),
                  pl.BlockSpec((seq, KV_WIDTH), lambda b, j: (b, 0))],
        out_specs=pl.BlockSpec((Q_TILE, ATTN_WIDTH), lambda b, j: (b * nqb + j, 0)),
        scratch_shapes=[pltpu.VMEM((N_KV_HEADS, seq + 2 * WINDOW, HEAD_DIM), _bf16),
                        pltpu.VMEM((seq // WINDOW + 2, KV_WIDTH, WINDOW), _bf16),
                        pltpu.VMEM((ATTN_WIDTH, Q_TILE), _f32)],
        compiler_params=pltpu.CompilerParams(dimension_semantics=("parallel", "arbitrary"),
                                             vmem_limit_bytes=VMEM_LIMIT),
        name="attention",
    )(sinks, q, k, v)


def _outproj_kernel(x_ref, mf_ref, at_ref, g1_ref, sh_ref, sc_ref, ng_ref, wof_ref, woa_ref, bo_ref,
                    wr_ref, br_ref, tri_ref, x1_ref, hp_ref, rt_ref, wc_ref, cnt_ref, carry_ref):
    i = pl.program_id(0)

    @pl.when(i == 0)
    def _():
        carry_ref[...] = jnp.zeros_like(carry_ref)

    def project(rows):
        mf = jnp.concatenate([mf_ref[c, rows, :] for c in range(FOURIER_WIDTH // LANES)], axis=1)
        return (jnp.dot(mf.astype(_bf16), wof_ref[...], preferred_element_type=_f32)
                + jnp.dot(at_ref[rows, :], woa_ref[...], preferred_element_type=_f32) + bo_ref[...])

    tiles = _sub_tiles(x_ref.shape[0])
    mix = project(tiles[0])
    logit_cols = []
    for n, rows in enumerate(tiles):
        mix_next = project(tiles[n + 1]) if n + 1 < len(tiles) else None
        x1 = x_ref[rows, :] + g1_ref[0] * mix
        x1_ref[rows, :] = x1
        h = _rms(x1, ng_ref[...]) * (1.0 + sc_ref[0]) + sh_ref[0]
        hp_ref[rows, :] = _pack_halves(h[:, :HALF], h[:, HALF:])
        logit_cols.append(lax.dot_general(wr_ref[...], h.astype(_bf16), (((1,), (1,)), ((), ())),
                                          preferred_element_type=_f32))
        mix = mix_next
    logits = jnp.concatenate(logit_cols, axis=1) + br_ref[...]
    tm = logits.shape[1]
    erow = lax.broadcasted_iota(jnp.int32, logits.shape, 0)
    work = logits
    hots, vals, idxs = [], [], []
    for _k in range(TOP_K):
        mx = jnp.max(work, axis=0, keepdims=True)
        ix = jnp.min(jnp.where(work == mx, erow, N_EXPERTS), axis=0, keepdims=True)
        hot = erow == ix
        work = jnp.where(hot, -jnp.inf, work)
        hots.append(hot)
        vals.append(mx)
        idxs.append(ix)
    es = [jnp.exp(v - vals[0]) for v in vals]
    den = es[0] + es[1] + es[2] + es[3]
    member = jnp.zeros(logits.shape, _f32)
    for hot in hots:
        member = member + hot.astype(_f32)
    carry = carry_ref[...]
    before = (jnp.dot(member.astype(_bf16), tri_ref[...], preferred_element_type=_f32)
              + jnp.concatenate([carry] * (tm // LANES), axis=1))
    r16 = lax.broadcasted_iota(jnp.int32, (ROUTE_ROWS, tm), 0)
    r128 = lax.broadcasted_iota(jnp.int32, (LANES, tm), 0)
    slab = jnp.zeros((ROUTE_ROWS, tm), jnp.int32)
    wrows = jnp.zeros((LANES, tm), _f32)
    for k in range(TOP_K):
        rank = jnp.sum(jnp.where(hots[k], before, 0.0), axis=0, keepdims=True).astype(jnp.int32)
        slab = jnp.where(r16 == k, idxs[k], slab)
        slab = jnp.where(r16 == TOP_K + k, rank, slab)
        wrows = jnp.where(r128 == k, es[k] / den, wrows)
    rt_ref[...] = slab
    wc_ref[...] = wrows.T
    carry = carry + jnp.broadcast_to(jnp.sum(member, axis=1, keepdims=True), carry.shape)
    carry_ref[...] = carry
    cnt_ref[...] = carry.astype(jnp.int32)


def _outproj(seq, x, mf, attn, g1, sh2, sc2, norm_g, wo_f, wo_a, b_o, w_r, b_r, tri):
    t_tokens = x.shape[0]
    tm = TOKEN_TILE
    tiles_per_seq = seq // tm
    row = lambda i: (i, 0)
    per_seq = lambda i: (i // tiles_per_seq, 0, 0)
    const = lambda i: (0, 0)
    return pl.pallas_call(
        _outproj_kernel,
        out_shape=[jax.ShapeDtypeStruct((t_tokens, D_MODEL), _f32),
                   jax.ShapeDtypeStruct((t_tokens, HALF), jnp.int32),
                   jax.ShapeDtypeStruct((ROUTE_ROWS, t_tokens), jnp.int32),
                   jax.ShapeDtypeStruct((t_tokens, LANES), _f32),
                   jax.ShapeDtypeStruct((N_EXPERTS, LANES), jnp.int32)],
        grid=(t_tokens // tm,),
        in_specs=[pl.BlockSpec((tm, D_MODEL), row),
                  pl.BlockSpec((FOURIER_WIDTH // LANES, tm, LANES), lambda i: (0, i, 0)),
                  pl.BlockSpec((tm, ATTN_WIDTH), row),
                  pl.BlockSpec((1, 1, D_MODEL), per_seq), pl.BlockSpec((1, 1, D_MODEL), per_seq),
                  pl.BlockSpec((1, 1, D_MODEL), per_seq),
                  pl.BlockSpec((1, D_MODEL), const),
                  pl.BlockSpec((FOURIER_WIDTH, D_MODEL), const), pl.BlockSpec((ATTN_WIDTH, D_MODEL), const),
                  pl.BlockSpec((1, D_MODEL), const),
                  pl.BlockSpec((N_EXPERTS, D_MODEL), const), pl.BlockSpec((N_EXPERTS, tm), const),
                  pl.BlockSpec((tm, tm), const)],
        out_specs=[pl.BlockSpec((tm, D_MODEL), row), pl.BlockSpec((tm, HALF), row),
                   pl.BlockSpec((ROUTE_ROWS, tm), lambda i: (0, i)), pl.BlockSpec((tm, LANES), row),
                   pl.BlockSpec((N_EXPERTS, LANES), const)],
        scratch_shapes=[pltpu.VMEM((N_EXPERTS, LANES), _f32)],
        compiler_params=pltpu.CompilerParams(dimension_semantics=("arbitrary",), vmem_limit_bytes=VMEM_LIMIT),
        name="outproj",
    )(x, mf, attn, g1, sh2, sc2, norm_g, wo_f, wo_a, b_o, w_r, b_r, tri)


def _expert_kernel(blk_ref, bexp_ref, nact_ref, x_ref, wgu_ref, bgu_ref, wd_ref, bd_ref, y_ref, wgu_bf, wd_bf):
    i = pl.program_id(0)
    prev = bexp_ref[jnp.maximum(i - 1, 0)]

    @pl.when((i == 0) | (bexp_ref[i] != prev))
    def _():
        wgu_bf[...] = wgu_ref[...].astype(_bf16)
        wd_bf[...] = wd_ref[...].astype(_bf16)

    def gate_up(rows):
        xa, xb = _unpack_halves(x_ref[rows, :])
        return (jnp.dot(xa.astype(_bf16), wgu_bf[:HALF, :], preferred_element_type=_f32)
                + jnp.dot(xb.astype(_bf16), wgu_bf[HALF:, :], preferred_element_type=_f32) + bgu_ref[...])

    def activate(gu):
        g = jnp.minimum(gu[:, :D_FF], SWIGLU_LIMIT)
        u = jnp.clip(gu[:, D_FF:], -SWIGLU_LIMIT, SWIGLU_LIMIT)
        return ((u + 1.0) * (g * (1.0 / (1.0 + jnp.exp(-SWIGLU_ALPHA * g))))).astype(_bf16)

    def down(rows, act):
        y = jnp.dot(act, wd_bf[...], preferred_element_type=_f32) + bd_ref[...]
        y_ref[rows, :] = _pack_halves(y[:, :HALF], y[:, HALF:])

    @pl.when(i < nact_ref[0])
    def _():
        tiles = [pl.ds(r, EXPERT_SUB) for r in range(0, EXPERT_BLOCK, EXPERT_SUB)]
        gu = gate_up(tiles[0])
        for n, rows in enumerate(tiles):
            gu_next = gate_up(tiles[n + 1]) if n + 1 < len(tiles) else None
            down(rows, activate(gu))
            gu = gu_next


def _experts(layer, xs, blk, bexp, nact, w_gu, b_gu, w_down, b_down):
    p_rows = xs.shape[0]
    nblk = p_rows // EXPERT_BLOCK
    grid_spec = pltpu.PrefetchScalarGridSpec(
        num_scalar_prefetch=3,
        grid=(nblk,),
        in_specs=[pl.BlockSpec((EXPERT_BLOCK, HALF), lambda i, blk, be, na: (blk[i], 0)),
                  pl.BlockSpec((None, None, D_MODEL, 2 * D_FF), lambda i, blk, be, na: (layer, be[i], 0, 0)),
                  pl.BlockSpec((None, 1, 2 * D_FF), lambda i, blk, be, na: (be[i], 0, 0)),
                  pl.BlockSpec((None, None, D_FF, D_MODEL), lambda i, blk, be, na: (layer, be[i], 0, 0)),
                  pl.BlockSpec((None, 1, D_MODEL), lambda i, blk, be, na: (be[i], 0, 0))],
        out_specs=pl.BlockSpec((EXPERT_BLOCK, HALF), lambda i, blk, be, na: (blk[i], 0)),
        scratch_shapes=[pltpu.VMEM((D_MODEL, 2 * D_FF), _bf16), pltpu.VMEM((D_FF, D_MODEL), _bf16)])
    return pl.pallas_call(
        _expert_kernel,
        out_shape=jax.ShapeDtypeStruct((p_rows, HALF), jnp.int32),
        grid_spec=grid_spec,
        compiler_params=pltpu.CompilerParams(dimension_semantics=("arbitrary",), vmem_limit_bytes=VMEM_LIMIT),
        name="experts",
    )(blk, bexp, nact, xs, w_gu, b_gu.reshape(N_EXPERTS, 1, 2 * D_FF), w_down,
      b_down.reshape(N_EXPERTS, 1, D_MODEL))


def _sc_mesh():
    return plsc.VectorSubcoreMesh(core_axis_name="c", subcore_axis_name="s")


def _sc_worker():
    return lax.axis_index("s") * 2 + lax.axis_index("c")


def _dispatch(hp, dest_t, p_rows):
    n_chunks = dest_t.shape[0]
    per_worker = n_chunks // SC_WORKERS

    @functools.partial(
        pl.kernel, mesh=_sc_mesh(),
        out_type=jax.ShapeDtypeStruct((p_rows, HALF), jnp.int32),
        scratch_types=[pltpu.VMEM((TOP_K, SC_CHUNK), jnp.int32),
                       pltpu.VMEM((SC_CHUNK, HALF), jnp.int32),
                       pltpu.SemaphoreType.DMA],
        name="dispatch")
    def run(hp_hbm, dest_hbm, xs_hbm, idx_v, rows_v, sem):
        base = _sc_worker() * per_worker

        @pl.loop(0, per_worker)
        def _(j):
            chunk = base + j
            pltpu.sync_copy(dest_hbm.at[chunk], idx_v)
            pltpu.sync_copy(hp_hbm.at[pl.ds(chunk * SC_CHUNK, SC_CHUNK)], rows_v)
            copies = [pltpu.async_copy(rows_v, xs_hbm.at[idx_v.at[k]], sem) for k in range(TOP_K)]
            for cp in copies:
                cp.wait()

    return run(hp, dest_t)


def _gather_rows(y, dest_t, t_tokens):
    n_chunks = dest_t.shape[0]
    per_worker = n_chunks // SC_WORKERS

    @functools.partial(
        pl.kernel, mesh=_sc_mesh(),
        out_type=jax.ShapeDtypeStruct((TOP_K, t_tokens, HALF), jnp.int32),
        scratch_types=[pltpu.VMEM((TOP_K, SC_CHUNK), jnp.int32),
                       pltpu.VMEM((SC_CHUNK, HALF), jnp.int32),
                       pltpu.SemaphoreType.DMA],
        name="gather_rows")
    def run(y_hbm, dest_hbm, yk_hbm, idx_v, rows_v, sem):
        base = _sc_worker() * per_worker

        @pl.loop(0, per_worker)
        def _(j):
            chunk = base + j
            pltpu.sync_copy(dest_hbm.at[chunk], idx_v)
            for k in range(TOP_K):
                pltpu.async_copy(y_hbm.at[idx_v.at[k]], rows_v, sem).wait()
                pltpu.sync_copy(rows_v, yk_hbm.at[k, pl.ds(chunk * SC_CHUNK, SC_CHUNK)])

    return run(y, dest_t)


def _final_kernel(x1_ref, yk_ref, wc_ref, g2_ref, ng_ref, o_ref):
    for rows in _sub_tiles(x1_ref.shape[0]):
        x = x1_ref[rows, :] + g2_ref[0] * _moe_combine(yk_ref, wc_ref, rows)
        o_ref[rows, :] = _rms(x, ng_ref[...])


def _final(seq, x1, yk, wcol, g2, final_g):
    t_tokens = x1.shape[0]
    tm = TOKEN_TILE
    tiles_per_seq = seq // tm
    row = lambda i: (i, 0)
    return pl.pallas_call(
        _final_kernel,
        out_shape=jax.ShapeDtypeStruct((t_tokens, D_MODEL), _f32),
        grid=(t_tokens // tm,),
        in_specs=[pl.BlockSpec((tm, D_MODEL), row),
                  pl.BlockSpec((TOP_K, tm, HALF), lambda i: (0, i, 0)),
                  pl.BlockSpec((tm, LANES), row),
                  pl.BlockSpec((1, 1, D_MODEL), lambda i: (i // tiles_per_seq, 0, 0)),
                  pl.BlockSpec((1, D_MODEL), lambda i: (0, 0))],
        out_specs=pl.BlockSpec((tm, D_MODEL), row),
        compiler_params=pltpu.CompilerParams(dimension_semantics=("parallel",), vmem_limit_bytes=VMEM_LIMIT),
        name="final",
    )(x1, yk, wcol, g2, final_g)


def _plan(route, counts, nblk):
    cnt = counts[:, 0]
    padded = ((cnt + EXPERT_BLOCK - 1) // EXPERT_BLOCK) * EXPERT_BLOCK
    pad_end = jnp.cumsum(padded)
    pad_start = pad_end - padded
    expert = jnp.arange(N_EXPERTS, dtype=jnp.int32)[:, None, None]
    start_of = jnp.sum(jnp.where(route[None, :TOP_K] == expert, pad_start[:, None, None], 0), axis=0)
    dest = start_of + route[TOP_K:2 * TOP_K]
    nact = jnp.maximum(pad_end[-1] // EXPERT_BLOCK, 1)
    blk = jnp.minimum(jnp.arange(nblk, dtype=jnp.int32), nact - 1)
    bexp = jnp.minimum(jnp.sum(pad_end[None, :] <= (blk * EXPERT_BLOCK)[:, None], axis=1), N_EXPERTS - 1)
    t_tokens = route.shape[1]
    dest_t = dest.reshape(TOP_K, t_tokens // SC_CHUNK, SC_CHUNK).transpose(1, 0, 2)
    return dest_t, blk.astype(jnp.int32), bexp.astype(jnp.int32), nact.reshape(1).astype(jnp.int32)


def _trunk(x_all, c_all, norm1_g, ada_w, ada_b, w_in, b_in, w_fmix, sinks, w_o, b_o,
           norm2_g, w_router, b_router, w_gu, b_gu, w_down, b_down, final_g):
    nb, seq, _ = x_all.shape
    t_tokens = nb * seq
    assert seq % TOKEN_TILE == 0 and seq % Q_TILE == 0 and seq >= KEY_SPAN
    assert t_tokens % (SC_CHUNK * SC_WORKERS) == 0
    nblk = (t_tokens * TOP_K) // EXPERT_BLOCK + N_EXPERTS
    p_rows = nblk * EXPERT_BLOCK

    mod = _modulation(c_all, ada_w, ada_b)
    mod = mod.reshape(DEPTH, nb, 6, 1, D_MODEL)
    rope_tabs = _rope_tables(seq)
    dft = _dft_tables(seq)
    tri = jnp.triu(jnp.ones((TOKEN_TILE, TOKEN_TILE), _f32), 1).astype(_bf16)

    x = x_all.reshape(t_tokens, D_MODEL)
    moe = None
    for l in range(DEPTH):
        sh1, sc1, g1, sh2, sc2, g2 = (mod[l, :, j] for j in range(6))
        ab = _channel_dft_fold(w_fmix[l], seq)
        outs = _inproj(seq, x, moe, sh1, sc1, norm1_g[l].reshape(1, D_MODEL), w_in[l].astype(_bf16),
                       b_in[l].reshape(1, IN_WIDTH), ab, rope_tabs)
        if moe is None:
            y, q, k, v = outs
        else:
            y, q, k, v, x = outs
        mf = _fourier(seq, y, dft)
        attn = _attention(seq, q, k, v, sinks[l])
        w_o_bf = w_o[l].astype(_bf16)
        w_r = w_router[l].T.astype(_bf16)
        b_r = jnp.broadcast_to(b_router[l][:, None], (N_EXPERTS, TOKEN_TILE))
        x1, hp, route, wcol, counts = _outproj(seq, x, mf, attn, g1, sh2, sc2, norm2_g[l].reshape(1, D_MODEL),
                                               w_o_bf[:FOURIER_WIDTH], w_o_bf[FOURIER_WIDTH:],
                                               b_o[l].reshape(1, D_MODEL), w_r, b_r, tri)
        dest_t, blk, bexp, nact = _plan(route, counts, nblk)
        xs = _dispatch(hp, dest_t, p_rows)
        ys = _experts(l, xs, blk, bexp, nact, w_gu, b_gu[l], w_down, b_down[l])
        yk = _gather_rows(ys, dest_t, t_tokens)
        x = x1
        moe = (yk, wcol, g2)
    out = _final(seq, x, moe[0], moe[1], moe[2], final_g.reshape(1, D_MODEL))
    return out.reshape(nb, seq, D_MODEL)


def kernel(x_prompt, x_sample, c_prompt, c_sample, norm1_g, ada_w, ada_b, w_in, b_in, w_fmix, sinks, w_o, b_o,
           norm2_g, w_router, b_router, w_gu, b_gu, w_down, b_down, final_g):
    ws = (norm1_g, ada_w, ada_b, w_in, b_in, w_fmix, sinks, w_o, b_o,
          norm2_g, w_router, b_router, w_gu, b_gu, w_down, b_down, final_g)
    return _trunk(x_prompt, c_prompt, *ws), _trunk(x_sample, c_sample, *ws)
```

```python
import functools

import jax
import jax.numpy as jnp
from jax import lax
from jax.experimental import pallas as pl
from jax.experimental.pallas import tpu as pltpu
from jax.experimental.pallas import tpu_sc as plsc

D_MODEL = 1024
DEPTH = 4
FOURIER_WIDTH = 256
N_FGROUPS = 4
FGROUP_DIM = 64
HEAD_DIM = 64
N_Q_HEADS = 12
N_KV_HEADS = 4
Q_PER_KV = 3
ATTN_WIDTH = N_Q_HEADS * HEAD_DIM
KV_WIDTH = N_KV_HEADS * HEAD_DIM
IN_WIDTH = FOURIER_WIDTH + ATTN_WIDTH + 2 * KV_WIDTH
WINDOW = 128
ROPE_THETA = 500000.0
ROT_DIM = 16
N_EXPERTS = 32
TOP_K = 4
D_FF = 512
EXPERT_BLOCK = 1024
EXPERT_SUB = 512
SWIGLU_LIMIT = 7.0
SWIGLU_ALPHA = 1.702
NORM_EPS = 1e-5

LANES = 128
HALF = D_MODEL // 2
TOKEN_TILE = 512
SUB_TILE = 256
Q_TILE = 256
KEY_SPAN = Q_TILE + 2 * WINDOW
SC_CHUNK = 128
SC_WORKERS = 32
SC_LANES = 16
COMBINE_CHUNK = 32
VMEM_LIMIT = 56 * 1024 * 1024
NEG_BIG = -1e30
LOG2E = 1.4426950408889634
Q_SCALE = HEAD_DIM ** -0.5 * LOG2E
HI_MASK = -65536
ROUTE_ROWS = 16

_f32 = jnp.float32
_bf16 = jnp.bfloat16


def _pack_halves(a, b):
    ua = lax.bitcast_convert_type(a.astype(_bf16).astype(_f32), jnp.int32)
    ub = lax.bitcast_convert_type(b.astype(_bf16).astype(_f32), jnp.int32)
    return ua | lax.shift_right_logical(ub, 16)


def _unpack_halves(p):
    a = lax.bitcast_convert_type(p & HI_MASK, _f32)
    b = lax.bitcast_convert_type(lax.shift_left(p, 16), _f32)
    return a, b


def _rms(x, g):
    ms = jnp.mean(x * x, axis=-1, keepdims=True)
    return x * lax.rsqrt(ms + NORM_EPS) * g


def _mod_kernel(c_ref, w_ref, b_ref, o_ref):
    c = c_ref[...]
    s = c * (1.0 / (1.0 + jnp.exp(-c)))
    o_ref[0] = jnp.dot(s.astype(_bf16), w_ref[0].astype(_bf16), preferred_element_type=_f32) + b_ref[0]


def _modulation(c_all, ada_w, ada_b):
    nb = c_all.shape[0]
    ncol = ada_w.shape[2] // D_MODEL
    return pl.pallas_call(
        _mod_kernel,
        out_shape=jax.ShapeDtypeStruct((DEPTH, nb, 6 * D_MODEL), _f32),
        grid=(DEPTH, ncol),
        in_specs=[
            pl.BlockSpec((nb, D_MODEL), lambda l, j: (0, 0)),
            pl.BlockSpec((1, D_MODEL, D_MODEL), lambda l, j: (l, 0, j)),
            pl.BlockSpec((1, 1, D_MODEL), lambda l, j: (l, 0, j)),
        ],
        out_specs=pl.BlockSpec((1, nb, D_MODEL), lambda l, j: (l, 0, j)),
        name="modulation",
    )(c_all, ada_w, ada_b.reshape(DEPTH, 1, 6 * D_MODEL))


def _inproj_matmul(x, sh_ref, sc_ref, ng_ref, w_ref, b_ref, *_):
    h = _rms(x, ng_ref[...]) * (1.0 + sc_ref[0]) + sh_ref[0]
    return jnp.dot(h.astype(_bf16), w_ref[...], preferred_element_type=_f32) + b_ref[...]


def _inproj_finish(rows, z, sh_ref, sc_ref, ng_ref, w_ref, b_ref, ab_ref, rc_ref, rs1_ref, rs2_ref,
                   y_ref, q_ref, k_ref, v_ref):
    f = z[:, :FOURIER_WIDTH]
    y_ref[rows, :] = jnp.dot(f.astype(_bf16), ab_ref[...], preferred_element_type=_f32).astype(_bf16)
    rc = rc_ref[rows, :]
    rs1 = rs1_ref[rows, :]
    rs2 = rs2_ref[rows, :]

    def rope(t):
        return t * rc + pltpu.roll(t, LANES - ROT_DIM // 2, 1) * rs1 + pltpu.roll(t, ROT_DIM // 2, 1) * rs2

    q0 = FOURIER_WIDTH
    for c in range(ATTN_WIDTH // LANES):
        t = z[:, q0 + c * LANES:q0 + (c + 1) * LANES]
        q_ref[rows, c * LANES:(c + 1) * LANES] = (rope(t) * Q_SCALE).astype(_bf16)
    k0 = q0 + ATTN_WIDTH
    for c in range(KV_WIDTH // LANES):
        t = z[:, k0 + c * LANES:k0 + (c + 1) * LANES]
        k_ref[rows, c * LANES:(c + 1) * LANES] = rope(t).astype(_bf16)
    v_ref[rows, :] = z[:, k0 + KV_WIDTH:].astype(_bf16)


def _sub_tiles(n_rows):
    return [pl.ds(r, SUB_TILE) for r in range(0, n_rows, SUB_TILE)]


def _inproj_kernel(x_ref, *refs):
    tiles = _sub_tiles(x_ref.shape[0])
    z = _inproj_matmul(x_ref[tiles[0], :], *refs)
    for n, rows in enumerate(tiles):
        z_next = _inproj_matmul(x_ref[tiles[n + 1], :], *refs) if n + 1 < len(tiles) else None
        _inproj_finish(rows, z, *refs)
        z = z_next


def _inproj(seq, x, sh1, sc1, norm_g, w_in, b_in, ab, rope_tabs):
    t_tokens = x.shape[0]
    tm = TOKEN_TILE
    tiles_per_seq = seq // tm
    row = lambda i: (i, 0)
    per_seq = lambda i: (i // tiles_per_seq, 0, 0)
    const = lambda i: (0, 0)
    pos = lambda i: (i % tiles_per_seq, 0)
    return pl.pallas_call(
        _inproj_kernel,
        out_shape=[jax.ShapeDtypeStruct((t_tokens, 2 * FOURIER_WIDTH), _bf16),
                   jax.ShapeDtypeStruct((t_tokens, ATTN_WIDTH), _bf16),
                   jax.ShapeDtypeStruct((t_tokens, KV_WIDTH), _bf16),
                   jax.ShapeDtypeStruct((t_tokens, KV_WIDTH), _bf16)],
        grid=(t_tokens // tm,),
        in_specs=[pl.BlockSpec((tm, D_MODEL), row),
                  pl.BlockSpec((1, 1, D_MODEL), per_seq), pl.BlockSpec((1, 1, D_MODEL), per_seq),
                  pl.BlockSpec((1, D_MODEL), const),
                  pl.BlockSpec((D_MODEL, IN_WIDTH), const), pl.BlockSpec((1, IN_WIDTH), const),
                  pl.BlockSpec((FOURIER_WIDTH, 2 * FOURIER_WIDTH), const),
                  pl.BlockSpec((tm, LANES), pos), pl.BlockSpec((tm, LANES), pos), pl.BlockSpec((tm, LANES), pos)],
        out_specs=[pl.BlockSpec((tm, 2 * FOURIER_WIDTH), row), pl.BlockSpec((tm, ATTN_WIDTH), row),
                   pl.BlockSpec((tm, KV_WIDTH), row), pl.BlockSpec((tm, KV_WIDTH), row)],
        compiler_params=pltpu.CompilerParams(dimension_semantics=("parallel",), vmem_limit_bytes=VMEM_LIMIT),
        name="inproj",
    )(x, sh1, sc1, norm_g, w_in, b_in, ab, *rope_tabs)


def _fourier_kernel(y_ref, ce_ref, se_ref, co_ref, so_ref, o_ref):
    half = y_ref.shape[0] // 2
    lo = y_ref[:half, :].astype(_f32)
    hi = y_ref[half:, :].astype(_f32)
    ye = (lo + hi).astype(_bf16)
    yo = (lo - hi).astype(_bf16)
    w = FOURIER_WIDTH
    even = (jnp.dot(ce_ref[...], ye[:, :w], preferred_element_type=_f32)
            + jnp.dot(se_ref[...], ye[:, w:], preferred_element_type=_f32))
    odd = (jnp.dot(co_ref[...], yo[:, :w], preferred_element_type=_f32)
           + jnp.dot(so_ref[...], yo[:, w:], preferred_element_type=_f32))
    for c in range(FOURIER_WIDTH // LANES):
        o_ref[c, pl.ds(0, half, stride=2), :] = even[:, c * LANES:(c + 1) * LANES]
        o_ref[c, pl.ds(1, half, stride=2), :] = odd[:, c * LANES:(c + 1) * LANES]


def _fourier(seq, y, dft):
    t_tokens = y.shape[0]
    half = seq // 2
    const = lambda b: (0, 0)
    return pl.pallas_call(
        _fourier_kernel,
        out_shape=jax.ShapeDtypeStruct((FOURIER_WIDTH // LANES, t_tokens, LANES), _f32),
        grid=(t_tokens // seq,),
        in_specs=[pl.BlockSpec((seq, 2 * FOURIER_WIDTH), lambda b: (b, 0))]
        + [pl.BlockSpec((half, half), const)] * 4,
        out_specs=pl.BlockSpec((FOURIER_WIDTH // LANES, seq, LANES), lambda b: (0, b, 0)),
        compiler_params=pltpu.CompilerParams(dimension_semantics=("parallel",), vmem_limit_bytes=VMEM_LIMIT),
        name="fourier",
    )(y, *dft)


def _dft_tables(seq):
    half = seq // 2
    j = jnp.arange(half, dtype=jnp.int32)[:, None]
    k = jnp.arange(half, dtype=jnp.int32)[None, :]
    ang_e = ((2 * j * k) % seq).astype(_f32) * (2.0 * jnp.pi / seq)
    ang_o = (((2 * j + 1) * k) % seq).astype(_f32) * (2.0 * jnp.pi / seq)
    return (jnp.cos(ang_e).astype(_bf16), jnp.sin(ang_e).astype(_bf16),
            jnp.cos(ang_o).astype(_bf16), jnp.sin(ang_o).astype(_bf16))


def _channel_dft_fold(w_fmix, seq):
    c = jnp.arange(FGROUP_DIM, dtype=jnp.int32)
    ang = ((c[:, None] * c[None, :]) % FGROUP_DIM).astype(_f32) * (2.0 * jnp.pi / FGROUP_DIM)
    scale = (seq * FGROUP_DIM) ** -0.5
    hp = lax.Precision.HIGHEST
    a = jnp.einsum('cm,gmd->gcd', jnp.cos(ang) * scale, w_fmix, precision=hp)
    b = jnp.einsum('cm,gmd->gcd', -jnp.sin(ang) * scale, w_fmix, precision=hp)
    eye = jnp.eye(N_FGROUPS, dtype=_f32)
    bd = lambda m: jnp.einsum('gcd,gh->gchd', m, eye).reshape(FOURIER_WIDTH, FOURIER_WIDTH)
    return jnp.concatenate([bd(a), bd(b)], axis=1).astype(_bf16)


def _rope_tables(seq):
    half = ROT_DIM // 2
    inv_freq = jnp.power(ROPE_THETA, -jnp.arange(0, ROT_DIM, 2, dtype=_f32) / ROT_DIM)
    ang = jnp.arange(seq, dtype=_f32)[:, None] * inv_freq[None, :]
    cos, sin = jnp.cos(ang), jnp.sin(ang)
    ones = jnp.ones((seq, HEAD_DIM - ROT_DIM), _f32)
    zeros = jnp.zeros((seq, HEAD_DIM - ROT_DIM), _f32)
    zh = jnp.zeros((seq, half), _f32)
    c1 = jnp.concatenate([cos, cos, ones], axis=1)
    s1 = jnp.concatenate([-sin, zh, zeros], axis=1)
    s2 = jnp.concatenate([zh, sin, zeros], axis=1)
    rep = LANES // HEAD_DIM
    return tuple(jnp.tile(t, (1, rep)) for t in (c1, s1, s2))


def _attn_kernel(sink_ref, q_ref, k_ref, v_ref, o_ref, kpad, vtb, ot):
    seq = k_ref.shape[0]
    nqb = seq // Q_TILE
    nkb = seq // WINDOW
    qb = pl.program_id(1)

    @pl.when(qb == 0)
    def _():
        zk = jnp.zeros((WINDOW, HEAD_DIM), _bf16)
        for h in range(N_KV_HEADS):
            kpad[h, :WINDOW, :] = zk
            kpad[h, WINDOW + seq:, :] = zk
            kpad[h, WINDOW:WINDOW + seq, :] = k_ref[:, h * HEAD_DIM:(h + 1) * HEAD_DIM]
        zv = jnp.zeros((KV_WIDTH, WINDOW), _bf16)
        vtb[0] = zv
        vtb[nkb + 1] = zv
        for j in range(nkb):
            vtb[j + 1] = v_ref[j * WINDOW:(j + 1) * WINDOW, :].astype(_f32).T.astype(_bf16)

    r0 = pl.multiple_of(qb * Q_TILE, Q_TILE)
    kb0 = qb * (Q_TILE // WINDOW)
    ka = lax.broadcasted_iota(jnp.int32, (WINDOW, WINDOW), 0)
    qc = lax.broadcasted_iota(jnp.int32, (WINDOW, WINDOW), 1)
    tri_ge = jnp.where(ka >= qc, 0.0, NEG_BIG).astype(_f32)
    tri_le = jnp.where(ka <= qc, 0.0, NEG_BIG).astype(_f32)
    bias_first = jnp.where(qb == 0, NEG_BIG, tri_ge)
    bias_last = jnp.where(qb == nqb - 1, NEG_BIG, tri_le)
    zero_blk = jnp.zeros((WINDOW, WINDOW), _bf16)
    vwin = jnp.concatenate([vtb[kb0 + j] for j in range(KEY_SPAN // WINDOW)], axis=1)

    def softmax_col(blocks, sink):
        m = blocks[0]
        for b in blocks[1:]:
            m = jnp.maximum(m, b)
        m = jnp.maximum(jnp.max(m, axis=0, keepdims=True), sink)
        ps = [jnp.exp2(b - m) for b in blocks]
        tot = ps[0]
        for p in ps[1:]:
            tot = tot + p
        return ps, jnp.sum(tot, axis=0, keepdims=True) + jnp.exp2(sink - m)

    w = WINDOW

    def scores(h):
        kh = kpad[h, pl.ds(r0, KEY_SPAN), :]
        q3 = jnp.concatenate([q_ref[:, g * HEAD_DIM:(g + 1) * HEAD_DIM]
                              for g in range(Q_PER_KV * h, Q_PER_KV * (h + 1))], axis=0)
        return lax.dot_general(kh, q3, (((1,), (1,)), ((), ())), preferred_element_type=_f32)

    s = scores(0)
    for h in range(N_KV_HEADS):
        s_next = scores(h + 1) if h + 1 < N_KV_HEADS else None
        cols, dens = [], []
        for i in range(Q_PER_KV):
            sink = sink_ref[Q_PER_KV * h + i] * LOG2E
            c0 = 2 * i * w
            c1 = c0 + w
            p0, l0 = softmax_col([s[0:w, c0:c1] + bias_first, s[w:2 * w, c0:c1], s[2 * w:3 * w, c0:c1] + tri_le],
                                 sink)
            p1, l1 = softmax_col([s[w:2 * w, c1:c1 + w] + tri_ge, s[2 * w:3 * w, c1:c1 + w],
                                  s[3 * w:, c1:c1 + w] + bias_last], sink)
            cols.append(jnp.concatenate([p.astype(_bf16) for p in p0] + [zero_blk], axis=0))
            cols.append(jnp.concatenate([zero_blk] + [p.astype(_bf16) for p in p1], axis=0))
            dens += [l0, l1]
        pt = jnp.concatenate(cols, axis=1)
        o_t = jnp.dot(vwin[h * HEAD_DIM:(h + 1) * HEAD_DIM, :], pt, preferred_element_type=_f32)
        o_t = o_t / jnp.concatenate(dens, axis=1)
        for i in range(Q_PER_KV):
            g = Q_PER_KV * h + i
            ot[g * HEAD_DIM:(g + 1) * HEAD_DIM, :] = o_t[:, i * Q_TILE:(i + 1) * Q_TILE]
        s = s_next
    for c in range(ATTN_WIDTH // LANES):
        o_ref[:, c * LANES:(c + 1) * LANES] = ot[c * LANES:(c + 1) * LANES, :].T.astype(_bf16)


def _attention(seq, q, k, v, sinks):
    t_tokens = q.shape[0]
    nqb = seq // Q_TILE
    return pl.pallas_call(
        _attn_kernel,
        out_shape=jax.ShapeDtypeStruct((t_tokens, ATTN_WIDTH), _bf16),
        grid=(t_tokens // seq, nqb),
        in_specs=[pl.BlockSpec(memory_space=pltpu.SMEM),
                  pl.BlockSpec((Q_TILE, ATTN_WIDTH), lambda b, j: (b * nqb + j, 0)),
                  pl.BlockSpec((seq, KV_WIDTH), lambda b, j: (b, 0)),
                  pl.BlockSpec((seq, KV_WIDTH), lambda b, j: (b, 0))],
        out_specs=pl.BlockSpec((Q_TILE, ATTN_WIDTH), lambda b, j: (b * nqb + j, 0)),
        scratch_shapes=[pltpu.VMEM((N_KV_HEADS, seq + 2 * WINDOW, HEAD_DIM), _bf16),
                        pltpu.VMEM((seq // WINDOW + 2, KV_WIDTH, WINDOW), _bf16),
                        pltpu.VMEM((ATTN_WIDTH, Q_TILE), _f32)],
        compiler_params=pltpu.CompilerParams(dimension_semantics=("parallel", "arbitrary"),
                                             vmem_limit_bytes=VMEM_LIMIT),
        name="attention",
    )(sinks, q, k, v)


def _outproj_kernel(x_ref, mf_ref, at_ref, g1_ref, sh_ref, sc_ref, ng_ref, wof_ref, woa_ref, bo_ref,
                    wr_ref, br_ref, tri_ref, x1_ref, hp_ref, rt_ref, cnt_ref, carry_ref):
    i = pl.program_id(0)

    @pl.when(i == 0)
    def _():
        carry_ref[...] = jnp.zeros_like(carry_ref)

    def project(rows):
        mf = jnp.concatenate([mf_ref[c, rows, :] for c in range(FOURIER_WIDTH // LANES)], axis=1)
        return (jnp.dot(mf.astype(_bf16), wof_ref[...], preferred_element_type=_f32)
                + jnp.dot(at_ref[rows, :], woa_ref[...], preferred_element_type=_f32) + bo_ref[...])

    tiles = _sub_tiles(x_ref.shape[0])
    mix = project(tiles[0])
    logit_cols = []
    for n, rows in enumerate(tiles):
        mix_next = project(tiles[n + 1]) if n + 1 < len(tiles) else None
        x1 = x_ref[rows, :] + g1_ref[0] * mix
        x1_ref[rows, :] = x1
        h = _rms(x1, ng_ref[...]) * (1.0 + sc_ref[0]) + sh_ref[0]
        hp_ref[rows, :] = _pack_halves(h[:, :HALF], h[:, HALF:])
        logit_cols.append(lax.dot_general(wr_ref[...], h.astype(_bf16), (((1,), (1,)), ((), ())),
                                          preferred_element_type=_f32))
        mix = mix_next
    logits = jnp.concatenate(logit_cols, axis=1) + br_ref[...]
    tm = logits.shape[1]
    erow = lax.broadcasted_iota(jnp.int32, logits.shape, 0)
    work = logits
    hots, vals, idxs = [], [], []
    for _k in range(TOP_K):
        mx = jnp.max(work, axis=0, keepdims=True)
        ix = jnp.min(jnp.where(work == mx, erow, N_EXPERTS), axis=0, keepdims=True)
        hot = erow == ix
        work = jnp.where(hot, -jnp.inf, work)
        hots.append(hot)
        vals.append(mx)
        idxs.append(ix)
    es = [jnp.exp(v - vals[0]) for v in vals]
    den = es[0] + es[1] + es[2] + es[3]
    member = jnp.zeros(logits.shape, _f32)
    for hot in hots:
        member = member + hot.astype(_f32)
    carry = carry_ref[...]
    before = (jnp.dot(member.astype(_bf16), tri_ref[...], preferred_element_type=_f32)
              + jnp.concatenate([carry] * (tm // LANES), axis=1))
    r16 = lax.broadcasted_iota(jnp.int32, (ROUTE_ROWS, tm), 0)
    slab = jnp.zeros((ROUTE_ROWS, tm), jnp.int32)
    for k in range(TOP_K):
        rank = jnp.sum(jnp.where(hots[k], before, 0.0), axis=0, keepdims=True).astype(jnp.int32)
        slab = jnp.where(r16 == k, idxs[k], slab)
        slab = jnp.where(r16 == TOP_K + k, rank, slab)
        slab = jnp.where(r16 == 2 * TOP_K + k, lax.bitcast_convert_type(es[k] / den, jnp.int32), slab)
    rt_ref[...] = slab
    carry = carry + jnp.broadcast_to(jnp.sum(member, axis=1, keepdims=True), carry.shape)
    carry_ref[...] = carry
    cnt_ref[...] = carry.astype(jnp.int32)


def _outproj(seq, x, mf, attn, g1, sh2, sc2, norm_g, wo_f, wo_a, b_o, w_r, b_r, tri):
    t_tokens = x.shape[0]
    tm = TOKEN_TILE
    tiles_per_seq = seq // tm
    row = lambda i: (i, 0)
    per_seq = lambda i: (i // tiles_per_seq, 0, 0)
    const = lambda i: (0, 0)
    return pl.pallas_call(
        _outproj_kernel,
        out_shape=[jax.ShapeDtypeStruct((t_tokens, D_MODEL), _f32),
                   jax.ShapeDtypeStruct((t_tokens, HALF), jnp.int32),
                   jax.ShapeDtypeStruct((ROUTE_ROWS, t_tokens), jnp.int32),
                   jax.ShapeDtypeStruct((N_EXPERTS, LANES), jnp.int32)],
        grid=(t_tokens // tm,),
        in_specs=[pl.BlockSpec((tm, D_MODEL), row),
                  pl.BlockSpec((FOURIER_WIDTH // LANES, tm, LANES), lambda i: (0, i, 0)),
                  pl.BlockSpec((tm, ATTN_WIDTH), row),
                  pl.BlockSpec((1, 1, D_MODEL), per_seq), pl.BlockSpec((1, 1, D_MODEL), per_seq),
                  pl.BlockSpec((1, 1, D_MODEL), per_seq),
                  pl.BlockSpec((1, D_MODEL), const),
                  pl.BlockSpec((FOURIER_WIDTH, D_MODEL), const), pl.BlockSpec((ATTN_WIDTH, D_MODEL), const),
                  pl.BlockSpec((1, D_MODEL), const),
                  pl.BlockSpec((N_EXPERTS, D_MODEL), const), pl.BlockSpec((N_EXPERTS, tm), const),
                  pl.BlockSpec((tm, tm), const)],
        out_specs=[pl.BlockSpec((tm, D_MODEL), row), pl.BlockSpec((tm, HALF), row),
                   pl.BlockSpec((ROUTE_ROWS, tm), lambda i: (0, i)),
                   pl.BlockSpec((N_EXPERTS, LANES), const)],
        scratch_shapes=[pltpu.VMEM((N_EXPERTS, LANES), _f32)],
        compiler_params=pltpu.CompilerParams(dimension_semantics=("arbitrary",), vmem_limit_bytes=VMEM_LIMIT),
        name="outproj",
    )(x, mf, attn, g1, sh2, sc2, norm_g, wo_f, wo_a, b_o, w_r, b_r, tri)


def _expert_kernel(blk_ref, bexp_ref, nact_ref, x_ref, wgu_ref, bgu_ref, wd_ref, bd_ref, y_ref, wgu_bf, wd_bf):
    i = pl.program_id(0)
    prev = bexp_ref[jnp.maximum(i - 1, 0)]

    @pl.when((i == 0) | (bexp_ref[i] != prev))
    def _():
        wgu_bf[...] = wgu_ref[...].astype(_bf16)
        wd_bf[...] = wd_ref[...].astype(_bf16)

    def gate_up(rows):
        xa, xb = _unpack_halves(x_ref[rows, :])
        return (jnp.dot(xa.astype(_bf16), wgu_bf[:HALF, :], preferred_element_type=_f32)
                + jnp.dot(xb.astype(_bf16), wgu_bf[HALF:, :], preferred_element_type=_f32) + bgu_ref[...])

    def activate(gu):
        g = jnp.minimum(gu[:, :D_FF], SWIGLU_LIMIT)
        u = jnp.clip(gu[:, D_FF:], -SWIGLU_LIMIT, SWIGLU_LIMIT)
        return ((u + 1.0) * (g * (1.0 / (1.0 + jnp.exp(-SWIGLU_ALPHA * g))))).astype(_bf16)

    def down(rows, act):
        y = jnp.dot(act, wd_bf[...], preferred_element_type=_f32) + bd_ref[...]
        y_ref[rows, :] = _pack_halves(y[:, :HALF], y[:, HALF:])

    @pl.when(i < nact_ref[0])
    def _():
        tiles = [pl.ds(r, EXPERT_SUB) for r in range(0, EXPERT_BLOCK, EXPERT_SUB)]
        gu = gate_up(tiles[0])
        for n, rows in enumerate(tiles):
            gu_next = gate_up(tiles[n + 1]) if n + 1 < len(tiles) else None
            down(rows, activate(gu))
            gu = gu_next


def _experts(layer, xs, blk, bexp, nact, w_gu, b_gu, w_down, b_down):
    p_rows = xs.shape[0]
    nblk = p_rows // EXPERT_BLOCK
    grid_spec = pltpu.PrefetchScalarGridSpec(
        num_scalar_prefetch=3,
        grid=(nblk,),
        in_specs=[pl.BlockSpec((EXPERT_BLOCK, HALF), lambda i, blk, be, na: (blk[i], 0)),
                  pl.BlockSpec((None, None, D_MODEL, 2 * D_FF), lambda i, blk, be, na: (layer, be[i], 0, 0)),
                  pl.BlockSpec((None, 1, 2 * D_FF), lambda i, blk, be, na: (be[i], 0, 0)),
                  pl.BlockSpec((None, None, D_FF, D_MODEL), lambda i, blk, be, na: (layer, be[i], 0, 0)),
                  pl.BlockSpec((None, 1, D_MODEL), lambda i, blk, be, na: (be[i], 0, 0))],
        out_specs=pl.BlockSpec((EXPERT_BLOCK, HALF), lambda i, blk, be, na: (blk[i], 0)),
        scratch_shapes=[pltpu.VMEM((D_MODEL, 2 * D_FF), _bf16), pltpu.VMEM((D_FF, D_MODEL), _bf16)])
    return pl.pallas_call(
        _expert_kernel,
        out_shape=jax.ShapeDtypeStruct((p_rows, HALF), jnp.int32),
        grid_spec=grid_spec,
        compiler_params=pltpu.CompilerParams(dimension_semantics=("arbitrary",), vmem_limit_bytes=VMEM_LIMIT),
        name="experts",
    )(blk, bexp, nact, xs, w_gu, b_gu.reshape(N_EXPERTS, 1, 2 * D_FF), w_down,
      b_down.reshape(N_EXPERTS, 1, D_MODEL))


def _sc_mesh():
    return plsc.VectorSubcoreMesh(core_axis_name="c", subcore_axis_name="s")


def _sc_worker():
    return lax.axis_index("s") * 2 + lax.axis_index("c")


def _dispatch(hp, dest_t, p_rows):
    n_chunks = dest_t.shape[0]
    per_worker = n_chunks // SC_WORKERS

    @functools.partial(
        pl.kernel, mesh=_sc_mesh(),
        out_type=jax.ShapeDtypeStruct((p_rows, HALF), jnp.int32),
        scratch_types=[pltpu.VMEM((TOP_K, SC_CHUNK), jnp.int32),
                       pltpu.VMEM((SC_CHUNK, HALF), jnp.int32),
                       pltpu.SemaphoreType.DMA],
        name="dispatch")
    def run(hp_hbm, dest_hbm, xs_hbm, idx_v, rows_v, sem):
        base = _sc_worker() * per_worker

        @pl.loop(0, per_worker)
        def _(j):
            chunk = base + j
            pltpu.sync_copy(dest_hbm.at[chunk], idx_v)
            pltpu.sync_copy(hp_hbm.at[pl.ds(chunk * SC_CHUNK, SC_CHUNK)], rows_v)
            copies = [pltpu.async_copy(rows_v, xs_hbm.at[idx_v.at[k]], sem) for k in range(TOP_K)]
            for cp in copies:
                cp.wait()

    return run(hp, dest_t)


def _combine_rows(y, dest_c, wb, x1, g2, seq):
    t_tokens = x1.shape[0]
    n_chunks = dest_c.shape[0]
    per_worker = n_chunks // SC_WORKERS
    chunks_per_seq = seq // COMBINE_CHUNK
    n_vec = HALF // SC_LANES

    @functools.partial(
        pl.kernel, mesh=_sc_mesh(),
        out_type=jax.ShapeDtypeStruct((t_tokens, D_MODEL), _f32),
        scratch_types=[pltpu.VMEM((TOP_K, COMBINE_CHUNK), jnp.int32),
                       pltpu.VMEM((TOP_K, COMBINE_CHUNK, HALF), jnp.int32),
                       pltpu.VMEM((COMBINE_CHUNK, D_MODEL), _f32),
                       pltpu.VMEM((COMBINE_CHUNK, TOP_K, SC_LANES), _f32),
                       pltpu.VMEM((D_MODEL,), _f32),
                       pltpu.SemaphoreType.DMA],
        compiler_params=pltpu.CompilerParams(needs_layout_passes=False),
        name="combine_rows")
    def run(y_hbm, dest_hbm, wb_hbm, x1_hbm, g2_hbm, x2_hbm, idx_v, rows_v, x_v, w_v, g_v, sem):
        base = _sc_worker() * per_worker

        @pl.loop(0, per_worker)
        def _(c):
            chunk = base + c
            tok0 = chunk * COMBINE_CHUNK
            pltpu.sync_copy(dest_hbm.at[chunk], idx_v)
            gathers = [pltpu.async_copy(y_hbm.at[idx_v.at[k]], rows_v.at[k], sem) for k in range(TOP_K)]
            pltpu.sync_copy(x1_hbm.at[pl.ds(tok0, COMBINE_CHUNK)], x_v)
            pltpu.sync_copy(wb_hbm.at[pl.ds(tok0, COMBINE_CHUNK)], w_v)
            pltpu.sync_copy(g2_hbm.at[chunk // chunks_per_seq], g_v)
            for cp in gathers:
                cp.wait()

            @pl.loop(0, COMBINE_CHUNK)
            def _(t):
                ws = [w_v[t, k, :] for k in range(TOP_K)]

                @plsc.parallel_loop(0, n_vec)
                def _(j):
                    lo = pl.ds(j * SC_LANES, SC_LANES)
                    hi = pl.ds(HALF + j * SC_LANES, SC_LANES)
                    acc_a = None
                    acc_b = None
                    for k in range(TOP_K):
                        p = rows_v[k, t, lo]
                        a = lax.bitcast_convert_type(p & HI_MASK, _f32) * ws[k]
                        b = lax.bitcast_convert_type(p << 16, _f32) * ws[k]
                        acc_a = a if acc_a is None else acc_a + a
                        acc_b = b if acc_b is None else acc_b + b
                    x_v[t, lo] = x_v[t, lo] + g_v[lo] * acc_a
                    x_v[t, hi] = x_v[t, hi] + g_v[hi] * acc_b

            pltpu.sync_copy(x_v, x2_hbm.at[pl.ds(tok0, COMBINE_CHUNK)])

    return run(y, dest_c, wb, x1, g2)


def _final_kernel(x_ref, ng_ref, o_ref):
    o_ref[...] = _rms(x_ref[...], ng_ref[...])


def _final(x, final_g):
    t_tokens = x.shape[0]
    tm = 2 * TOKEN_TILE
    row = lambda i: (i, 0)
    return pl.pallas_call(
        _final_kernel,
        out_shape=jax.ShapeDtypeStruct((t_tokens, D_MODEL), _f32),
        grid=(t_tokens // tm,),
        in_specs=[pl.BlockSpec((tm, D_MODEL), row), pl.BlockSpec((1, D_MODEL), lambda i: (0, 0))],
        out_specs=pl.BlockSpec((tm, D_MODEL), row),
        compiler_params=pltpu.CompilerParams(dimension_semantics=("parallel",), vmem_limit_bytes=VMEM_LIMIT),
        name="final",
    )(x, final_g)


def _plan(route, counts, nblk):
    cnt = counts[:, 0]
    padded = ((cnt + EXPERT_BLOCK - 1) // EXPERT_BLOCK) * EXPERT_BLOCK
    pad_end = jnp.cumsum(padded)
    pad_start = pad_end - padded
    expert = jnp.arange(N_EXPERTS, dtype=jnp.int32)[:, None, None]
    start_of = jnp.sum(jnp.where(route[None, :TOP_K] == expert, pad_start[:, None, None], 0), axis=0)
    dest = start_of + route[TOP_K:2 * TOP_K]
    nact = jnp.maximum(pad_end[-1] // EXPERT_BLOCK, 1)
    blk = jnp.minimum(jnp.arange(nblk, dtype=jnp.int32), nact - 1)
    bexp = jnp.minimum(jnp.sum(pad_end[None, :] <= (blk * EXPERT_BLOCK)[:, None], axis=1), N_EXPERTS - 1)
    t_tokens = route.shape[1]
    dest_t = dest.reshape(TOP_K, t_tokens // SC_CHUNK, SC_CHUNK).transpose(1, 0, 2)
    dest_c = dest.reshape(TOP_K, t_tokens // COMBINE_CHUNK, COMBINE_CHUNK).transpose(1, 0, 2)
    w_t = lax.bitcast_convert_type(route[2 * TOP_K:3 * TOP_K], _f32)
    wb = jnp.broadcast_to(w_t.T[:, :, None], (t_tokens, TOP_K, SC_LANES))
    return dest_t, dest_c, wb, blk.astype(jnp.int32), bexp.astype(jnp.int32), nact.reshape(1).astype(jnp.int32)


def _trunk(x_all, c_all, norm1_g, ada_w, ada_b, w_in, b_in, w_fmix, sinks, w_o, b_o,
           norm2_g, w_router, b_router, w_gu, b_gu, w_down, b_down, final_g):
    nb, seq, _ = x_all.shape
    t_tokens = nb * seq
    assert seq % TOKEN_TILE == 0 and seq % Q_TILE == 0 and seq >= KEY_SPAN
    assert t_tokens % (SC_CHUNK * SC_WORKERS) == 0 and seq % COMBINE_CHUNK == 0
    nblk = (t_tokens * TOP_K) // EXPERT_BLOCK + N_EXPERTS
    p_rows = nblk * EXPERT_BLOCK

    mod = _modulation(c_all, ada_w, ada_b)
    mod = mod.reshape(DEPTH, nb, 6, 1, D_MODEL)
    rope_tabs = _rope_tables(seq)
    dft = _dft_tables(seq)
    tri = jnp.triu(jnp.ones((TOKEN_TILE, TOKEN_TILE), _f32), 1).astype(_bf16)

    x = x_all.reshape(t_tokens, D_MODEL)
    for l in range(DEPTH):
        sh1, sc1, g1, sh2, sc2, g2 = (mod[l, :, j] for j in range(6))
        ab = _channel_dft_fold(w_fmix[l], seq)
        y, q, k, v = _inproj(seq, x, sh1, sc1, norm1_g[l].reshape(1, D_MODEL), w_in[l].astype(_bf16),
                             b_in[l].reshape(1, IN_WIDTH), ab, rope_tabs)
        mf = _fourier(seq, y, dft)
        attn = _attention(seq, q, k, v, sinks[l])
        w_o_bf = w_o[l].astype(_bf16)
        w_r = w_router[l].T.astype(_bf16)
        b_r = jnp.broadcast_to(b_router[l][:, None], (N_EXPERTS, TOKEN_TILE))
        x1, hp, route, counts = _outproj(seq, x, mf, attn, g1, sh2, sc2, norm2_g[l].reshape(1, D_MODEL),
                                         w_o_bf[:FOURIER_WIDTH], w_o_bf[FOURIER_WIDTH:],
                                         b_o[l].reshape(1, D_MODEL), w_r, b_r, tri)
        dest_t, dest_c, wb, blk, bexp, nact = _plan(route, counts, nblk)
        xs = _dispatch(hp, dest_t, p_rows)
        ys = _experts(l, xs, blk, bexp, nact, w_gu, b_gu[l], w_down, b_down[l])
        x = _combine_rows(ys, dest_c, wb, x1, g2.reshape(nb, D_MODEL), seq)
    out = _final(x, final_g.reshape(1, D_MODEL))
    return out.reshape(nb, seq, D_MODEL)


def kernel(x_prompt, x_sample, c_prompt, c_sample, norm1_g, ada_w, ada_b, w_in, b_in, w_fmix, sinks, w_o, b_o,
           norm2_g, w_router, b_router, w_gu, b_gu, w_down, b_down, final_g):
    ws = (norm1_g, ada_w, ada_b, w_in, b_in, w_fmix, sinks, w_o, b_o,
          norm2_g, w_router, b_router, w_gu, b_gu, w_down, b_down, final_g)
    return _trunk(x_prompt, c_prompt, *ws), _trunk(x_sample, c_sample, *ws)
```

```python
import functools

import jax
import jax.numpy as jnp
from jax import lax
from jax.experimental import pallas as pl
from jax.experimental.pallas import tpu as pltpu
from jax.experimental.pallas import tpu_sc as plsc

D_MODEL = 1024
DEPTH = 4
FOURIER_WIDTH = 256
N_FGROUPS = 4
FGROUP_DIM = 64
HEAD_DIM = 64
N_Q_HEADS = 12
N_KV_HEADS = 4
Q_PER_KV = 3
ATTN_WIDTH = N_Q_HEADS * HEAD_DIM
KV_WIDTH = N_KV_HEADS * HEAD_DIM
IN_WIDTH = FOURIER_WIDTH + ATTN_WIDTH + 2 * KV_WIDTH
WINDOW = 128
ROPE_THETA = 500000.0
ROT_DIM = 16
N_EXPERTS = 32
TOP_K = 4
D_FF = 512
EXPERT_BLOCK = 1024
EXPERT_SUB = 512
SWIGLU_LIMIT = 7.0
SWIGLU_ALPHA = 1.702
NORM_EPS = 1e-5

LANES = 128
HALF = D_MODEL // 2
TOKEN_TILE = 512
SUB_TILE = 256
Q_TILE = 256
KEY_SPAN = Q_TILE + 2 * WINDOW
SC_CHUNK = 128
SC_WORKERS = 32
SC_LANES = 16
COMBINE_CHUNK = 16
VMEM_LIMIT = 56 * 1024 * 1024
NEG_BIG = -1e30
LOG2E = 1.4426950408889634
Q_SCALE = HEAD_DIM ** -0.5 * LOG2E
HI_MASK = -65536
ROUTE_ROWS = 16

_f32 = jnp.float32
_bf16 = jnp.bfloat16


def _pack_halves(a, b):
    ua = lax.bitcast_convert_type(a.astype(_bf16).astype(_f32), jnp.int32)
    ub = lax.bitcast_convert_type(b.astype(_bf16).astype(_f32), jnp.int32)
    return ua | lax.shift_right_logical(ub, 16)


def _unpack_halves(p):
    a = lax.bitcast_convert_type(p & HI_MASK, _f32)
    b = lax.bitcast_convert_type(lax.shift_left(p, 16), _f32)
    return a, b


def _rms(x, g):
    ms = jnp.mean(x * x, axis=-1, keepdims=True)
    return x * lax.rsqrt(ms + NORM_EPS) * g


def _mod_kernel(c_ref, w_ref, b_ref, o_ref):
    c = c_ref[...]
    s = c * (1.0 / (1.0 + jnp.exp(-c)))
    o_ref[0] = jnp.dot(s.astype(_bf16), w_ref[0].astype(_bf16), preferred_element_type=_f32) + b_ref[0]


def _modulation(c_all, ada_w, ada_b):
    nb = c_all.shape[0]
    ncol = ada_w.shape[2] // D_MODEL
    return pl.pallas_call(
        _mod_kernel,
        out_shape=jax.ShapeDtypeStruct((DEPTH, nb, 6 * D_MODEL), _f32),
        grid=(DEPTH, ncol),
        in_specs=[
            pl.BlockSpec((nb, D_MODEL), lambda l, j: (0, 0)),
            pl.BlockSpec((1, D_MODEL, D_MODEL), lambda l, j: (l, 0, j)),
            pl.BlockSpec((1, 1, D_MODEL), lambda l, j: (l, 0, j)),
        ],
        out_specs=pl.BlockSpec((1, nb, D_MODEL), lambda l, j: (l, 0, j)),
        name="modulation",
    )(c_all, ada_w, ada_b.reshape(DEPTH, 1, 6 * D_MODEL))


def _inproj_matmul(x, sh_ref, sc_ref, ng_ref, w_ref, b_ref, *_):
    h = _rms(x, ng_ref[...]) * (1.0 + sc_ref[0]) + sh_ref[0]
    return jnp.dot(h.astype(_bf16), w_ref[...], preferred_element_type=_f32) + b_ref[...]


def _inproj_finish(rows, z, sh_ref, sc_ref, ng_ref, w_ref, b_ref, ab_ref, rc_ref, rs1_ref, rs2_ref,
                   y_ref, q_ref, k_ref, v_ref):
    f = z[:, :FOURIER_WIDTH]
    y_ref[rows, :] = jnp.dot(f.astype(_bf16), ab_ref[...], preferred_element_type=_f32).astype(_bf16)
    rc = rc_ref[rows, :]
    rs1 = rs1_ref[rows, :]
    rs2 = rs2_ref[rows, :]

    def rope(t):
        return t * rc + pltpu.roll(t, LANES - ROT_DIM // 2, 1) * rs1 + pltpu.roll(t, ROT_DIM // 2, 1) * rs2

    q0 = FOURIER_WIDTH
    for c in range(ATTN_WIDTH // LANES):
        t = z[:, q0 + c * LANES:q0 + (c + 1) * LANES]
        q_ref[rows, c * LANES:(c + 1) * LANES] = (rope(t) * Q_SCALE).astype(_bf16)
    k0 = q0 + ATTN_WIDTH
    for c in range(KV_WIDTH // LANES):
        t = z[:, k0 + c * LANES:k0 + (c + 1) * LANES]
        k_ref[rows, c * LANES:(c + 1) * LANES] = rope(t).astype(_bf16)
    v_ref[rows, :] = z[:, k0 + KV_WIDTH:].astype(_bf16)


def _sub_tiles(n_rows):
    return [pl.ds(r, SUB_TILE) for r in range(0, n_rows, SUB_TILE)]


def _inproj_kernel(x_ref, *refs):
    tiles = _sub_tiles(x_ref.shape[0])
    z = _inproj_matmul(x_ref[tiles[0], :], *refs)
    for n, rows in enumerate(tiles):
        z_next = _inproj_matmul(x_ref[tiles[n + 1], :], *refs) if n + 1 < len(tiles) else None
        _inproj_finish(rows, z, *refs)
        z = z_next


def _inproj(seq, x, sh1, sc1, norm_g, w_in, b_in, ab, rope_tabs):
    t_tokens = x.shape[0]
    tm = TOKEN_TILE
    tiles_per_seq = seq // tm
    row = lambda i: (i, 0)
    per_seq = lambda i: (i // tiles_per_seq, 0, 0)
    const = lambda i: (0, 0)
    pos = lambda i: (i % tiles_per_seq, 0)
    return pl.pallas_call(
        _inproj_kernel,
        out_shape=[jax.ShapeDtypeStruct((t_tokens, 2 * FOURIER_WIDTH), _bf16),
                   jax.ShapeDtypeStruct((t_tokens, ATTN_WIDTH), _bf16),
                   jax.ShapeDtypeStruct((t_tokens, KV_WIDTH), _bf16),
                   jax.ShapeDtypeStruct((t_tokens, KV_WIDTH), _bf16)],
        grid=(t_tokens // tm,),
        in_specs=[pl.BlockSpec((tm, D_MODEL), row),
                  pl.BlockSpec((1, 1, D_MODEL), per_seq), pl.BlockSpec((1, 1, D_MODEL), per_seq),
                  pl.BlockSpec((1, D_MODEL), const),
                  pl.BlockSpec((D_MODEL, IN_WIDTH), const), pl.BlockSpec((1, IN_WIDTH), const),
                  pl.BlockSpec((FOURIER_WIDTH, 2 * FOURIER_WIDTH), const),
                  pl.BlockSpec((tm, LANES), pos), pl.BlockSpec((tm, LANES), pos), pl.BlockSpec((tm, LANES), pos)],
        out_specs=[pl.BlockSpec((tm, 2 * FOURIER_WIDTH), row), pl.BlockSpec((tm, ATTN_WIDTH), row),
                   pl.BlockSpec((tm, KV_WIDTH), row), pl.BlockSpec((tm, KV_WIDTH), row)],
        compiler_params=pltpu.CompilerParams(dimension_semantics=("parallel",), vmem_limit_bytes=VMEM_LIMIT),
        name="inproj",
    )(x, sh1, sc1, norm_g, w_in, b_in, ab, *rope_tabs)


def _fourier_kernel(y_ref, ce_ref, se_ref, co_ref, so_ref, o_ref):
    half = y_ref.shape[0] // 2
    lo = y_ref[:half, :].astype(_f32)
    hi = y_ref[half:, :].astype(_f32)
    ye = (lo + hi).astype(_bf16)
    yo = (lo - hi).astype(_bf16)
    w = FOURIER_WIDTH
    even = (jnp.dot(ce_ref[...], ye[:, :w], preferred_element_type=_f32)
            + jnp.dot(se_ref[...], ye[:, w:], preferred_element_type=_f32))
    odd = (jnp.dot(co_ref[...], yo[:, :w], preferred_element_type=_f32)
           + jnp.dot(so_ref[...], yo[:, w:], preferred_element_type=_f32))
    for c in range(FOURIER_WIDTH // LANES):
        o_ref[c, pl.ds(0, half, stride=2), :] = even[:, c * LANES:(c + 1) * LANES]
        o_ref[c, pl.ds(1, half, stride=2), :] = odd[:, c * LANES:(c + 1) * LANES]


def _fourier(seq, y, dft):
    t_tokens = y.shape[0]
    half = seq // 2
    const = lambda b: (0, 0)
    return pl.pallas_call(
        _fourier_kernel,
        out_shape=jax.ShapeDtypeStruct((FOURIER_WIDTH // LANES, t_tokens, LANES), _f32),
        grid=(t_tokens // seq,),
        in_specs=[pl.BlockSpec((seq, 2 * FOURIER_WIDTH), lambda b: (b, 0))]
        + [pl.BlockSpec((half, half), const)] * 4,
        out_specs=pl.BlockSpec((FOURIER_WIDTH // LANES, seq, LANES), lambda b: (0, b, 0)),
        compiler_params=pltpu.CompilerParams(dimension_semantics=("parallel",), vmem_limit_bytes=VMEM_LIMIT),
        name="fourier",
    )(y, *dft)


def _dft_tables(seq):
    half = seq // 2
    j = jnp.arange(half, dtype=jnp.int32)[:, None]
    k = jnp.arange(half, dtype=jnp.int32)[None, :]
    ang_e = ((2 * j * k) % seq).astype(_f32) * (2.0 * jnp.pi / seq)
    ang_o = (((2 * j + 1) * k) % seq).astype(_f32) * (2.0 * jnp.pi / seq)
    return (jnp.cos(ang_e).astype(_bf16), jnp.sin(ang_e).astype(_bf16),
            jnp.cos(ang_o).astype(_bf16), jnp.sin(ang_o).astype(_bf16))


def _channel_dft_fold(w_fmix, seq):
    c = jnp.arange(FGROUP_DIM, dtype=jnp.int32)
    ang = ((c[:, None] * c[None, :]) % FGROUP_DIM).astype(_f32) * (2.0 * jnp.pi / FGROUP_DIM)
    scale = (seq * FGROUP_DIM) ** -0.5
    hp = lax.Precision.HIGHEST
    a = jnp.einsum('cm,gmd->gcd', jnp.cos(ang) * scale, w_fmix, precision=hp)
    b = jnp.einsum('cm,gmd->gcd', -jnp.sin(ang) * scale, w_fmix, precision=hp)
    eye = jnp.eye(N_FGROUPS, dtype=_f32)
    bd = lambda m: jnp.einsum('gcd,gh->gchd', m, eye).reshape(FOURIER_WIDTH, FOURIER_WIDTH)
    return jnp.concatenate([bd(a), bd(b)], axis=1).astype(_bf16)


def _rope_tables(seq):
    half = ROT_DIM // 2
    inv_freq = jnp.power(ROPE_THETA, -jnp.arange(0, ROT_DIM, 2, dtype=_f32) / ROT_DIM)
    ang = jnp.arange(seq, dtype=_f32)[:, None] * inv_freq[None, :]
    cos, sin = jnp.cos(ang), jnp.sin(ang)
    ones = jnp.ones((seq, HEAD_DIM - ROT_DIM), _f32)
    zeros = jnp.zeros((seq, HEAD_DIM - ROT_DIM), _f32)
    zh = jnp.zeros((seq, half), _f32)
    c1 = jnp.concatenate([cos, cos, ones], axis=1)
    s1 = jnp.concatenate([-sin, zh, zeros], axis=1)
    s2 = jnp.concatenate([zh, sin, zeros], axis=1)
    rep = LANES // HEAD_DIM
    return tuple(jnp.tile(t, (1, rep)) for t in (c1, s1, s2))


def _attn_kernel(sink_ref, q_ref, k_ref, v_ref, o_ref, kpad, vtb, ot):
    seq = k_ref.shape[0]
    nqb = seq // Q_TILE
    nkb = seq // WINDOW
    qb = pl.program_id(1)

    @pl.when(qb == 0)
    def _():
        zk = jnp.zeros((WINDOW, HEAD_DIM), _bf16)
        for h in range(N_KV_HEADS):
            kpad[h, :WINDOW, :] = zk
            kpad[h, WINDOW + seq:, :] = zk
            kpad[h, WINDOW:WINDOW + seq, :] = k_ref[:, h * HEAD_DIM:(h + 1) * HEAD_DIM]
        zv = jnp.zeros((KV_WIDTH, WINDOW), _bf16)
        vtb[0] = zv
        vtb[nkb + 1] = zv
        for j in range(nkb):
            vtb[j + 1] = v_ref[j * WINDOW:(j + 1) * WINDOW, :].astype(_f32).T.astype(_bf16)

    r0 = pl.multiple_of(qb * Q_TILE, Q_TILE)
    kb0 = qb * (Q_TILE // WINDOW)
    ka = lax.broadcasted_iota(jnp.int32, (WINDOW, WINDOW), 0)
    qc = lax.broadcasted_iota(jnp.int32, (WINDOW, WINDOW), 1)
    tri_ge = jnp.where(ka >= qc, 0.0, NEG_BIG).astype(_f32)
    tri_le = jnp.where(ka <= qc, 0.0, NEG_BIG).astype(_f32)
    bias_first = jnp.where(qb == 0, NEG_BIG, tri_ge)
    bias_last = jnp.where(qb == nqb - 1, NEG_BIG, tri_le)
    zero_blk = jnp.zeros((WINDOW, WINDOW), _bf16)
    vwin = jnp.concatenate([vtb[kb0 + j] for j in range(KEY_SPAN // WINDOW)], axis=1)

    def softmax_col(blocks, sink):
        m = blocks[0]
        for b in blocks[1:]:
            m = jnp.maximum(m, b)
        m = jnp.maximum(jnp.max(m, axis=0, keepdims=True), sink)
        ps = [jnp.exp2(b - m) for b in blocks]
        tot = ps[0]
        for p in ps[1:]:
            tot = tot + p
        return ps, jnp.sum(tot, axis=0, keepdims=True) + jnp.exp2(sink - m)

    w = WINDOW

    def scores(h):
        kh = kpad[h, pl.ds(r0, KEY_SPAN), :]
        q3 = jnp.concatenate([q_ref[:, g * HEAD_DIM:(g + 1) * HEAD_DIM]
                              for g in range(Q_PER_KV * h, Q_PER_KV * (h + 1))], axis=0)
        return lax.dot_general(kh, q3, (((1,), (1,)), ((), ())), preferred_element_type=_f32)

    s = scores(0)
    for h in range(N_KV_HEADS):
        s_next = scores(h + 1) if h + 1 < N_KV_HEADS else None
        cols, dens = [], []
        for i in range(Q_PER_KV):
            sink = sink_ref[Q_PER_KV * h + i] * LOG2E
            c0 = 2 * i * w
            c1 = c0 + w
            p0, l0 = softmax_col([s[0:w, c0:c1] + bias_first, s[w:2 * w, c0:c1], s[2 * w:3 * w, c0:c1] + tri_le],
                                 sink)
            p1, l1 = softmax_col([s[w:2 * w, c1:c1 + w] + tri_ge, s[2 * w:3 * w, c1:c1 + w],
                                  s[3 * w:, c1:c1 + w] + bias_last], sink)
            cols.append(jnp.concatenate([p.astype(_bf16) for p in p0] + [zero_blk], axis=0))
            cols.append(jnp.concatenate([zero_blk] + [p.astype(_bf16) for p in p1], axis=0))
            dens += [l0, l1]
        pt = jnp.concatenate(cols, axis=1)
        o_t = jnp.dot(vwin[h * HEAD_DIM:(h + 1) * HEAD_DIM, :], pt, preferred_element_type=_f32)
        o_t = o_t / jnp.concatenate(dens, axis=1)
        for i in range(Q_PER_KV):
            g = Q_PER_KV * h + i
            ot[g * HEAD_DIM:(g + 1) * HEAD_DIM, :] = o_t[:, i * Q_TILE:(i + 1) * Q_TILE]
        s = s_next
    for c in range(ATTN_WIDTH // LANES):
        o_ref[:, c * LANES:(c + 1) * LANES] = ot[c * LANES:(c + 1) * LANES, :].T.astype(_bf16)


def _attention(seq, q, k, v, sinks):
    t_tokens = q.shape[0]
    nqb = seq // Q_TILE
    return pl.pallas_call(
        _attn_kernel,
        out_shape=jax.ShapeDtypeStruct((t_tokens, ATTN_WIDTH), _bf16),
        grid=(t_tokens // seq, nqb),
        in_specs=[pl.BlockSpec(memory_space=pltpu.SMEM),
                  pl.BlockSpec((Q_TILE, ATTN_WIDTH), lambda b, j: (b * nqb + j, 0)),
                  pl.BlockSpec((seq, KV_WIDTH), lambda b, j: (b, 0)),
                  pl.BlockSpec((seq, KV_WIDTH), lambda b, j: (b, 0))],
        out_specs=pl.BlockSpec((Q_TILE, ATTN_WIDTH), lambda b, j: (b * nqb + j, 0)),
        scratch_shapes=[pltpu.VMEM((N_KV_HEADS, seq + 2 * WINDOW, HEAD_DIM), _bf16),
                        pltpu.VMEM((seq // WINDOW + 2, KV_WIDTH, WINDOW), _bf16),
                        pltpu.VMEM((ATTN_WIDTH, Q_TILE), _f32)],
        compiler_params=pltpu.CompilerParams(dimension_semantics=("parallel", "arbitrary"),
                                             vmem_limit_bytes=VMEM_LIMIT),
        name="attention",
    )(sinks, q, k, v)


def _outproj_kernel(x_ref, mf_ref, at_ref, g1_ref, sh_ref, sc_ref, ng_ref, wof_ref, woa_ref, bo_ref,
                    wr_ref, br_ref, tri_ref, x1_ref, hp_ref, rt_ref, cnt_ref, carry_ref):
    i = pl.program_id(0)

    @pl.when(i == 0)
    def _():
        carry_ref[...] = jnp.zeros_like(carry_ref)

    def project(rows):
        mf = jnp.concatenate([mf_ref[c, rows, :] for c in range(FOURIER_WIDTH // LANES)], axis=1)
        return (jnp.dot(mf.astype(_bf16), wof_ref[...], preferred_element_type=_f32)
                + jnp.dot(at_ref[rows, :], woa_ref[...], preferred_element_type=_f32) + bo_ref[...])

    tiles = _sub_tiles(x_ref.shape[0])
    mix = project(tiles[0])
    logit_cols = []
    for n, rows in enumerate(tiles):
        mix_next = project(tiles[n + 1]) if n + 1 < len(tiles) else None
        x1 = x_ref[rows, :] + g1_ref[0] * mix
        x1_ref[rows, :] = x1
        h = _rms(x1, ng_ref[...]) * (1.0 + sc_ref[0]) + sh_ref[0]
        hp_ref[rows, :] = _pack_halves(h[:, :HALF], h[:, HALF:])
        logit_cols.append(lax.dot_general(wr_ref[...], h.astype(_bf16), (((1,), (1,)), ((), ())),
                                          preferred_element_type=_f32))
        mix = mix_next
    logits = jnp.concatenate(logit_cols, axis=1) + br_ref[...]
    tm = logits.shape[1]
    erow = lax.broadcasted_iota(jnp.int32, logits.shape, 0)
    work = logits
    hots, vals, idxs = [], [], []
    for _k in range(TOP_K):
        mx = jnp.max(work, axis=0, keepdims=True)
        ix = jnp.min(jnp.where(work == mx, erow, N_EXPERTS), axis=0, keepdims=True)
        hot = erow == ix
        work = jnp.where(hot, -jnp.inf, work)
        hots.append(hot)
        vals.append(mx)
        idxs.append(ix)
    es = [jnp.exp(v - vals[0]) for v in vals]
    den = es[0] + es[1] + es[2] + es[3]
    member = jnp.zeros(logits.shape, _f32)
    for hot in hots:
        member = member + hot.astype(_f32)
    carry = carry_ref[...]
    before = (jnp.dot(member.astype(_bf16), tri_ref[...], preferred_element_type=_f32)
              + jnp.concatenate([carry] * (tm // LANES), axis=1))
    r16 = lax.broadcasted_iota(jnp.int32, (ROUTE_ROWS, tm), 0)
    slab = jnp.zeros((ROUTE_ROWS, tm), jnp.int32)
    for k in range(TOP_K):
        rank = jnp.sum(jnp.where(hots[k], before, 0.0), axis=0, keepdims=True).astype(jnp.int32)
        slab = jnp.where(r16 == k, idxs[k], slab)
        slab = jnp.where(r16 == TOP_K + k, rank, slab)
        slab = jnp.where(r16 == 2 * TOP_K + k, lax.bitcast_convert_type(es[k] / den, jnp.int32), slab)
    rt_ref[...] = slab
    carry = carry + jnp.broadcast_to(jnp.sum(member, axis=1, keepdims=True), carry.shape)
    carry_ref[...] = carry
    cnt_ref[...] = carry.astype(jnp.int32)


def _outproj(seq, x, mf, attn, g1, sh2, sc2, norm_g, wo_f, wo_a, b_o, w_r, b_r, tri):
    t_tokens = x.shape[0]
    tm = TOKEN_TILE
    tiles_per_seq = seq // tm
    row = lambda i: (i, 0)
    per_seq = lambda i: (i // tiles_per_seq, 0, 0)
    const = lambda i: (0, 0)
    return pl.pallas_call(
        _outproj_kernel,
        out_shape=[jax.ShapeDtypeStruct((t_tokens, D_MODEL), _f32),
                   jax.ShapeDtypeStruct((t_tokens, HALF), jnp.int32),
                   jax.ShapeDtypeStruct((ROUTE_ROWS, t_tokens), jnp.int32),
                   jax.ShapeDtypeStruct((N_EXPERTS, LANES), jnp.int32)],
        grid=(t_tokens // tm,),
        in_specs=[pl.BlockSpec((tm, D_MODEL), row),
                  pl.BlockSpec((FOURIER_WIDTH // LANES, tm, LANES), lambda i: (0, i, 0)),
                  pl.BlockSpec((tm, ATTN_WIDTH), row),
                  pl.BlockSpec((1, 1, D_MODEL), per_seq), pl.BlockSpec((1, 1, D_MODEL), per_seq),
                  pl.BlockSpec((1, 1, D_MODEL), per_seq),
                  pl.BlockSpec((1, D_MODEL), const),
                  pl.BlockSpec((FOURIER_WIDTH, D_MODEL), const), pl.BlockSpec((ATTN_WIDTH, D_MODEL), const),
                  pl.BlockSpec((1, D_MODEL), const),
                  pl.BlockSpec((N_EXPERTS, D_MODEL), const), pl.BlockSpec((N_EXPERTS, tm), const),
                  pl.BlockSpec((tm, tm), const)],
        out_specs=[pl.BlockSpec((tm, D_MODEL), row), pl.BlockSpec((tm, HALF), row),
                   pl.BlockSpec((ROUTE_ROWS, tm), lambda i: (0, i)),
                   pl.BlockSpec((N_EXPERTS, LANES), const)],
        scratch_shapes=[pltpu.VMEM((N_EXPERTS, LANES), _f32)],
        compiler_params=pltpu.CompilerParams(dimension_semantics=("arbitrary",), vmem_limit_bytes=VMEM_LIMIT),
        name="outproj",
    )(x, mf, attn, g1, sh2, sc2, norm_g, wo_f, wo_a, b_o, w_r, b_r, tri)


def _expert_kernel(blk_ref, bexp_ref, nact_ref, x_ref, wgu_ref, bgu_ref, wd_ref, bd_ref, y_ref, wgu_bf, wd_bf):
    i = pl.program_id(0)
    prev = bexp_ref[jnp.maximum(i - 1, 0)]

    @pl.when((i == 0) | (bexp_ref[i] != prev))
    def _():
        wgu_bf[...] = wgu_ref[...].astype(_bf16)
        wd_bf[...] = wd_ref[...].astype(_bf16)

    def gate_up(rows):
        xa, xb = _unpack_halves(x_ref[rows, :])
        return (jnp.dot(xa.astype(_bf16), wgu_bf[:HALF, :], preferred_element_type=_f32)
                + jnp.dot(xb.astype(_bf16), wgu_bf[HALF:, :], preferred_element_type=_f32) + bgu_ref[...])

    def activate(gu):
        g = jnp.minimum(gu[:, :D_FF], SWIGLU_LIMIT)
        u = jnp.clip(gu[:, D_FF:], -SWIGLU_LIMIT, SWIGLU_LIMIT)
        return ((u + 1.0) * (g * (1.0 / (1.0 + jnp.exp(-SWIGLU_ALPHA * g))))).astype(_bf16)

    def down(rows, act):
        y = jnp.dot(act, wd_bf[...], preferred_element_type=_f32) + bd_ref[...]
        y_ref[rows, :] = _pack_halves(y[:, :HALF], y[:, HALF:])

    @pl.when(i < nact_ref[0])
    def _():
        tiles = [pl.ds(r, EXPERT_SUB) for r in range(0, EXPERT_BLOCK, EXPERT_SUB)]
        gu = gate_up(tiles[0])
        for n, rows in enumerate(tiles):
            gu_next = gate_up(tiles[n + 1]) if n + 1 < len(tiles) else None
            down(rows, activate(gu))
            gu = gu_next


def _experts(layer, xs, blk, bexp, nact, w_gu, b_gu, w_down, b_down):
    p_rows = xs.shape[0]
    nblk = p_rows // EXPERT_BLOCK
    grid_spec = pltpu.PrefetchScalarGridSpec(
        num_scalar_prefetch=3,
        grid=(nblk,),
        in_specs=[pl.BlockSpec((EXPERT_BLOCK, HALF), lambda i, blk, be, na: (blk[i], 0)),
                  pl.BlockSpec((None, None, D_MODEL, 2 * D_FF), lambda i, blk, be, na: (layer, be[i], 0, 0)),
                  pl.BlockSpec((None, 1, 2 * D_FF), lambda i, blk, be, na: (be[i], 0, 0)),
                  pl.BlockSpec((None, None, D_FF, D_MODEL), lambda i, blk, be, na: (layer, be[i], 0, 0)),
                  pl.BlockSpec((None, 1, D_MODEL), lambda i, blk, be, na: (be[i], 0, 0))],
        out_specs=pl.BlockSpec((EXPERT_BLOCK, HALF), lambda i, blk, be, na: (blk[i], 0)),
        scratch_shapes=[pltpu.VMEM((D_MODEL, 2 * D_FF), _bf16), pltpu.VMEM((D_FF, D_MODEL), _bf16)])
    return pl.pallas_call(
        _expert_kernel,
        out_shape=jax.ShapeDtypeStruct((p_rows, HALF), jnp.int32),
        grid_spec=grid_spec,
        compiler_params=pltpu.CompilerParams(dimension_semantics=("arbitrary",), vmem_limit_bytes=VMEM_LIMIT),
        name="experts",
    )(blk, bexp, nact, xs, w_gu, b_gu.reshape(N_EXPERTS, 1, 2 * D_FF), w_down,
      b_down.reshape(N_EXPERTS, 1, D_MODEL))


def _sc_mesh():
    return plsc.VectorSubcoreMesh(core_axis_name="c", subcore_axis_name="s")


def _sc_worker():
    return lax.axis_index("s") * 2 + lax.axis_index("c")


def _dispatch(hp, dest_t, p_rows):
    n_chunks = dest_t.shape[0]
    per_worker = n_chunks // SC_WORKERS

    @functools.partial(
        pl.kernel, mesh=_sc_mesh(),
        out_type=jax.ShapeDtypeStruct((p_rows, HALF), jnp.int32),
        scratch_types=[pltpu.VMEM((TOP_K, SC_CHUNK), jnp.int32),
                       pltpu.VMEM((SC_CHUNK, HALF), jnp.int32),
                       pltpu.SemaphoreType.DMA],
        name="dispatch")
    def run(hp_hbm, dest_hbm, xs_hbm, idx_v, rows_v, sem):
        base = _sc_worker() * per_worker

        @pl.loop(0, per_worker)
        def _(j):
            chunk = base + j
            pltpu.sync_copy(dest_hbm.at[chunk], idx_v)
            pltpu.sync_copy(hp_hbm.at[pl.ds(chunk * SC_CHUNK, SC_CHUNK)], rows_v)
            copies = [pltpu.async_copy(rows_v, xs_hbm.at[idx_v.at[k]], sem) for k in range(TOP_K)]
            for cp in copies:
                cp.wait()

    return run(hp, dest_t)


def _combine_rows(y, dest_c, w_c, x1, g2, seq):
    t_tokens = x1.shape[0]
    cc = COMBINE_CHUNK
    n_chunks = t_tokens // cc
    per_worker = n_chunks // SC_WORKERS
    assert per_worker % 2 == 0 and seq % (per_worker * cc) == 0
    n_vec = HALF // SC_LANES
    row_buf = pltpu.VMEM((TOP_K, cc, HALF), jnp.int32)
    x_buf = pltpu.VMEM((cc, D_MODEL), _f32)

    @functools.partial(
        pl.kernel, mesh=_sc_mesh(),
        out_type=jax.ShapeDtypeStruct((t_tokens, D_MODEL), _f32),
        scratch_types=[pltpu.VMEM((per_worker * TOP_K * cc,), jnp.int32),
                       pltpu.VMEM((per_worker * TOP_K * cc,), _f32),
                       row_buf, row_buf, x_buf, x_buf,
                       pltpu.VMEM((D_MODEL,), _f32),
                       pltpu.SemaphoreType.DMA, pltpu.SemaphoreType.DMA,
                       pltpu.SemaphoreType.DMA, pltpu.SemaphoreType.DMA],
        compiler_params=pltpu.CompilerParams(needs_layout_passes=False),
        name="combine_rows")
    def run(y_hbm, dest_hbm, w_hbm, x1_hbm, g2_hbm, x2_hbm,
            idx_v, w_v, rows0, rows1, xb0, xb1, g_v, in0, in1, out0, out1):
        base = _sc_worker() * per_worker
        bufs = ((rows0, xb0, in0, out0), (rows1, xb1, in1, out1))
        pltpu.sync_copy(dest_hbm.at[pl.ds(base * (TOP_K * cc), per_worker * TOP_K * cc)], idx_v)
        pltpu.sync_copy(w_hbm.at[pl.ds(base * (TOP_K * cc), per_worker * TOP_K * cc)], w_v)
        pltpu.sync_copy(g2_hbm.at[(base * cc) // seq], g_v)

        def loads(c, b):
            rows, xb, isem, _ = bufs[b]
            cps = [pltpu.make_async_copy(y_hbm.at[idx_v.at[pl.ds((c * TOP_K + k) * cc, cc)]], rows.at[k], isem)
                   for k in range(TOP_K)]
            cps.append(pltpu.make_async_copy(x1_hbm.at[pl.ds((base + c) * cc, cc)], xb, isem))
            return cps

        def store(c, b):
            _, xb, _, osem = bufs[b]
            return pltpu.make_async_copy(xb, x2_hbm.at[pl.ds((base + c) * cc, cc)], osem)

        def combine(c, b):
            rows, xb, _, _ = bufs[b]

            @pl.loop(0, cc)
            def _(t):
                ws = [plsc.load_gather(w_v, [jnp.full((SC_LANES,), (c * TOP_K + k) * cc + t, jnp.int32)])
                      for k in range(TOP_K)]

                @plsc.parallel_loop(0, n_vec)
                def _(j):
                    lo = pl.ds(j * SC_LANES, SC_LANES)
                    hi = pl.ds(HALF + j * SC_LANES, SC_LANES)
                    acc_a = None
                    acc_b = None
                    for k in range(TOP_K):
                        p = rows[k, t, lo]
                        a = lax.bitcast_convert_type(p & HI_MASK, _f32) * ws[k]
                        b_ = lax.bitcast_convert_type(p << 16, _f32) * ws[k]
                        acc_a = a if acc_a is None else acc_a + a
                        acc_b = b_ if acc_b is None else acc_b + b_
                    xb[t, lo] = xb[t, lo] + g_v[lo] * acc_a
                    xb[t, hi] = xb[t, hi] + g_v[hi] * acc_b

        for cp in loads(0, 0):
            cp.start()

        @pl.loop(0, per_worker // 2)
        def _(c2):
            for b in range(2):
                c = 2 * c2 + b

                @pl.when(c >= 1)
                def _():
                    store(c - 1, 1 - b).wait()

                @pl.when(c + 1 < per_worker)
                def _():
                    for cp in loads(c + 1, 1 - b):
                        cp.start()

                for cp in loads(c, b):
                    cp.wait()
                combine(c, b)
                store(c, b).start()

        store(per_worker - 1, 1).wait()

    return run(y, dest_c, w_c, x1, g2)


def _final_kernel(x_ref, ng_ref, o_ref):
    o_ref[...] = _rms(x_ref[...], ng_ref[...])


def _final(x, final_g):
    t_tokens = x.shape[0]
    tm = 2 * TOKEN_TILE
    row = lambda i: (i, 0)
    return pl.pallas_call(
        _final_kernel,
        out_shape=jax.ShapeDtypeStruct((t_tokens, D_MODEL), _f32),
        grid=(t_tokens // tm,),
        in_specs=[pl.BlockSpec((tm, D_MODEL), row), pl.BlockSpec((1, D_MODEL), lambda i: (0, 0))],
        out_specs=pl.BlockSpec((tm, D_MODEL), row),
        compiler_params=pltpu.CompilerParams(dimension_semantics=("parallel",), vmem_limit_bytes=VMEM_LIMIT),
        name="final",
    )(x, final_g)


def _plan(route, counts, nblk):
    cnt = counts[:, 0]
    padded = ((cnt + EXPERT_BLOCK - 1) // EXPERT_BLOCK) * EXPERT_BLOCK
    pad_end = jnp.cumsum(padded)
    pad_start = pad_end - padded
    expert = jnp.arange(N_EXPERTS, dtype=jnp.int32)[:, None, None]
    start_of = jnp.sum(jnp.where(route[None, :TOP_K] == expert, pad_start[:, None, None], 0), axis=0)
    dest = start_of + route[TOP_K:2 * TOP_K]
    nact = jnp.maximum(pad_end[-1] // EXPERT_BLOCK, 1)
    blk = jnp.minimum(jnp.arange(nblk, dtype=jnp.int32), nact - 1)
    bexp = jnp.minimum(jnp.sum(pad_end[None, :] <= (blk * EXPERT_BLOCK)[:, None], axis=1), N_EXPERTS - 1)
    t_tokens = route.shape[1]
    dest_t = dest.reshape(TOP_K, t_tokens // SC_CHUNK, SC_CHUNK).transpose(1, 0, 2)
    by_chunk = lambda a: a.reshape(TOP_K, t_tokens // COMBINE_CHUNK, COMBINE_CHUNK).transpose(1, 0, 2)
    dest_c = by_chunk(dest).reshape(-1)
    w_c = by_chunk(lax.bitcast_convert_type(route[2 * TOP_K:3 * TOP_K], _f32)).reshape(-1)
    return dest_t, dest_c, w_c, blk.astype(jnp.int32), bexp.astype(jnp.int32), nact.reshape(1).astype(jnp.int32)


def _trunk(x_all, c_all, norm1_g, ada_w, ada_b, w_in, b_in, w_fmix, sinks, w_o, b_o,
           norm2_g, w_router, b_router, w_gu, b_gu, w_down, b_down, final_g):
    nb, seq, _ = x_all.shape
    t_tokens = nb * seq
    assert seq % TOKEN_TILE == 0 and seq % Q_TILE == 0 and seq >= KEY_SPAN
    assert t_tokens % (SC_CHUNK * SC_WORKERS) == 0 and seq % COMBINE_CHUNK == 0
    nblk = (t_tokens * TOP_K) // EXPERT_BLOCK + N_EXPERTS
    p_rows = nblk * EXPERT_BLOCK

    mod = _modulation(c_all, ada_w, ada_b)
    mod = mod.reshape(DEPTH, nb, 6, 1, D_MODEL)
    rope_tabs = _rope_tables(seq)
    dft = _dft_tables(seq)
    tri = jnp.triu(jnp.ones((TOKEN_TILE, TOKEN_TILE), _f32), 1).astype(_bf16)

    x = x_all.reshape(t_tokens, D_MODEL)
    for l in range(DEPTH):
        sh1, sc1, g1, sh2, sc2, g2 = (mod[l, :, j] for j in range(6))
        ab = _channel_dft_fold(w_fmix[l], seq)
        y, q, k, v = _inproj(seq, x, sh1, sc1, norm1_g[l].reshape(1, D_MODEL), w_in[l].astype(_bf16),
                             b_in[l].reshape(1, IN_WIDTH), ab, rope_tabs)
        mf = _fourier(seq, y, dft)
        attn = _attention(seq, q, k, v, sinks[l])
        w_o_bf = w_o[l].astype(_bf16)
        w_r = w_router[l].T.astype(_bf16)
        b_r = jnp.broadcast_to(b_router[l][:, None], (N_EXPERTS, TOKEN_TILE))
        x1, hp, route, counts = _outproj(seq, x, mf, attn, g1, sh2, sc2, norm2_g[l].reshape(1, D_MODEL),
                                         w_o_bf[:FOURIER_WIDTH], w_o_bf[FOURIER_WIDTH:],
                                         b_o[l].reshape(1, D_MODEL), w_r, b_r, tri)
        dest_t, dest_c, w_c, blk, bexp, nact = _plan(route, counts, nblk)
        xs = _dispatch(hp, dest_t, p_rows)
        ys = _experts(l, xs, blk, bexp, nact, w_gu, b_gu[l], w_down, b_down[l])
        x = _combine_rows(ys, dest_c, w_c, x1, g2.reshape(nb, D_MODEL), seq)
    out = _final(x, final_g.reshape(1, D_MODEL))
    return out.reshape(nb, seq, D_MODEL)


def kernel(x_prompt, x_sample, c_prompt, c_sample, norm1_g, ada_w, ada_b, w_in, b_in, w_fmix, sinks, w_o, b_o,
           norm2_g, w_router, b_router, w_gu, b_gu, w_down, b_down, final_g):
    ws = (norm1_g, ada_w, ada_b, w_in, b_in, w_fmix, sinks, w_o, b_o,
          norm2_g, w_router, b_router, w_gu, b_gu, w_down, b_down, final_g)
    return _trunk(x_prompt, c_prompt, *ws), _trunk(x_sample, c_sample, *ws)
```

```python
import functools

import jax
import jax.numpy as jnp
from jax import lax
from jax.experimental import pallas as pl
from jax.experimental.pallas import tpu as pltpu
from jax.experimental.pallas import tpu_sc as plsc

D_MODEL = 1024
DEPTH = 4
FOURIER_WIDTH = 256
N_FGROUPS = 4
FGROUP_DIM = 64
HEAD_DIM = 64
N_Q_HEADS = 12
N_KV_HEADS = 4
Q_PER_KV = 3
ATTN_WIDTH = N_Q_HEADS * HEAD_DIM
KV_WIDTH = N_KV_HEADS * HEAD_DIM
IN_WIDTH = FOURIER_WIDTH + ATTN_WIDTH + 2 * KV_WIDTH
WINDOW = 128
ROPE_THETA = 500000.0
ROT_DIM = 16
N_EXPERTS = 32
TOP_K = 4
D_FF = 512
EXPERT_BLOCK = 1024
EXPERT_SUB = 512
SWIGLU_LIMIT = 7.0
SWIGLU_ALPHA = 1.702
NORM_EPS = 1e-5

LANES = 128
HALF = D_MODEL // 2
TOKEN_TILE = 512
SUB_TILE = 256
Q_TILE = 256
ATTN_TILES = 4
KEY_SPAN = Q_TILE + 2 * WINDOW
SC_CHUNK = 128
SC_WORKERS = 32
SC_LANES = 16
COMBINE_CHUNK = 16
VMEM_LIMIT = 56 * 1024 * 1024
NEG_BIG = -1e30
LOG2E = 1.4426950408889634
Q_SCALE = HEAD_DIM ** -0.5 * LOG2E
HI_MASK = -65536
ROUTE_ROWS = 16

_f32 = jnp.float32
_bf16 = jnp.bfloat16


def _pack_halves(a, b):
    ua = lax.bitcast_convert_type(a.astype(_bf16).astype(_f32), jnp.int32)
    ub = lax.bitcast_convert_type(b.astype(_bf16).astype(_f32), jnp.int32)
    return ua | lax.shift_right_logical(ub, 16)


def _unpack_halves(p):
    a = lax.bitcast_convert_type(p & HI_MASK, _f32)
    b = lax.bitcast_convert_type(lax.shift_left(p, 16), _f32)
    return a, b


def _rms(x, g):
    ms = jnp.mean(x * x, axis=-1, keepdims=True)
    return x * lax.rsqrt(ms + NORM_EPS) * g


def _mod_kernel(c_ref, w_ref, b_ref, o_ref):
    c = c_ref[...]
    s = c * (1.0 / (1.0 + jnp.exp(-c)))
    o_ref[0] = jnp.dot(s.astype(_bf16), w_ref[0].astype(_bf16), preferred_element_type=_f32) + b_ref[0]


def _modulation(c_all, ada_w, ada_b):
    nb = c_all.shape[0]
    ncol = ada_w.shape[2] // D_MODEL
    return pl.pallas_call(
        _mod_kernel,
        out_shape=jax.ShapeDtypeStruct((DEPTH, nb, 6 * D_MODEL), _f32),
        grid=(DEPTH, ncol),
        in_specs=[
            pl.BlockSpec((nb, D_MODEL), lambda l, j: (0, 0)),
            pl.BlockSpec((1, D_MODEL, D_MODEL), lambda l, j: (l, 0, j)),
            pl.BlockSpec((1, 1, D_MODEL), lambda l, j: (l, 0, j)),
        ],
        out_specs=pl.BlockSpec((1, nb, D_MODEL), lambda l, j: (l, 0, j)),
        name="modulation",
    )(c_all, ada_w, ada_b.reshape(DEPTH, 1, 6 * D_MODEL))


def _inproj_matmul(x, sh_ref, sc_ref, ng_ref, w_ref, b_ref, *_):
    h = _rms(x, ng_ref[...]) * (1.0 + sc_ref[0]) + sh_ref[0]
    return jnp.dot(h.astype(_bf16), w_ref[...], preferred_element_type=_f32) + b_ref[...]


def _inproj_finish(rows, z, sh_ref, sc_ref, ng_ref, w_ref, b_ref, ab_ref, rc_ref, rs1_ref, rs2_ref,
                   y_ref, q_ref, k_ref, v_ref):
    f = z[:, :FOURIER_WIDTH]
    y_ref[rows, :] = jnp.dot(f.astype(_bf16), ab_ref[...], preferred_element_type=_f32).astype(_bf16)
    rc = rc_ref[rows, :]
    rs1 = rs1_ref[rows, :]
    rs2 = rs2_ref[rows, :]

    def rope(t):
        return t * rc + pltpu.roll(t, LANES - ROT_DIM // 2, 1) * rs1 + pltpu.roll(t, ROT_DIM // 2, 1) * rs2

    q0 = FOURIER_WIDTH
    for c in range(ATTN_WIDTH // LANES):
        t = z[:, q0 + c * LANES:q0 + (c + 1) * LANES]
        q_ref[rows, c * LANES:(c + 1) * LANES] = (rope(t) * Q_SCALE).astype(_bf16)
    k0 = q0 + ATTN_WIDTH
    for c in range(KV_WIDTH // LANES):
        t = z[:, k0 + c * LANES:k0 + (c + 1) * LANES]
        k_ref[rows, c * LANES:(c + 1) * LANES] = rope(t).astype(_bf16)
    v_ref[rows, :] = z[:, k0 + KV_WIDTH:].astype(_bf16)


def _sub_tiles(n_rows):
    return [pl.ds(r, SUB_TILE) for r in range(0, n_rows, SUB_TILE)]


def _inproj_kernel(x_ref, *refs):
    tiles = _sub_tiles(x_ref.shape[0])
    z = _inproj_matmul(x_ref[tiles[0], :], *refs)
    for n, rows in enumerate(tiles):
        z_next = _inproj_matmul(x_ref[tiles[n + 1], :], *refs) if n + 1 < len(tiles) else None
        _inproj_finish(rows, z, *refs)
        z = z_next


def _inproj(seq, x, sh1, sc1, norm_g, w_in, b_in, ab, rope_tabs):
    t_tokens = x.shape[0]
    tm = TOKEN_TILE
    tiles_per_seq = seq // tm
    row = lambda i: (i, 0)
    per_seq = lambda i: (i // tiles_per_seq, 0, 0)
    const = lambda i: (0, 0)
    pos = lambda i: (i % tiles_per_seq, 0)
    return pl.pallas_call(
        _inproj_kernel,
        out_shape=[jax.ShapeDtypeStruct((t_tokens, 2 * FOURIER_WIDTH), _bf16),
                   jax.ShapeDtypeStruct((t_tokens, ATTN_WIDTH), _bf16),
                   jax.ShapeDtypeStruct((t_tokens, KV_WIDTH), _bf16),
                   jax.ShapeDtypeStruct((t_tokens, KV_WIDTH), _bf16)],
        grid=(t_tokens // tm,),
        in_specs=[pl.BlockSpec((tm, D_MODEL), row),
                  pl.BlockSpec((1, 1, D_MODEL), per_seq), pl.BlockSpec((1, 1, D_MODEL), per_seq),
                  pl.BlockSpec((1, D_MODEL), const),
                  pl.BlockSpec((D_MODEL, IN_WIDTH), const), pl.BlockSpec((1, IN_WIDTH), const),
                  pl.BlockSpec((FOURIER_WIDTH, 2 * FOURIER_WIDTH), const),
                  pl.BlockSpec((tm, LANES), pos), pl.BlockSpec((tm, LANES), pos), pl.BlockSpec((tm, LANES), pos)],
        out_specs=[pl.BlockSpec((tm, 2 * FOURIER_WIDTH), row), pl.BlockSpec((tm, ATTN_WIDTH), row),
                   pl.BlockSpec((tm, KV_WIDTH), row), pl.BlockSpec((tm, KV_WIDTH), row)],
        compiler_params=pltpu.CompilerParams(dimension_semantics=("parallel",), vmem_limit_bytes=VMEM_LIMIT),
        name="inproj",
    )(x, sh1, sc1, norm_g, w_in, b_in, ab, *rope_tabs)


def _fourier_kernel(y_ref, ce_ref, se_ref, co_ref, so_ref, o_ref):
    half = y_ref.shape[0] // 2
    lo = y_ref[:half, :].astype(_f32)
    hi = y_ref[half:, :].astype(_f32)
    ye = (lo + hi).astype(_bf16)
    yo = (lo - hi).astype(_bf16)
    w = FOURIER_WIDTH
    even = (jnp.dot(ce_ref[...], ye[:, :w], preferred_element_type=_f32)
            + jnp.dot(se_ref[...], ye[:, w:], preferred_element_type=_f32))
    odd = (jnp.dot(co_ref[...], yo[:, :w], preferred_element_type=_f32)
           + jnp.dot(so_ref[...], yo[:, w:], preferred_element_type=_f32))
    for c in range(FOURIER_WIDTH // LANES):
        o_ref[c, pl.ds(0, half, stride=2), :] = even[:, c * LANES:(c + 1) * LANES]
        o_ref[c, pl.ds(1, half, stride=2), :] = odd[:, c * LANES:(c + 1) * LANES]


def _fourier(seq, y, dft):
    t_tokens = y.shape[0]
    half = seq // 2
    const = lambda b: (0, 0)
    return pl.pallas_call(
        _fourier_kernel,
        out_shape=jax.ShapeDtypeStruct((FOURIER_WIDTH // LANES, t_tokens, LANES), _f32),
        grid=(t_tokens // seq,),
        in_specs=[pl.BlockSpec((seq, 2 * FOURIER_WIDTH), lambda b: (b, 0))]
        + [pl.BlockSpec((half, half), const)] * 4,
        out_specs=pl.BlockSpec((FOURIER_WIDTH // LANES, seq, LANES), lambda b: (0, b, 0)),
        compiler_params=pltpu.CompilerParams(dimension_semantics=("parallel",), vmem_limit_bytes=VMEM_LIMIT),
        name="fourier",
    )(y, *dft)


def _dft_tables(seq):
    half = seq // 2
    j = jnp.arange(half, dtype=jnp.int32)[:, None]
    k = jnp.arange(half, dtype=jnp.int32)[None, :]
    ang_e = ((2 * j * k) % seq).astype(_f32) * (2.0 * jnp.pi / seq)
    ang_o = (((2 * j + 1) * k) % seq).astype(_f32) * (2.0 * jnp.pi / seq)
    return (jnp.cos(ang_e).astype(_bf16), jnp.sin(ang_e).astype(_bf16),
            jnp.cos(ang_o).astype(_bf16), jnp.sin(ang_o).astype(_bf16))


def _channel_dft_fold(w_fmix, seq):
    c = jnp.arange(FGROUP_DIM, dtype=jnp.int32)
    ang = ((c[:, None] * c[None, :]) % FGROUP_DIM).astype(_f32) * (2.0 * jnp.pi / FGROUP_DIM)
    scale = (seq * FGROUP_DIM) ** -0.5
    hp = lax.Precision.HIGHEST
    a = jnp.einsum('cm,gmd->gcd', jnp.cos(ang) * scale, w_fmix, precision=hp)
    b = jnp.einsum('cm,gmd->gcd', -jnp.sin(ang) * scale, w_fmix, precision=hp)
    eye = jnp.eye(N_FGROUPS, dtype=_f32)
    bd = lambda m: jnp.einsum('gcd,gh->gchd', m, eye).reshape(FOURIER_WIDTH, FOURIER_WIDTH)
    return jnp.concatenate([bd(a), bd(b)], axis=1).astype(_bf16)


def _rope_tables(seq):
    half = ROT_DIM // 2
    inv_freq = jnp.power(ROPE_THETA, -jnp.arange(0, ROT_DIM, 2, dtype=_f32) / ROT_DIM)
    ang = jnp.arange(seq, dtype=_f32)[:, None] * inv_freq[None, :]
    cos, sin = jnp.cos(ang), jnp.sin(ang)
    ones = jnp.ones((seq, HEAD_DIM - ROT_DIM), _f32)
    zeros = jnp.zeros((seq, HEAD_DIM - ROT_DIM), _f32)
    zh = jnp.zeros((seq, half), _f32)
    c1 = jnp.concatenate([cos, cos, ones], axis=1)
    s1 = jnp.concatenate([-sin, zh, zeros], axis=1)
    s2 = jnp.concatenate([zh, sin, zeros], axis=1)
    rep = LANES // HEAD_DIM
    return tuple(jnp.tile(t, (1, rep)) for t in (c1, s1, s2))


def _attn_kernel(sink_ref, q_ref, k_ref, v_ref, o_ref, kpad, vtb, ot):
    seq = k_ref.shape[0]
    nqb = seq // Q_TILE
    nkb = seq // WINDOW

    @pl.when(pl.program_id(1) == 0)
    def _():
        zk = jnp.zeros((WINDOW, HEAD_DIM), _bf16)
        for h in range(N_KV_HEADS):
            kpad[h, :WINDOW, :] = zk
            kpad[h, WINDOW + seq:, :] = zk
            kpad[h, WINDOW:WINDOW + seq, :] = k_ref[:, h * HEAD_DIM:(h + 1) * HEAD_DIM]
        zv = jnp.zeros((KV_WIDTH, WINDOW), _bf16)
        vtb[0] = zv
        vtb[nkb + 1] = zv
        for j in range(nkb):
            vtb[j + 1] = v_ref[j * WINDOW:(j + 1) * WINDOW, :].astype(_f32).T.astype(_bf16)

    ka = lax.broadcasted_iota(jnp.int32, (WINDOW, WINDOW), 0)
    qc = lax.broadcasted_iota(jnp.int32, (WINDOW, WINDOW), 1)
    tri_ge = jnp.where(ka >= qc, 0.0, NEG_BIG).astype(_f32)
    tri_le = jnp.where(ka <= qc, 0.0, NEG_BIG).astype(_f32)
    zero_blk = jnp.zeros((WINDOW, WINDOW), _bf16)
    n_tiles = q_ref.shape[0] // Q_TILE
    qbs = [pl.program_id(1) * n_tiles + t for t in range(n_tiles)]
    bias_first = [jnp.where(qb == 0, NEG_BIG, tri_ge) for qb in qbs]
    bias_last = [jnp.where(qb == nqb - 1, NEG_BIG, tri_le) for qb in qbs]

    def softmax_col(blocks, sink):
        m = blocks[0]
        for b in blocks[1:]:
            m = jnp.maximum(m, b)
        m = jnp.maximum(jnp.max(m, axis=0, keepdims=True), sink)
        ps = [jnp.exp2(b - m) for b in blocks]
        tot = ps[0]
        for p in ps[1:]:
            tot = tot + p
        return ps, jnp.sum(tot, axis=0, keepdims=True) + jnp.exp2(sink - m)

    w = WINDOW

    def scores(t, h):
        r0 = pl.multiple_of(qbs[t] * Q_TILE, Q_TILE)
        kh = kpad[h, pl.ds(r0, KEY_SPAN), :]
        q3 = jnp.concatenate([q_ref[t * Q_TILE:(t + 1) * Q_TILE, g * HEAD_DIM:(g + 1) * HEAD_DIM]
                              for g in range(Q_PER_KV * h, Q_PER_KV * (h + 1))], axis=0)
        return lax.dot_general(kh, q3, (((1,), (1,)), ((), ())), preferred_element_type=_f32)

    work = [(t, h) for t in range(n_tiles) for h in range(N_KV_HEADS)]
    s = scores(*work[0])
    for n, (t, h) in enumerate(work):
        s_next = scores(*work[n + 1]) if n + 1 < len(work) else None
        cols, dens = [], []
        for i in range(Q_PER_KV):
            sink = sink_ref[Q_PER_KV * h + i] * LOG2E
            c0 = 2 * i * w
            c1 = c0 + w
            p0, l0 = softmax_col([s[0:w, c0:c1] + bias_first[t], s[w:2 * w, c0:c1], s[2 * w:3 * w, c0:c1] + tri_le],
                                 sink)
            p1, l1 = softmax_col([s[w:2 * w, c1:c1 + w] + tri_ge, s[2 * w:3 * w, c1:c1 + w],
                                  s[3 * w:, c1:c1 + w] + bias_last[t]], sink)
            cols.append(jnp.concatenate([p.astype(_bf16) for p in p0] + [zero_blk], axis=0))
            cols.append(jnp.concatenate([zero_blk] + [p.astype(_bf16) for p in p1], axis=0))
            dens += [l0, l1]
        pt = jnp.concatenate(cols, axis=1)
        kb0 = qbs[t] * (Q_TILE // WINDOW)
        vth = jnp.concatenate([vtb[kb0 + j, h * HEAD_DIM:(h + 1) * HEAD_DIM, :]
                               for j in range(KEY_SPAN // WINDOW)], axis=1)
        o_t = jnp.dot(vth, pt, preferred_element_type=_f32) / jnp.concatenate(dens, axis=1)
        for i in range(Q_PER_KV):
            g = Q_PER_KV * h + i
            ot[t, g * HEAD_DIM:(g + 1) * HEAD_DIM, :] = o_t[:, i * Q_TILE:(i + 1) * Q_TILE]
        if h == N_KV_HEADS - 1:
            for c in range(ATTN_WIDTH // LANES):
                o_ref[t * Q_TILE:(t + 1) * Q_TILE, c * LANES:(c + 1) * LANES] = (
                    ot[t, c * LANES:(c + 1) * LANES, :].T.astype(_bf16))
        s = s_next


def _attention(seq, q, k, v, sinks):
    t_tokens = q.shape[0]
    rows = ATTN_TILES * Q_TILE
    steps = seq // rows
    return pl.pallas_call(
        _attn_kernel,
        out_shape=jax.ShapeDtypeStruct((t_tokens, ATTN_WIDTH), _bf16),
        grid=(t_tokens // seq, steps),
        in_specs=[pl.BlockSpec(memory_space=pltpu.SMEM),
                  pl.BlockSpec((rows, ATTN_WIDTH), lambda b, j: (b * steps + j, 0)),
                  pl.BlockSpec((seq, KV_WIDTH), lambda b, j: (b, 0)),
                  pl.BlockSpec((seq, KV_WIDTH), lambda b, j: (b, 0))],
        out_specs=pl.BlockSpec((rows, ATTN_WIDTH), lambda b, j: (b * steps + j, 0)),
        scratch_shapes=[pltpu.VMEM((N_KV_HEADS, seq + 2 * WINDOW, HEAD_DIM), _bf16),
                        pltpu.VMEM((seq // WINDOW + 2, KV_WIDTH, WINDOW), _bf16),
                        pltpu.VMEM((ATTN_TILES, ATTN_WIDTH, Q_TILE), _f32)],
        compiler_params=pltpu.CompilerParams(dimension_semantics=("parallel", "arbitrary"),
                                             vmem_limit_bytes=VMEM_LIMIT),
        name="attention",
    )(sinks, q, k, v)


def _outproj_kernel(x_ref, mf_ref, at_ref, g1_ref, sh_ref, sc_ref, ng_ref, wof_ref, woa_ref, bo_ref,
                    wr_ref, br_ref, tri_ref, x1_ref, hp_ref, rt_ref, cnt_ref, carry_ref):
    i = pl.program_id(0)

    @pl.when(i == 0)
    def _():
        carry_ref[...] = jnp.zeros_like(carry_ref)

    def project(rows):
        mf = jnp.concatenate([mf_ref[c, rows, :] for c in range(FOURIER_WIDTH // LANES)], axis=1)
        return (jnp.dot(mf.astype(_bf16), wof_ref[...], preferred_element_type=_f32)
                + jnp.dot(at_ref[rows, :], woa_ref[...], preferred_element_type=_f32) + bo_ref[...])

    tiles = _sub_tiles(x_ref.shape[0])
    mix = project(tiles[0])
    logit_cols = []
    for n, rows in enumerate(tiles):
        mix_next = project(tiles[n + 1]) if n + 1 < len(tiles) else None
        x1 = x_ref[rows, :] + g1_ref[0] * mix
        x1_ref[rows, :] = x1
        h = _rms(x1, ng_ref[...]) * (1.0 + sc_ref[0]) + sh_ref[0]
        hp_ref[rows, :] = _pack_halves(h[:, :HALF], h[:, HALF:])
        logit_cols.append(lax.dot_general(wr_ref[...], h.astype(_bf16), (((1,), (1,)), ((), ())),
                                          preferred_element_type=_f32))
        mix = mix_next
    logits = jnp.concatenate(logit_cols, axis=1) + br_ref[...]
    tm = logits.shape[1]
    erow = lax.broadcasted_iota(jnp.int32, logits.shape, 0)
    work = logits
    hots, vals, idxs = [], [], []
    for _k in range(TOP_K):
        mx = jnp.max(work, axis=0, keepdims=True)
        ix = jnp.min(jnp.where(work == mx, erow, N_EXPERTS), axis=0, keepdims=True)
        hot = erow == ix
        work = jnp.where(hot, -jnp.inf, work)
        hots.append(hot)
        vals.append(mx)
        idxs.append(ix)
    es = [jnp.exp(v - vals[0]) for v in vals]
    den = es[0] + es[1] + es[2] + es[3]
    member = jnp.zeros(logits.shape, _f32)
    for hot in hots:
        member = member + hot.astype(_f32)
    carry = carry_ref[...]
    before = (jnp.dot(member.astype(_bf16), tri_ref[...], preferred_element_type=_f32)
              + jnp.concatenate([carry] * (tm // LANES), axis=1))
    r16 = lax.broadcasted_iota(jnp.int32, (ROUTE_ROWS, tm), 0)
    slab = jnp.zeros((ROUTE_ROWS, tm), jnp.int32)
    for k in range(TOP_K):
        rank = jnp.sum(jnp.where(hots[k], before, 0.0), axis=0, keepdims=True).astype(jnp.int32)
        slab = jnp.where(r16 == k, idxs[k], slab)
        slab = jnp.where(r16 == TOP_K + k, rank, slab)
        slab = jnp.where(r16 == 2 * TOP_K + k, lax.bitcast_convert_type(es[k] / den, jnp.int32), slab)
    rt_ref[...] = slab
    carry = carry + jnp.broadcast_to(jnp.sum(member, axis=1, keepdims=True), carry.shape)
    carry_ref[...] = carry
    cnt_ref[...] = carry.astype(jnp.int32)


def _outproj(seq, x, mf, attn, g1, sh2, sc2, norm_g, wo_f, wo_a, b_o, w_r, b_r, tri):
    t_tokens = x.shape[0]
    tm = TOKEN_TILE
    tiles_per_seq = seq // tm
    row = lambda i: (i, 0)
    per_seq = lambda i: (i // tiles_per_seq, 0, 0)
    const = lambda i: (0, 0)
    return pl.pallas_call(
        _outproj_kernel,
        out_shape=[jax.ShapeDtypeStruct((t_tokens, D_MODEL), _f32),
                   jax.ShapeDtypeStruct((t_tokens, HALF), jnp.int32),
                   jax.ShapeDtypeStruct((ROUTE_ROWS, t_tokens), jnp.int32),
                   jax.ShapeDtypeStruct((N_EXPERTS, LANES), jnp.int32)],
        grid=(t_tokens // tm,),
        in_specs=[pl.BlockSpec((tm, D_MODEL), row),
                  pl.BlockSpec((FOURIER_WIDTH // LANES, tm, LANES), lambda i: (0, i, 0)),
                  pl.BlockSpec((tm, ATTN_WIDTH), row),
                  pl.BlockSpec((1, 1, D_MODEL), per_seq), pl.BlockSpec((1, 1, D_MODEL), per_seq),
                  pl.BlockSpec((1, 1, D_MODEL), per_seq),
                  pl.BlockSpec((1, D_MODEL), const),
                  pl.BlockSpec((FOURIER_WIDTH, D_MODEL), const), pl.BlockSpec((ATTN_WIDTH, D_MODEL), const),
                  pl.BlockSpec((1, D_MODEL), const),
                  pl.BlockSpec((N_EXPERTS, D_MODEL), const), pl.BlockSpec((N_EXPERTS, tm), const),
                  pl.BlockSpec((tm, tm), const)],
        out_specs=[pl.BlockSpec((tm, D_MODEL), row), pl.BlockSpec((tm, HALF), row),
                   pl.BlockSpec((ROUTE_ROWS, tm), lambda i: (0, i)),
                   pl.BlockSpec((N_EXPERTS, LANES), const)],
        scratch_shapes=[pltpu.VMEM((N_EXPERTS, LANES), _f32)],
        compiler_params=pltpu.CompilerParams(dimension_semantics=("arbitrary",), vmem_limit_bytes=VMEM_LIMIT),
        name="outproj",
    )(x, mf, attn, g1, sh2, sc2, norm_g, wo_f, wo_a, b_o, w_r, b_r, tri)


def _expert_kernel(blk_ref, bexp_ref, nact_ref, x_ref, wgu_ref, bgu_ref, wd_ref, bd_ref, y_ref, wgu_bf, wd_bf):
    i = pl.program_id(0)
    prev = bexp_ref[jnp.maximum(i - 1, 0)]

    @pl.when((i == 0) | (bexp_ref[i] != prev))
    def _():
        wgu_bf[...] = wgu_ref[...].astype(_bf16)
        wd_bf[...] = wd_ref[...].astype(_bf16)

    def gate_up(rows):
        xa, xb = _unpack_halves(x_ref[rows, :])
        return (jnp.dot(xa.astype(_bf16), wgu_bf[:HALF, :], preferred_element_type=_f32)
                + jnp.dot(xb.astype(_bf16), wgu_bf[HALF:, :], preferred_element_type=_f32) + bgu_ref[...])

    def activate(gu):
        g = jnp.minimum(gu[:, :D_FF], SWIGLU_LIMIT)
        u = jnp.clip(gu[:, D_FF:], -SWIGLU_LIMIT, SWIGLU_LIMIT)
        return ((u + 1.0) * (g * (1.0 / (1.0 + jnp.exp(-SWIGLU_ALPHA * g))))).astype(_bf16)

    def down(rows, act):
        y = jnp.dot(act, wd_bf[...], preferred_element_type=_f32) + bd_ref[...]
        y_ref[rows, :] = _pack_halves(y[:, :HALF], y[:, HALF:])

    @pl.when(i < nact_ref[0])
    def _():
        tiles = [pl.ds(r, EXPERT_SUB) for r in range(0, EXPERT_BLOCK, EXPERT_SUB)]
        gu = gate_up(tiles[0])
        for n, rows in enumerate(tiles):
            gu_next = gate_up(tiles[n + 1]) if n + 1 < len(tiles) else None
            down(rows, activate(gu))
            gu = gu_next


def _experts(layer, xs, blk, bexp, nact, w_gu, b_gu, w_down, b_down):
    p_rows = xs.shape[0]
    nblk = p_rows // EXPERT_BLOCK
    grid_spec = pltpu.PrefetchScalarGridSpec(
        num_scalar_prefetch=3,
        grid=(nblk,),
        in_specs=[pl.BlockSpec((EXPERT_BLOCK, HALF), lambda i, blk, be, na: (blk[i], 0)),
                  pl.BlockSpec((None, None, D_MODEL, 2 * D_FF), lambda i, blk, be, na: (layer, be[i], 0, 0)),
                  pl.BlockSpec((None, 1, 2 * D_FF), lambda i, blk, be, na: (be[i], 0, 0)),
                  pl.BlockSpec((None, None, D_FF, D_MODEL), lambda i, blk, be, na: (layer, be[i], 0, 0)),
                  pl.BlockSpec((None, 1, D_MODEL), lambda i, blk, be, na: (be[i], 0, 0))],
        out_specs=pl.BlockSpec((EXPERT_BLOCK, HALF), lambda i, blk, be, na: (blk[i], 0)),
        scratch_shapes=[pltpu.VMEM((D_MODEL, 2 * D_FF), _bf16), pltpu.VMEM((D_FF, D_MODEL), _bf16)])
    return pl.pallas_call(
        _expert_kernel,
        out_shape=jax.ShapeDtypeStruct((p_rows, HALF), jnp.int32),
        grid_spec=grid_spec,
        compiler_params=pltpu.CompilerParams(dimension_semantics=("arbitrary",), vmem_limit_bytes=VMEM_LIMIT),
        name="experts",
    )(blk, bexp, nact, xs, w_gu, b_gu.reshape(N_EXPERTS, 1, 2 * D_FF), w_down,
      b_down.reshape(N_EXPERTS, 1, D_MODEL))


def _sc_mesh():
    return plsc.VectorSubcoreMesh(core_axis_name="c", subcore_axis_name="s")


def _sc_worker():
    return lax.axis_index("s") * 2 + lax.axis_index("c")


def _dispatch(hp, dest, p_rows):
    t_tokens = hp.shape[0]
    per_worker = t_tokens // (SC_CHUNK * SC_WORKERS)

    @functools.partial(
        pl.kernel, mesh=_sc_mesh(),
        out_type=jax.ShapeDtypeStruct((p_rows, HALF), jnp.int32),
        scratch_types=[pltpu.VMEM((TOP_K, SC_CHUNK), jnp.int32),
                       pltpu.VMEM((SC_CHUNK, HALF), jnp.int32),
                       pltpu.SemaphoreType.DMA],
        name="dispatch")
    def run(hp_hbm, dest_hbm, xs_hbm, idx_v, rows_v, sem):
        base = _sc_worker() * per_worker

        @pl.loop(0, per_worker)
        def _(j):
            chunk = base + j
            for k in range(TOP_K):
                pltpu.sync_copy(dest_hbm.at[pl.ds(k * t_tokens + chunk * SC_CHUNK, SC_CHUNK)], idx_v.at[k])
            pltpu.sync_copy(hp_hbm.at[pl.ds(chunk * SC_CHUNK, SC_CHUNK)], rows_v)
            copies = [pltpu.async_copy(rows_v, xs_hbm.at[idx_v.at[k]], sem) for k in range(TOP_K)]
            for cp in copies:
                cp.wait()

    return run(hp, dest)


def _combine_rows(y, dest, w, x1, g2, seq):
    t_tokens = x1.shape[0]
    cc = COMBINE_CHUNK
    n_chunks = t_tokens // cc
    per_worker = n_chunks // SC_WORKERS
    own = per_worker * cc
    assert per_worker % 2 == 0 and seq % own == 0
    n_vec = HALF // SC_LANES
    row_buf = pltpu.VMEM((TOP_K, cc, HALF), jnp.int32)
    x_buf = pltpu.VMEM((cc, D_MODEL), _f32)

    @functools.partial(
        pl.kernel, mesh=_sc_mesh(),
        out_type=jax.ShapeDtypeStruct((t_tokens, D_MODEL), _f32),
        scratch_types=[pltpu.VMEM((TOP_K * own,), jnp.int32),
                       pltpu.VMEM((TOP_K * own,), _f32),
                       row_buf, row_buf, x_buf, x_buf,
                       pltpu.VMEM((D_MODEL,), _f32),
                       pltpu.SemaphoreType.DMA, pltpu.SemaphoreType.DMA,
                       pltpu.SemaphoreType.DMA, pltpu.SemaphoreType.DMA],
        compiler_params=pltpu.CompilerParams(needs_layout_passes=False),
        name="combine_rows")
    def run(y_hbm, dest_hbm, w_hbm, x1_hbm, g2_hbm, x2_hbm,
            idx_v, w_v, rows0, rows1, xb0, xb1, g_v, in0, in1, out0, out1):
        base = _sc_worker() * per_worker
        bufs = ((rows0, xb0, in0, out0), (rows1, xb1, in1, out1))
        for k in range(TOP_K):
            pltpu.sync_copy(dest_hbm.at[pl.ds(k * t_tokens + base * cc, own)], idx_v.at[pl.ds(k * own, own)])
            pltpu.sync_copy(w_hbm.at[pl.ds(k * t_tokens + base * cc, own)], w_v.at[pl.ds(k * own, own)])
        pltpu.sync_copy(g2_hbm.at[(base * cc) // seq], g_v)

        def loads(c, b):
            rows, xb, isem, _ = bufs[b]
            cps = [pltpu.make_async_copy(y_hbm.at[idx_v.at[pl.ds(k * own + c * cc, cc)]], rows.at[k], isem)
                   for k in range(TOP_K)]
            cps.append(pltpu.make_async_copy(x1_hbm.at[pl.ds((base + c) * cc, cc)], xb, isem))
            return cps

        def store(c, b):
            _, xb, _, osem = bufs[b]
            return pltpu.make_async_copy(xb, x2_hbm.at[pl.ds((base + c) * cc, cc)], osem)

        def combine(c, b):
            rows, xb, _, _ = bufs[b]

            @pl.loop(0, cc)
            def _(t):
                ws = [plsc.load_gather(w_v, [jnp.full((SC_LANES,), k * own + c * cc + t, jnp.int32)])
                      for k in range(TOP_K)]

                @plsc.parallel_loop(0, n_vec)
                def _(j):
                    lo = pl.ds(j * SC_LANES, SC_LANES)
                    hi = pl.ds(HALF + j * SC_LANES, SC_LANES)
                    acc_a = None
                    acc_b = None
                    for k in range(TOP_K):
                        p = rows[k, t, lo]
                        a = lax.bitcast_convert_type(p & HI_MASK, _f32) * ws[k]
                        b_ = lax.bitcast_convert_type(p << 16, _f32) * ws[k]
                        acc_a = a if acc_a is None else acc_a + a
                        acc_b = b_ if acc_b is None else acc_b + b_
                    xb[t, lo] = xb[t, lo] + g_v[lo] * acc_a
                    xb[t, hi] = xb[t, hi] + g_v[hi] * acc_b

        for cp in loads(0, 0):
            cp.start()

        @pl.loop(0, per_worker // 2)
        def _(c2):
            for b in range(2):
                c = 2 * c2 + b

                @pl.when(c >= 1)
                def _():
                    store(c - 1, 1 - b).wait()

                @pl.when(c + 1 < per_worker)
                def _():
                    for cp in loads(c + 1, 1 - b):
                        cp.start()

                for cp in loads(c, b):
                    cp.wait()
                combine(c, b)
                store(c, b).start()

        store(per_worker - 1, 1).wait()

    return run(y, dest, w, x1, g2)


def _final_kernel(x_ref, ng_ref, o_ref):
    o_ref[...] = _rms(x_ref[...], ng_ref[...])


def _final(x, final_g):
    t_tokens = x.shape[0]
    tm = 2 * TOKEN_TILE
    row = lambda i: (i, 0)
    return pl.pallas_call(
        _final_kernel,
        out_shape=jax.ShapeDtypeStruct((t_tokens, D_MODEL), _f32),
        grid=(t_tokens // tm,),
        in_specs=[pl.BlockSpec((tm, D_MODEL), row), pl.BlockSpec((1, D_MODEL), lambda i: (0, 0))],
        out_specs=pl.BlockSpec((tm, D_MODEL), row),
        compiler_params=pltpu.CompilerParams(dimension_semantics=("parallel",), vmem_limit_bytes=VMEM_LIMIT),
        name="final",
    )(x, final_g)


def _plan(route, counts, nblk):
    cnt = counts[:, 0]
    padded = ((cnt + EXPERT_BLOCK - 1) // EXPERT_BLOCK) * EXPERT_BLOCK
    pad_end = jnp.cumsum(padded)
    pad_start = pad_end - padded
    expert = jnp.arange(N_EXPERTS, dtype=jnp.int32)[:, None, None]
    start_of = jnp.sum(jnp.where(route[None, :TOP_K] == expert, pad_start[:, None, None], 0), axis=0)
    dest = start_of + route[TOP_K:2 * TOP_K]
    nact = jnp.maximum(pad_end[-1] // EXPERT_BLOCK, 1)
    blk = jnp.minimum(jnp.arange(nblk, dtype=jnp.int32), nact - 1)
    bexp = jnp.minimum(jnp.sum(pad_end[None, :] <= (blk * EXPERT_BLOCK)[:, None], axis=1), N_EXPERTS - 1)
    w = lax.bitcast_convert_type(route[2 * TOP_K:3 * TOP_K], _f32)
    return (dest.reshape(-1), w.reshape(-1), blk.astype(jnp.int32), bexp.astype(jnp.int32),
            nact.reshape(1).astype(jnp.int32))


def _trunk(x_all, c_all, norm1_g, ada_w, ada_b, w_in, b_in, w_fmix, sinks, w_o, b_o,
           norm2_g, w_router, b_router, w_gu, b_gu, w_down, b_down, final_g):
    nb, seq, _ = x_all.shape
    t_tokens = nb * seq
    assert seq % TOKEN_TILE == 0 and seq % (ATTN_TILES * Q_TILE) == 0 and seq >= KEY_SPAN
    assert t_tokens % (SC_CHUNK * SC_WORKERS) == 0 and seq % COMBINE_CHUNK == 0
    nblk = (t_tokens * TOP_K) // EXPERT_BLOCK + N_EXPERTS
    p_rows = nblk * EXPERT_BLOCK

    mod = _modulation(c_all, ada_w, ada_b)
    mod = mod.reshape(DEPTH, nb, 6, 1, D_MODEL)
    rope_tabs = _rope_tables(seq)
    dft = _dft_tables(seq)
    tri = jnp.triu(jnp.ones((TOKEN_TILE, TOKEN_TILE), _f32), 1).astype(_bf16)

    x = x_all.reshape(t_tokens, D_MODEL)
    for l in range(DEPTH):
        sh1, sc1, g1, sh2, sc2, g2 = (mod[l, :, j] for j in range(6))
        ab = _channel_dft_fold(w_fmix[l], seq)
        y, q, k, v = _inproj(seq, x, sh1, sc1, norm1_g[l].reshape(1, D_MODEL), w_in[l].astype(_bf16),
                             b_in[l].reshape(1, IN_WIDTH), ab, rope_tabs)
        mf = _fourier(seq, y, dft)
        attn = _attention(seq, q, k, v, sinks[l])
        w_o_bf = w_o[l].astype(_bf16)
        w_r = w_router[l].T.astype(_bf16)
        b_r = jnp.broadcast_to(b_router[l][:, None], (N_EXPERTS, TOKEN_TILE))
        x1, hp, route, counts = _outproj(seq, x, mf, attn, g1, sh2, sc2, norm2_g[l].reshape(1, D_MODEL),
                                         w_o_bf[:FOURIER_WIDTH], w_o_bf[FOURIER_WIDTH:],
                                         b_o[l].reshape(1, D_MODEL), w_r, b_r, tri)
        dest, w, blk, bexp, nact = _plan(route, counts, nblk)
        xs = _dispatch(hp, dest, p_rows)
        ys = _experts(l, xs, blk, bexp, nact, w_gu, b_gu[l], w_down, b_down[l])
        x = _combine_rows(ys, dest, w, x1, g2.reshape(nb, D_MODEL), seq)
    out = _final(x, final_g.reshape(1, D_MODEL))
    return out.reshape(nb, seq, D_MODEL)


def kernel(x_prompt, x_sample, c_prompt, c_sample, norm1_g, ada_w, ada_b, w_in, b_in, w_fmix, sinks, w_o, b_o,
           norm2_g, w_router, b_router, w_gu, b_gu, w_down, b_down, final_g):
    ws = (norm1_g, ada_w, ada_b, w_in, b_in, w_fmix, sinks, w_o, b_o,
          norm2_g, w_router, b_router, w_gu, b_gu, w_down, b_down, final_g)
    return _trunk(x_prompt, c_prompt, *ws), _trunk(x_sample, c_sample, *ws)
```

```python
import functools

import jax
import jax.numpy as jnp
from jax import lax
from jax.experimental import pallas as pl
from jax.experimental.pallas import tpu as pltpu
from jax.experimental.pallas import tpu_sc as plsc

D_MODEL = 1024
DEPTH = 4
FOURIER_WIDTH = 256
N_FGROUPS = 4
FGROUP_DIM = 64
HEAD_DIM = 64
N_Q_HEADS = 12
N_KV_HEADS = 4
Q_PER_KV = 3
ATTN_WIDTH = N_Q_HEADS * HEAD_DIM
KV_WIDTH = N_KV_HEADS * HEAD_DIM
IN_WIDTH = FOURIER_WIDTH + ATTN_WIDTH + 2 * KV_WIDTH
WINDOW = 128
ROPE_THETA = 500000.0
ROT_DIM = 16
N_EXPERTS = 32
TOP_K = 4
D_FF = 512
EXPERT_BLOCK = 1024
EXPERT_SUB = 256
SWIGLU_LIMIT = 7.0
SWIGLU_ALPHA = 1.702
NORM_EPS = 1e-5

LANES = 128
HALF = D_MODEL // 2
TOKEN_TILE = 512
WIDE_TILE = 1024
SUB_TILE = 256
Q_TILE = 256
ATTN_TILES = 4
KEY_SPAN = Q_TILE + 2 * WINDOW
SC_CHUNK = 128
SC_WORKERS = 32
SC_LANES = 16
COMBINE_CHUNK = 16
VMEM_LIMIT = 56 * 1024 * 1024
NEG_BIG = -1e30
LOG2E = 1.4426950408889634
Q_SCALE = HEAD_DIM ** -0.5 * LOG2E
HI_MASK = -65536
ROUTE_ROWS = 16

_f32 = jnp.float32
_bf16 = jnp.bfloat16


def _pack_halves(a, b):
    ua = lax.bitcast_convert_type(a.astype(_bf16).astype(_f32), jnp.int32)
    ub = lax.bitcast_convert_type(b.astype(_bf16).astype(_f32), jnp.int32)
    return ua | lax.shift_right_logical(ub, 16)


def _unpack_halves(p):
    a = lax.bitcast_convert_type(p & HI_MASK, _f32)
    b = lax.bitcast_convert_type(lax.shift_left(p, 16), _f32)
    return a, b


def _rms(x, g):
    ms = jnp.mean(x * x, axis=-1, keepdims=True)
    return x * lax.rsqrt(ms + NORM_EPS) * g


def _mod_kernel(c_ref, w_ref, b_ref, o_ref):
    c = c_ref[...]
    s = c * (1.0 / (1.0 + jnp.exp(-c)))
    o_ref[0] = jnp.dot(s.astype(_bf16), w_ref[0].astype(_bf16), preferred_element_type=_f32) + b_ref[0]


def _modulation(c_all, ada_w, ada_b):
    nb = c_all.shape[0]
    ncol = ada_w.shape[2] // D_MODEL
    return pl.pallas_call(
        _mod_kernel,
        out_shape=jax.ShapeDtypeStruct((DEPTH, nb, 6 * D_MODEL), _f32),
        grid=(DEPTH, ncol),
        in_specs=[
            pl.BlockSpec((nb, D_MODEL), lambda l, j: (0, 0)),
            pl.BlockSpec((1, D_MODEL, D_MODEL), lambda l, j: (l, 0, j)),
            pl.BlockSpec((1, 1, D_MODEL), lambda l, j: (l, 0, j)),
        ],
        out_specs=pl.BlockSpec((1, nb, D_MODEL), lambda l, j: (l, 0, j)),
        name="modulation",
    )(c_all, ada_w, ada_b.reshape(DEPTH, 1, 6 * D_MODEL))


def _inproj_matmul(x, sh_ref, sc_ref, ng_ref, w_ref, b_ref, *_):
    h = _rms(x, ng_ref[...]) * (1.0 + sc_ref[0]) + sh_ref[0]
    return jnp.dot(h.astype(_bf16), w_ref[...], preferred_element_type=_f32) + b_ref[...]


def _inproj_finish(rows, z, sh_ref, sc_ref, ng_ref, w_ref, b_ref, ab_ref, rc_ref, rs1_ref, rs2_ref,
                   y_ref, q_ref, k_ref, v_ref):
    f = z[:, :FOURIER_WIDTH]
    y_ref[rows, :] = jnp.dot(f.astype(_bf16), ab_ref[...], preferred_element_type=_f32).astype(_bf16)
    rc = rc_ref[rows, :]
    rs1 = rs1_ref[rows, :]
    rs2 = rs2_ref[rows, :]

    def rope(t):
        return t * rc + pltpu.roll(t, LANES - ROT_DIM // 2, 1) * rs1 + pltpu.roll(t, ROT_DIM // 2, 1) * rs2

    q0 = FOURIER_WIDTH
    for c in range(ATTN_WIDTH // LANES):
        t = z[:, q0 + c * LANES:q0 + (c + 1) * LANES]
        q_ref[rows, c * LANES:(c + 1) * LANES] = (rope(t) * Q_SCALE).astype(_bf16)
    k0 = q0 + ATTN_WIDTH
    for c in range(KV_WIDTH // LANES):
        t = z[:, k0 + c * LANES:k0 + (c + 1) * LANES]
        k_ref[rows, c * LANES:(c + 1) * LANES] = rope(t).astype(_bf16)
    v_ref[rows, :] = z[:, k0 + KV_WIDTH:].astype(_bf16)


def _sub_tiles(n_rows):
    return [pl.ds(r, SUB_TILE) for r in range(0, n_rows, SUB_TILE)]


def _inproj_kernel(x_ref, *refs):
    tiles = _sub_tiles(x_ref.shape[0])
    z = _inproj_matmul(x_ref[tiles[0], :], *refs)
    for n, rows in enumerate(tiles):
        z_next = _inproj_matmul(x_ref[tiles[n + 1], :], *refs) if n + 1 < len(tiles) else None
        _inproj_finish(rows, z, *refs)
        z = z_next


def _inproj(seq, x, sh1, sc1, norm_g, w_in, b_in, ab, rope_tabs):
    t_tokens = x.shape[0]
    tm = WIDE_TILE
    tiles_per_seq = seq // tm
    row = lambda i: (i, 0)
    per_seq = lambda i: (i // tiles_per_seq, 0, 0)
    const = lambda i: (0, 0)
    pos = lambda i: (i % tiles_per_seq, 0)
    return pl.pallas_call(
        _inproj_kernel,
        out_shape=[jax.ShapeDtypeStruct((t_tokens, 2 * FOURIER_WIDTH), _bf16),
                   jax.ShapeDtypeStruct((t_tokens, ATTN_WIDTH), _bf16),
                   jax.ShapeDtypeStruct((t_tokens, KV_WIDTH), _bf16),
                   jax.ShapeDtypeStruct((t_tokens, KV_WIDTH), _bf16)],
        grid=(t_tokens // tm,),
        in_specs=[pl.BlockSpec((tm, D_MODEL), row),
                  pl.BlockSpec((1, 1, D_MODEL), per_seq), pl.BlockSpec((1, 1, D_MODEL), per_seq),
                  pl.BlockSpec((1, D_MODEL), const),
                  pl.BlockSpec((D_MODEL, IN_WIDTH), const), pl.BlockSpec((1, IN_WIDTH), const),
                  pl.BlockSpec((FOURIER_WIDTH, 2 * FOURIER_WIDTH), const),
                  pl.BlockSpec((tm, LANES), pos), pl.BlockSpec((tm, LANES), pos), pl.BlockSpec((tm, LANES), pos)],
        out_specs=[pl.BlockSpec((tm, 2 * FOURIER_WIDTH), row), pl.BlockSpec((tm, ATTN_WIDTH), row),
                   pl.BlockSpec((tm, KV_WIDTH), row), pl.BlockSpec((tm, KV_WIDTH), row)],
        compiler_params=pltpu.CompilerParams(dimension_semantics=("parallel",), vmem_limit_bytes=VMEM_LIMIT),
        name="inproj",
    )(x, sh1, sc1, norm_g, w_in, b_in, ab, *rope_tabs)


def _fourier_kernel(y_ref, ce_ref, se_ref, co_ref, so_ref, o_ref):
    half = y_ref.shape[0] // 2
    lo = y_ref[:half, :].astype(_f32)
    hi = y_ref[half:, :].astype(_f32)
    ye = (lo + hi).astype(_bf16)
    yo = (lo - hi).astype(_bf16)
    w = FOURIER_WIDTH
    even = (jnp.dot(ce_ref[...], ye[:, :w], preferred_element_type=_f32)
            + jnp.dot(se_ref[...], ye[:, w:], preferred_element_type=_f32))
    odd = (jnp.dot(co_ref[...], yo[:, :w], preferred_element_type=_f32)
           + jnp.dot(so_ref[...], yo[:, w:], preferred_element_type=_f32))
    for c in range(FOURIER_WIDTH // LANES):
        o_ref[c, pl.ds(0, half, stride=2), :] = even[:, c * LANES:(c + 1) * LANES]
        o_ref[c, pl.ds(1, half, stride=2), :] = odd[:, c * LANES:(c + 1) * LANES]


def _fourier(seq, y, dft):
    t_tokens = y.shape[0]
    half = seq // 2
    const = lambda b: (0, 0)
    return pl.pallas_call(
        _fourier_kernel,
        out_shape=jax.ShapeDtypeStruct((FOURIER_WIDTH // LANES, t_tokens, LANES), _f32),
        grid=(t_tokens // seq,),
        in_specs=[pl.BlockSpec((seq, 2 * FOURIER_WIDTH), lambda b: (b, 0))]
        + [pl.BlockSpec((half, half), const)] * 4,
        out_specs=pl.BlockSpec((FOURIER_WIDTH // LANES, seq, LANES), lambda b: (0, b, 0)),
        compiler_params=pltpu.CompilerParams(dimension_semantics=("parallel",), vmem_limit_bytes=VMEM_LIMIT),
        name="fourier",
    )(y, *dft)


def _dft_tables(seq):
    half = seq // 2
    j = jnp.arange(half, dtype=jnp.int32)[:, None]
    k = jnp.arange(half, dtype=jnp.int32)[None, :]
    ang_e = ((2 * j * k) % seq).astype(_f32) * (2.0 * jnp.pi / seq)
    ang_o = (((2 * j + 1) * k) % seq).astype(_f32) * (2.0 * jnp.pi / seq)
    return (jnp.cos(ang_e).astype(_bf16), jnp.sin(ang_e).astype(_bf16),
            jnp.cos(ang_o).astype(_bf16), jnp.sin(ang_o).astype(_bf16))


def _channel_dft_fold(w_fmix, seq):
    c = jnp.arange(FGROUP_DIM, dtype=jnp.int32)
    ang = ((c[:, None] * c[None, :]) % FGROUP_DIM).astype(_f32) * (2.0 * jnp.pi / FGROUP_DIM)
    scale = (seq * FGROUP_DIM) ** -0.5
    hp = lax.Precision.HIGHEST
    a = jnp.einsum('cm,gmd->gcd', jnp.cos(ang) * scale, w_fmix, precision=hp)
    b = jnp.einsum('cm,gmd->gcd', -jnp.sin(ang) * scale, w_fmix, precision=hp)
    eye = jnp.eye(N_FGROUPS, dtype=_f32)
    bd = lambda m: jnp.einsum('gcd,gh->gchd', m, eye).reshape(FOURIER_WIDTH, FOURIER_WIDTH)
    return jnp.concatenate([bd(a), bd(b)], axis=1).astype(_bf16)


def _rope_tables(seq):
    half = ROT_DIM // 2
    inv_freq = jnp.power(ROPE_THETA, -jnp.arange(0, ROT_DIM, 2, dtype=_f32) / ROT_DIM)
    ang = jnp.arange(seq, dtype=_f32)[:, None] * inv_freq[None, :]
    cos, sin = jnp.cos(ang), jnp.sin(ang)
    ones = jnp.ones((seq, HEAD_DIM - ROT_DIM), _f32)
    zeros = jnp.zeros((seq, HEAD_DIM - ROT_DIM), _f32)
    zh = jnp.zeros((seq, half), _f32)
    c1 = jnp.concatenate([cos, cos, ones], axis=1)
    s1 = jnp.concatenate([-sin, zh, zeros], axis=1)
    s2 = jnp.concatenate([zh, sin, zeros], axis=1)
    rep = LANES // HEAD_DIM
    return tuple(jnp.tile(t, (1, rep)) for t in (c1, s1, s2))


def _attn_kernel(sink_ref, q_ref, k_ref, v_ref, o_ref, kpad, vtb, ot):
    seq = k_ref.shape[0]
    nqb = seq // Q_TILE
    nkb = seq // WINDOW

    @pl.when(pl.program_id(1) == 0)
    def _():
        zk = jnp.zeros((WINDOW, HEAD_DIM), _bf16)
        for h in range(N_KV_HEADS):
            kpad[h, :WINDOW, :] = zk
            kpad[h, WINDOW + seq:, :] = zk
            kpad[h, WINDOW:WINDOW + seq, :] = k_ref[:, h * HEAD_DIM:(h + 1) * HEAD_DIM]
        zv = jnp.zeros((KV_WIDTH, WINDOW), _bf16)
        vtb[0] = zv
        vtb[nkb + 1] = zv
        for j in range(nkb):
            vtb[j + 1] = v_ref[j * WINDOW:(j + 1) * WINDOW, :].astype(_f32).T.astype(_bf16)

    ka = lax.broadcasted_iota(jnp.int32, (WINDOW, WINDOW), 0)
    qc = lax.broadcasted_iota(jnp.int32, (WINDOW, WINDOW), 1)
    tri_ge = jnp.where(ka >= qc, 0.0, NEG_BIG).astype(_f32)
    tri_le = jnp.where(ka <= qc, 0.0, NEG_BIG).astype(_f32)
    zero_blk = jnp.zeros((WINDOW, WINDOW), _bf16)
    n_tiles = q_ref.shape[0] // Q_TILE
    qbs = [pl.program_id(1) * n_tiles + t for t in range(n_tiles)]
    bias_first = [jnp.where(qb == 0, NEG_BIG, tri_ge) for qb in qbs]
    bias_last = [jnp.where(qb == nqb - 1, NEG_BIG, tri_le) for qb in qbs]

    def softmax_col(blocks, sink):
        m = blocks[0]
        for b in blocks[1:]:
            m = jnp.maximum(m, b)
        m = jnp.maximum(jnp.max(m, axis=0, keepdims=True), sink)
        ps = [jnp.exp2(b - m) for b in blocks]
        tot = ps[0]
        for p in ps[1:]:
            tot = tot + p
        return ps, jnp.sum(tot, axis=0, keepdims=True) + jnp.exp2(sink - m)

    w = WINDOW

    def scores(t, h):
        r0 = pl.multiple_of(qbs[t] * Q_TILE, Q_TILE)
        kh = kpad[h, pl.ds(r0, KEY_SPAN), :]
        q3 = jnp.concatenate([q_ref[t * Q_TILE:(t + 1) * Q_TILE, g * HEAD_DIM:(g + 1) * HEAD_DIM]
                              for g in range(Q_PER_KV * h, Q_PER_KV * (h + 1))], axis=0)
        return lax.dot_general(kh, q3, (((1,), (1,)), ((), ())), preferred_element_type=_f32)

    work = [(t, h) for t in range(n_tiles) for h in range(N_KV_HEADS)]
    s = scores(*work[0])
    for n, (t, h) in enumerate(work):
        s_next = scores(*work[n + 1]) if n + 1 < len(work) else None
        cols, dens = [], []
        for i in range(Q_PER_KV):
            sink = sink_ref[Q_PER_KV * h + i] * LOG2E
            c0 = 2 * i * w
            c1 = c0 + w
            p0, l0 = softmax_col([s[0:w, c0:c1] + bias_first[t], s[w:2 * w, c0:c1], s[2 * w:3 * w, c0:c1] + tri_le],
                                 sink)
            p1, l1 = softmax_col([s[w:2 * w, c1:c1 + w] + tri_ge, s[2 * w:3 * w, c1:c1 + w],
                                  s[3 * w:, c1:c1 + w] + bias_last[t]], sink)
            cols.append(jnp.concatenate([p.astype(_bf16) for p in p0] + [zero_blk], axis=0))
            cols.append(jnp.concatenate([zero_blk] + [p.astype(_bf16) for p in p1], axis=0))
            dens += [l0, l1]
        pt = jnp.concatenate(cols, axis=1)
        kb0 = qbs[t] * (Q_TILE // WINDOW)
        vth = jnp.concatenate([vtb[kb0 + j, h * HEAD_DIM:(h + 1) * HEAD_DIM, :]
                               for j in range(KEY_SPAN // WINDOW)], axis=1)
        o_t = jnp.dot(vth, pt, preferred_element_type=_f32) / jnp.concatenate(dens, axis=1)
        for i in range(Q_PER_KV):
            g = Q_PER_KV * h + i
            ot[t, g * HEAD_DIM:(g + 1) * HEAD_DIM, :] = o_t[:, i * Q_TILE:(i + 1) * Q_TILE]
        if h == N_KV_HEADS - 1:
            for c in range(ATTN_WIDTH // LANES):
                o_ref[t * Q_TILE:(t + 1) * Q_TILE, c * LANES:(c + 1) * LANES] = (
                    ot[t, c * LANES:(c + 1) * LANES, :].T.astype(_bf16))
        s = s_next


def _attention(seq, q, k, v, sinks):
    t_tokens = q.shape[0]
    rows = ATTN_TILES * Q_TILE
    steps = seq // rows
    return pl.pallas_call(
        _attn_kernel,
        out_shape=jax.ShapeDtypeStruct((t_tokens, ATTN_WIDTH), _bf16),
        grid=(t_tokens // seq, steps),
        in_specs=[pl.BlockSpec(memory_space=pltpu.SMEM),
                  pl.BlockSpec((rows, ATTN_WIDTH), lambda b, j: (b * steps + j, 0)),
                  pl.BlockSpec((seq, KV_WIDTH), lambda b, j: (b, 0)),
                  pl.BlockSpec((seq, KV_WIDTH), lambda b, j: (b, 0))],
        out_specs=pl.BlockSpec((rows, ATTN_WIDTH), lambda b, j: (b * steps + j, 0)),
        scratch_shapes=[pltpu.VMEM((N_KV_HEADS, seq + 2 * WINDOW, HEAD_DIM), _bf16),
                        pltpu.VMEM((seq // WINDOW + 2, KV_WIDTH, WINDOW), _bf16),
                        pltpu.VMEM((ATTN_TILES, ATTN_WIDTH, Q_TILE), _f32)],
        compiler_params=pltpu.CompilerParams(dimension_semantics=("parallel", "arbitrary"),
                                             vmem_limit_bytes=VMEM_LIMIT),
        name="attention",
    )(sinks, q, k, v)


def _outproj_kernel(x_ref, mf_ref, at_ref, g1_ref, sh_ref, sc_ref, ng_ref, wof_ref, woa_ref, bo_ref,
                    wr_ref, br_ref, tri_ref, x1_ref, hp_ref, rt_ref, cnt_ref, carry_ref):
    i = pl.program_id(0)

    @pl.when(i == 0)
    def _():
        carry_ref[...] = jnp.zeros_like(carry_ref)

    def project(rows):
        mf = jnp.concatenate([mf_ref[c, rows, :] for c in range(FOURIER_WIDTH // LANES)], axis=1)
        return (jnp.dot(mf.astype(_bf16), wof_ref[...], preferred_element_type=_f32)
                + jnp.dot(at_ref[rows, :], woa_ref[...], preferred_element_type=_f32) + bo_ref[...])

    tiles = _sub_tiles(x_ref.shape[0])
    mix = project(tiles[0])
    logit_cols = []
    for n, rows in enumerate(tiles):
        mix_next = project(tiles[n + 1]) if n + 1 < len(tiles) else None
        x1 = x_ref[rows, :] + g1_ref[0] * mix
        x1_ref[rows, :] = x1
        h = _rms(x1, ng_ref[...]) * (1.0 + sc_ref[0]) + sh_ref[0]
        hp_ref[rows, :] = _pack_halves(h[:, :HALF], h[:, HALF:])
        logit_cols.append(lax.dot_general(wr_ref[...], h.astype(_bf16), (((1,), (1,)), ((), ())),
                                          preferred_element_type=_f32))
        mix = mix_next
    logits = jnp.concatenate(logit_cols, axis=1) + br_ref[...]
    tm = logits.shape[1]
    erow = lax.broadcasted_iota(jnp.int32, logits.shape, 0)
    work = logits
    hots, vals, idxs = [], [], []
    for _k in range(TOP_K):
        mx = jnp.max(work, axis=0, keepdims=True)
        ix = jnp.min(jnp.where(work == mx, erow, N_EXPERTS), axis=0, keepdims=True)
        hot = erow == ix
        work = jnp.where(hot, -jnp.inf, work)
        hots.append(hot)
        vals.append(mx)
        idxs.append(ix)
    es = [jnp.exp(v - vals[0]) for v in vals]
    den = es[0] + es[1] + es[2] + es[3]
    member = jnp.zeros(logits.shape, _f32)
    for hot in hots:
        member = member + hot.astype(_f32)
    carry = carry_ref[...]
    before = (jnp.dot(member.astype(_bf16), tri_ref[...], preferred_element_type=_f32)
              + jnp.concatenate([carry] * (tm // LANES), axis=1))
    r16 = lax.broadcasted_iota(jnp.int32, (ROUTE_ROWS, tm), 0)
    slab = jnp.zeros((ROUTE_ROWS, tm), jnp.int32)
    for k in range(TOP_K):
        rank = jnp.sum(jnp.where(hots[k], before, 0.0), axis=0, keepdims=True).astype(jnp.int32)
        slab = jnp.where(r16 == k, idxs[k], slab)
        slab = jnp.where(r16 == TOP_K + k, rank, slab)
        slab = jnp.where(r16 == 2 * TOP_K + k, lax.bitcast_convert_type(es[k] / den, jnp.int32), slab)
    rt_ref[...] = slab
    carry = carry + jnp.broadcast_to(jnp.sum(member, axis=1, keepdims=True), carry.shape)
    carry_ref[...] = carry
    cnt_ref[...] = carry.astype(jnp.int32)


def _outproj(seq, x, mf, attn, g1, sh2, sc2, norm_g, wo_f, wo_a, b_o, w_r, b_r, tri):
    t_tokens = x.shape[0]
    tm = TOKEN_TILE
    tiles_per_seq = seq // tm
    row = lambda i: (i, 0)
    per_seq = lambda i: (i // tiles_per_seq, 0, 0)
    const = lambda i: (0, 0)
    return pl.pallas_call(
        _outproj_kernel,
        out_shape=[jax.ShapeDtypeStruct((t_tokens, D_MODEL), _f32),
                   jax.ShapeDtypeStruct((t_tokens, HALF), jnp.int32),
                   jax.ShapeDtypeStruct((ROUTE_ROWS, t_tokens), jnp.int32),
                   jax.ShapeDtypeStruct((N_EXPERTS, LANES), jnp.int32)],
        grid=(t_tokens // tm,),
        in_specs=[pl.BlockSpec((tm, D_MODEL), row),
                  pl.BlockSpec((FOURIER_WIDTH // LANES, tm, LANES), lambda i: (0, i, 0)),
                  pl.BlockSpec((tm, ATTN_WIDTH), row),
                  pl.BlockSpec((1, 1, D_MODEL), per_seq), pl.BlockSpec((1, 1, D_MODEL), per_seq),
                  pl.BlockSpec((1, 1, D_MODEL), per_seq),
                  pl.BlockSpec((1, D_MODEL), const),
                  pl.BlockSpec((FOURIER_WIDTH, D_MODEL), const), pl.BlockSpec((ATTN_WIDTH, D_MODEL), const),
                  pl.BlockSpec((1, D_MODEL), const),
                  pl.BlockSpec((N_EXPERTS, D_MODEL), const), pl.BlockSpec((N_EXPERTS, tm), const),
                  pl.BlockSpec((tm, tm), const)],
        out_specs=[pl.BlockSpec((tm, D_MODEL), row), pl.BlockSpec((tm, HALF), row),
                   pl.BlockSpec((ROUTE_ROWS, tm), lambda i: (0, i)),
                   pl.BlockSpec((N_EXPERTS, LANES), const)],
        scratch_shapes=[pltpu.VMEM((N_EXPERTS, LANES), _f32)],
        compiler_params=pltpu.CompilerParams(dimension_semantics=("arbitrary",), vmem_limit_bytes=VMEM_LIMIT),
        name="outproj",
    )(x, mf, attn, g1, sh2, sc2, norm_g, wo_f, wo_a, b_o, w_r, b_r, tri)


def _expert_kernel(blk_ref, bexp_ref, nsub_ref, x_ref, wgu_ref, bgu_ref, wd_ref, bd_ref, y_ref, wgu_bf, wd_bf):
    i = pl.program_id(0)
    prev = bexp_ref[jnp.maximum(i - 1, 0)]

    @pl.when((i == 0) | (bexp_ref[i] != prev))
    def _():
        wgu_bf[...] = wgu_ref[...].astype(_bf16)
        wd_bf[...] = wd_ref[...].astype(_bf16)

    def gate_up(rows):
        xa, xb = _unpack_halves(x_ref[rows, :])
        return (jnp.dot(xa.astype(_bf16), wgu_bf[:HALF, :], preferred_element_type=_f32)
                + jnp.dot(xb.astype(_bf16), wgu_bf[HALF:, :], preferred_element_type=_f32) + bgu_ref[...])

    def activate(gu):
        g = jnp.minimum(gu[:, :D_FF], SWIGLU_LIMIT)
        u = jnp.clip(gu[:, D_FF:], -SWIGLU_LIMIT, SWIGLU_LIMIT)
        return ((u + 1.0) * (g * (1.0 / (1.0 + jnp.exp(-SWIGLU_ALPHA * g))))).astype(_bf16)

    def down(rows, act):
        y = jnp.dot(act, wd_bf[...], preferred_element_type=_f32) + bd_ref[...]
        y_ref[rows, :] = _pack_halves(y[:, :HALF], y[:, HALF:])

    def run(n_sub):
        tiles = [pl.ds(r * EXPERT_SUB, EXPERT_SUB) for r in range(n_sub)]
        gu = gate_up(tiles[0])
        for n, rows in enumerate(tiles):
            gu_next = gate_up(tiles[n + 1]) if n + 1 < len(tiles) else None
            down(rows, activate(gu))
            gu = gu_next

    for n_sub in range(1, EXPERT_BLOCK // EXPERT_SUB + 1):
        pl.when(nsub_ref[i] == n_sub)(functools.partial(run, n_sub))


def _experts(layer, xs, blk, bexp, nsub, w_gu, b_gu, w_down, b_down):
    p_rows = xs.shape[0]
    nblk = p_rows // EXPERT_BLOCK
    grid_spec = pltpu.PrefetchScalarGridSpec(
        num_scalar_prefetch=3,
        grid=(nblk,),
        in_specs=[pl.BlockSpec((EXPERT_BLOCK, HALF), lambda i, blk, be, na: (blk[i], 0)),
                  pl.BlockSpec((None, None, D_MODEL, 2 * D_FF), lambda i, blk, be, na: (layer, be[i], 0, 0)),
                  pl.BlockSpec((None, 1, 2 * D_FF), lambda i, blk, be, na: (be[i], 0, 0)),
                  pl.BlockSpec((None, None, D_FF, D_MODEL), lambda i, blk, be, na: (layer, be[i], 0, 0)),
                  pl.BlockSpec((None, 1, D_MODEL), lambda i, blk, be, na: (be[i], 0, 0))],
        out_specs=pl.BlockSpec((EXPERT_BLOCK, HALF), lambda i, blk, be, na: (blk[i], 0)),
        scratch_shapes=[pltpu.VMEM((D_MODEL, 2 * D_FF), _bf16), pltpu.VMEM((D_FF, D_MODEL), _bf16)])
    return pl.pallas_call(
        _expert_kernel,
        out_shape=jax.ShapeDtypeStruct((p_rows, HALF), jnp.int32),
        grid_spec=grid_spec,
        compiler_params=pltpu.CompilerParams(dimension_semantics=("arbitrary",), vmem_limit_bytes=VMEM_LIMIT),
        name="experts",
    )(blk, bexp, nsub, xs, w_gu, b_gu.reshape(N_EXPERTS, 1, 2 * D_FF), w_down,
      b_down.reshape(N_EXPERTS, 1, D_MODEL))


def _sc_mesh():
    return plsc.VectorSubcoreMesh(core_axis_name="c", subcore_axis_name="s")


def _sc_worker():
    return lax.axis_index("s") * 2 + lax.axis_index("c")


def _dispatch(hp, dest, p_rows):
    t_tokens = hp.shape[0]
    per_worker = t_tokens // (SC_CHUNK * SC_WORKERS)

    @functools.partial(
        pl.kernel, mesh=_sc_mesh(),
        out_type=jax.ShapeDtypeStruct((p_rows, HALF), jnp.int32),
        scratch_types=[pltpu.VMEM((TOP_K, SC_CHUNK), jnp.int32),
                       pltpu.VMEM((SC_CHUNK, HALF), jnp.int32),
                       pltpu.SemaphoreType.DMA],
        name="dispatch")
    def run(hp_hbm, dest_hbm, xs_hbm, idx_v, rows_v, sem):
        base = _sc_worker() * per_worker

        @pl.loop(0, per_worker)
        def _(j):
            chunk = base + j
            for k in range(TOP_K):
                pltpu.sync_copy(dest_hbm.at[pl.ds(k * t_tokens + chunk * SC_CHUNK, SC_CHUNK)], idx_v.at[k])
            pltpu.sync_copy(hp_hbm.at[pl.ds(chunk * SC_CHUNK, SC_CHUNK)], rows_v)
            copies = [pltpu.async_copy(rows_v, xs_hbm.at[idx_v.at[k]], sem) for k in range(TOP_K)]
            for cp in copies:
                cp.wait()

    return run(hp, dest)


def _combine_rows(y, dest, w, x1, g2, seq):
    t_tokens = x1.shape[0]
    cc = COMBINE_CHUNK
    n_chunks = t_tokens // cc
    per_worker = n_chunks // SC_WORKERS
    own = per_worker * cc
    assert per_worker % 2 == 0 and seq % own == 0
    n_vec = HALF // SC_LANES
    row_buf = pltpu.VMEM((TOP_K, cc, HALF), jnp.int32)
    x_buf = pltpu.VMEM((cc, D_MODEL), _f32)

    @functools.partial(
        pl.kernel, mesh=_sc_mesh(),
        out_type=jax.ShapeDtypeStruct((t_tokens, D_MODEL), _f32),
        scratch_types=[pltpu.VMEM((TOP_K * own,), jnp.int32),
                       pltpu.VMEM((TOP_K * own,), _f32),
                       row_buf, row_buf, x_buf, x_buf,
                       pltpu.VMEM((D_MODEL,), _f32),
                       pltpu.SemaphoreType.DMA, pltpu.SemaphoreType.DMA,
                       pltpu.SemaphoreType.DMA, pltpu.SemaphoreType.DMA],
        compiler_params=pltpu.CompilerParams(needs_layout_passes=False),
        name="combine_rows")
    def run(y_hbm, dest_hbm, w_hbm, x1_hbm, g2_hbm, x2_hbm,
            idx_v, w_v, rows0, rows1, xb0, xb1, g_v, in0, in1, out0, out1):
        base = _sc_worker() * per_worker
        bufs = ((rows0, xb0, in0, out0), (rows1, xb1, in1, out1))
        for k in range(TOP_K):
            pltpu.sync_copy(dest_hbm.at[pl.ds(k * t_tokens + base * cc, own)], idx_v.at[pl.ds(k * own, own)])
            pltpu.sync_copy(w_hbm.at[pl.ds(k * t_tokens + base * cc, own)], w_v.at[pl.ds(k * own, own)])
        pltpu.sync_copy(g2_hbm.at[(base * cc) // seq], g_v)

        def loads(c, b):
            rows, xb, isem, _ = bufs[b]
            cps = [pltpu.make_async_copy(y_hbm.at[idx_v.at[pl.ds(k * own + c * cc, cc)]], rows.at[k], isem)
                   for k in range(TOP_K)]
            cps.append(pltpu.make_async_copy(x1_hbm.at[pl.ds((base + c) * cc, cc)], xb, isem))
            return cps

        def store(c, b):
            _, xb, _, osem = bufs[b]
            return pltpu.make_async_copy(xb, x2_hbm.at[pl.ds((base + c) * cc, cc)], osem)

        def combine(c, b):
            rows, xb, _, _ = bufs[b]

            @pl.loop(0, cc)
            def _(t):
                ws = [plsc.load_gather(w_v, [jnp.full((SC_LANES,), k * own + c * cc + t, jnp.int32)])
                      for k in range(TOP_K)]

                @plsc.parallel_loop(0, n_vec)
                def _(j):
                    lo = pl.ds(j * SC_LANES, SC_LANES)
                    hi = pl.ds(HALF + j * SC_LANES, SC_LANES)
                    acc_a = None
                    acc_b = None
                    for k in range(TOP_K):
                        p = rows[k, t, lo]
                        a = lax.bitcast_convert_type(p & HI_MASK, _f32) * ws[k]
                        b_ = lax.bitcast_convert_type(p << 16, _f32) * ws[k]
                        acc_a = a if acc_a is None else acc_a + a
                        acc_b = b_ if acc_b is None else acc_b + b_
                    xb[t, lo] = xb[t, lo] + g_v[lo] * acc_a
                    xb[t, hi] = xb[t, hi] + g_v[hi] * acc_b

        for cp in loads(0, 0):
            cp.start()

        @pl.loop(0, per_worker // 2)
        def _(c2):
            for b in range(2):
                c = 2 * c2 + b

                @pl.when(c >= 1)
                def _():
                    store(c - 1, 1 - b).wait()

                @pl.when(c + 1 < per_worker)
                def _():
                    for cp in loads(c + 1, 1 - b):
                        cp.start()

                for cp in loads(c, b):
                    cp.wait()
                combine(c, b)
                store(c, b).start()

        store(per_worker - 1, 1).wait()

    return run(y, dest, w, x1, g2)


def _final_kernel(x_ref, ng_ref, o_ref):
    o_ref[...] = _rms(x_ref[...], ng_ref[...])


def _final(x, final_g):
    t_tokens = x.shape[0]
    tm = WIDE_TILE
    row = lambda i: (i, 0)
    return pl.pallas_call(
        _final_kernel,
        out_shape=jax.ShapeDtypeStruct((t_tokens, D_MODEL), _f32),
        grid=(t_tokens // tm,),
        in_specs=[pl.BlockSpec((tm, D_MODEL), row), pl.BlockSpec((1, D_MODEL), lambda i: (0, 0))],
        out_specs=pl.BlockSpec((tm, D_MODEL), row),
        compiler_params=pltpu.CompilerParams(dimension_semantics=("parallel",), vmem_limit_bytes=VMEM_LIMIT),
        name="final",
    )(x, final_g)


def _plan(route, counts, nblk):
    cnt = counts[:, 0]
    padded = ((cnt + EXPERT_BLOCK - 1) // EXPERT_BLOCK) * EXPERT_BLOCK
    pad_end = jnp.cumsum(padded)
    pad_start = pad_end - padded
    expert = jnp.arange(N_EXPERTS, dtype=jnp.int32)[:, None, None]
    start_of = jnp.sum(jnp.where(route[None, :TOP_K] == expert, pad_start[:, None, None], 0), axis=0)
    dest = start_of + route[TOP_K:2 * TOP_K]
    nact = jnp.maximum(pad_end[-1] // EXPERT_BLOCK, 1)
    step = jnp.arange(nblk, dtype=jnp.int32)
    blk = jnp.minimum(step, nact - 1)
    bexp = jnp.minimum(jnp.sum(pad_end[None, :] <= (blk * EXPERT_BLOCK)[:, None], axis=1), N_EXPERTS - 1)
    onehot = bexp[:, None] == jnp.arange(N_EXPERTS, dtype=jnp.int32)[None, :]
    routed = jnp.sum(jnp.where(onehot, cnt + pad_start - (blk * EXPERT_BLOCK)[:, None], 0), axis=1)
    routed = jnp.clip(routed, 0, EXPERT_BLOCK)
    nsub = jnp.where(step < pad_end[-1] // EXPERT_BLOCK, (routed + EXPERT_SUB - 1) // EXPERT_SUB, 0)
    w = lax.bitcast_convert_type(route[2 * TOP_K:3 * TOP_K], _f32)
    return (dest.reshape(-1), w.reshape(-1), blk.astype(jnp.int32), bexp.astype(jnp.int32), nsub.astype(jnp.int32))


def _trunk(x_all, c_all, norm1_g, ada_w, ada_b, w_in, b_in, w_fmix, sinks, w_o, b_o,
           norm2_g, w_router, b_router, w_gu, b_gu, w_down, b_down, final_g):
    nb, seq, _ = x_all.shape
    t_tokens = nb * seq
    assert seq % WIDE_TILE == 0 and seq % (ATTN_TILES * Q_TILE) == 0 and seq >= KEY_SPAN
    assert t_tokens % (SC_CHUNK * SC_WORKERS) == 0 and seq % COMBINE_CHUNK == 0
    nblk = (t_tokens * TOP_K) // EXPERT_BLOCK + N_EXPERTS
    p_rows = nblk * EXPERT_BLOCK

    mod = _modulation(c_all, ada_w, ada_b)
    mod = mod.reshape(DEPTH, nb, 6, 1, D_MODEL)
    rope_tabs = _rope_tables(seq)
    dft = _dft_tables(seq)
    tri = jnp.triu(jnp.ones((TOKEN_TILE, TOKEN_TILE), _f32), 1).astype(_bf16)

    x = x_all.reshape(t_tokens, D_MODEL)
    for l in range(DEPTH):
        sh1, sc1, g1, sh2, sc2, g2 = (mod[l, :, j] for j in range(6))
        ab = _channel_dft_fold(w_fmix[l], seq)
        y, q, k, v = _inproj(seq, x, sh1, sc1, norm1_g[l].reshape(1, D_MODEL), w_in[l].astype(_bf16),
                             b_in[l].reshape(1, IN_WIDTH), ab, rope_tabs)
        mf = _fourier(seq, y, dft)
        attn = _attention(seq, q, k, v, sinks[l])
        w_o_bf = w_o[l].astype(_bf16)
        w_r = w_router[l].T.astype(_bf16)
        b_r = jnp.broadcast_to(b_router[l][:, None], (N_EXPERTS, TOKEN_TILE))
        x1, hp, route, counts = _outproj(seq, x, mf, attn, g1, sh2, sc2, norm2_g[l].reshape(1, D_MODEL),
                                         w_o_bf[:FOURIER_WIDTH], w_o_bf[FOURIER_WIDTH:],
                                         b_o[l].reshape(1, D_MODEL), w_r, b_r, tri)
        dest, w, blk, bexp, nsub = _plan(route, counts, nblk)
        xs = _dispatch(hp, dest, p_rows)
        ys = _experts(l, xs, blk, bexp, nsub, w_gu, b_gu[l], w_down, b_down[l])
        x = _combine_rows(ys, dest, w, x1, g2.reshape(nb, D_MODEL), seq)
    out = _final(x, final_g.reshape(1, D_MODEL))
    return out.reshape(nb, seq, D_MODEL)


def kernel(x_prompt, x_sample, c_prompt, c_sample, norm1_g, ada_w, ada_b, w_in, b_in, w_fmix, sinks, w_o, b_o,
           norm2_g, w_router, b_router, w_gu, b_gu, w_down, b_down, final_g):
    ws = (norm1_g, ada_w, ada_b, w_in, b_in, w_fmix, sinks, w_o, b_o,
          norm2_g, w_router, b_router, w_gu, b_gu, w_down, b_down, final_g)
    return _trunk(x_prompt, c_prompt, *ws), _trunk(x_sample, c_sample, *ws)
```

```python
import functools

import jax
import jax.numpy as jnp
from jax import lax
from jax.experimental import pallas as pl
from jax.experimental.pallas import tpu as pltpu
from jax.experimental.pallas import tpu_sc as plsc

D_MODEL = 1024
DEPTH = 4
FOURIER_WIDTH = 256
N_FGROUPS = 4
FGROUP_DIM = 64
HEAD_DIM = 64
N_Q_HEADS = 12
N_KV_HEADS = 4
Q_PER_KV = 3
ATTN_WIDTH = N_Q_HEADS * HEAD_DIM
KV_WIDTH = N_KV_HEADS * HEAD_DIM
IN_WIDTH = FOURIER_WIDTH + ATTN_WIDTH + 2 * KV_WIDTH
WINDOW = 128
ROPE_THETA = 500000.0
ROT_DIM = 16
N_EXPERTS = 32
TOP_K = 4
D_FF = 512
EXPERT_BLOCK = 1024
EXPERT_SUB = 512
SWIGLU_LIMIT = 7.0
SWIGLU_ALPHA = 1.702
NORM_EPS = 1e-5

LANES = 128
HALF = D_MODEL // 2
TOKEN_TILE = 1024
RANK_TILE = 512
SUB_TILE = 256
Q_TILE = 256
ATTN_TILES = 4
KEY_SPAN = Q_TILE + 2 * WINDOW
SC_CHUNK = 128
SC_WORKERS = 32
SC_LANES = 16
COMBINE_CHUNK = 16
VMEM_LIMIT = 56 * 1024 * 1024
NEG_BIG = -1e30
LOG2E = 1.4426950408889634
Q_SCALE = HEAD_DIM ** -0.5 * LOG2E
HI_MASK = -65536
ROUTE_ROWS = 16

_f32 = jnp.float32
_bf16 = jnp.bfloat16


def _pack_halves(a, b):
    ua = lax.bitcast_convert_type(a.astype(_bf16).astype(_f32), jnp.int32)
    ub = lax.bitcast_convert_type(b.astype(_bf16).astype(_f32), jnp.int32)
    return ua | lax.shift_right_logical(ub, 16)


def _unpack_halves(p):
    a = lax.bitcast_convert_type(p & HI_MASK, _f32)
    b = lax.bitcast_convert_type(lax.shift_left(p, 16), _f32)
    return a, b


def _rms(x, g):
    ms = jnp.mean(x * x, axis=-1, keepdims=True)
    return x * lax.rsqrt(ms + NORM_EPS) * g


def _mod_kernel(c_ref, w_ref, b_ref, o_ref):
    c = c_ref[...]
    s = c * (1.0 / (1.0 + jnp.exp(-c)))
    o_ref[0] = jnp.dot(s.astype(_bf16), w_ref[0].astype(_bf16), preferred_element_type=_f32) + b_ref[0]


def _modulation(c_all, ada_w, ada_b):
    nb = c_all.shape[0]
    ncol = ada_w.shape[2] // D_MODEL
    return pl.pallas_call(
        _mod_kernel,
        out_shape=jax.ShapeDtypeStruct((DEPTH, nb, 6 * D_MODEL), _f32),
        grid=(DEPTH, ncol),
        in_specs=[
            pl.BlockSpec((nb, D_MODEL), lambda l, j: (0, 0)),
            pl.BlockSpec((1, D_MODEL, D_MODEL), lambda l, j: (l, 0, j)),
            pl.BlockSpec((1, 1, D_MODEL), lambda l, j: (l, 0, j)),
        ],
        out_specs=pl.BlockSpec((1, nb, D_MODEL), lambda l, j: (l, 0, j)),
        name="modulation",
    )(c_all, ada_w, ada_b.reshape(DEPTH, 1, 6 * D_MODEL))


def _inproj_matmul(x, sh_ref, sc_ref, ng_ref, w_ref, b_ref, *_):
    h = _rms(x, ng_ref[...]) * (1.0 + sc_ref[0]) + sh_ref[0]
    return jnp.dot(h.astype(_bf16), w_ref[...], preferred_element_type=_f32) + b_ref[...]


def _inproj_finish(rows, z, sh_ref, sc_ref, ng_ref, w_ref, b_ref, ab_ref, rc_ref, rs1_ref, rs2_ref,
                   y_ref, q_ref, k_ref, v_ref):
    f = z[:, :FOURIER_WIDTH]
    y_ref[rows, :] = jnp.dot(f.astype(_bf16), ab_ref[...], preferred_element_type=_f32).astype(_bf16)
    rc = rc_ref[rows, :]
    rs1 = rs1_ref[rows, :]
    rs2 = rs2_ref[rows, :]

    def rope(t):
        return t * rc + pltpu.roll(t, LANES - ROT_DIM // 2, 1) * rs1 + pltpu.roll(t, ROT_DIM // 2, 1) * rs2

    q0 = FOURIER_WIDTH
    for c in range(ATTN_WIDTH // LANES):
        t = z[:, q0 + c * LANES:q0 + (c + 1) * LANES]
        q_ref[rows, c * LANES:(c + 1) * LANES] = (rope(t) * Q_SCALE).astype(_bf16)
    k0 = q0 + ATTN_WIDTH
    for c in range(KV_WIDTH // LANES):
        t = z[:, k0 + c * LANES:k0 + (c + 1) * LANES]
        k_ref[rows, c * LANES:(c + 1) * LANES] = rope(t).astype(_bf16)
    v_ref[rows, :] = z[:, k0 + KV_WIDTH:].astype(_bf16)


def _sub_tiles(n_rows):
    return [pl.ds(r, SUB_TILE) for r in range(0, n_rows, SUB_TILE)]


def _inproj_kernel(x_ref, *refs):
    tiles = _sub_tiles(x_ref.shape[0])
    z = _inproj_matmul(x_ref[tiles[0], :], *refs)
    for n, rows in enumerate(tiles):
        z_next = _inproj_matmul(x_ref[tiles[n + 1], :], *refs) if n + 1 < len(tiles) else None
        _inproj_finish(rows, z, *refs)
        z = z_next


def _inproj(seq, x, sh1, sc1, norm_g, w_in, b_in, ab, rope_tabs):
    t_tokens = x.shape[0]
    tm = TOKEN_TILE
    tiles_per_seq = seq // tm
    row = lambda i: (i, 0)
    per_seq = lambda i: (i // tiles_per_seq, 0, 0)
    const = lambda i: (0, 0)
    pos = lambda i: (i % tiles_per_seq, 0)
    return pl.pallas_call(
        _inproj_kernel,
        out_shape=[jax.ShapeDtypeStruct((t_tokens, 2 * FOURIER_WIDTH), _bf16),
                   jax.ShapeDtypeStruct((t_tokens, ATTN_WIDTH), _bf16),
                   jax.ShapeDtypeStruct((t_tokens, KV_WIDTH), _bf16),
                   jax.ShapeDtypeStruct((t_tokens, KV_WIDTH), _bf16)],
        grid=(t_tokens // tm,),
        in_specs=[pl.BlockSpec((tm, D_MODEL), row),
                  pl.BlockSpec((1, 1, D_MODEL), per_seq), pl.BlockSpec((1, 1, D_MODEL), per_seq),
                  pl.BlockSpec((1, D_MODEL), const),
                  pl.BlockSpec((D_MODEL, IN_WIDTH), const), pl.BlockSpec((1, IN_WIDTH), const),
                  pl.BlockSpec((FOURIER_WIDTH, 2 * FOURIER_WIDTH), const),
                  pl.BlockSpec((tm, LANES), pos), pl.BlockSpec((tm, LANES), pos), pl.BlockSpec((tm, LANES), pos)],
        out_specs=[pl.BlockSpec((tm, 2 * FOURIER_WIDTH), row), pl.BlockSpec((tm, ATTN_WIDTH), row),
                   pl.BlockSpec((tm, KV_WIDTH), row), pl.BlockSpec((tm, KV_WIDTH), row)],
        compiler_params=pltpu.CompilerParams(dimension_semantics=("parallel",), vmem_limit_bytes=VMEM_LIMIT),
        name="inproj",
    )(x, sh1, sc1, norm_g, w_in, b_in, ab, *rope_tabs)


def _fourier_kernel(y_ref, ce_ref, se_ref, co_ref, so_ref, o_ref):
    half = y_ref.shape[0] // 2
    lo = y_ref[:half, :].astype(_f32)
    hi = y_ref[half:, :].astype(_f32)
    ye = (lo + hi).astype(_bf16)
    yo = (lo - hi).astype(_bf16)
    w = FOURIER_WIDTH
    even = (jnp.dot(ce_ref[...], ye[:, :w], preferred_element_type=_f32)
            + jnp.dot(se_ref[...], ye[:, w:], preferred_element_type=_f32))
    odd = (jnp.dot(co_ref[...], yo[:, :w], preferred_element_type=_f32)
           + jnp.dot(so_ref[...], yo[:, w:], preferred_element_type=_f32))
    for c in range(FOURIER_WIDTH // LANES):
        o_ref[c, pl.ds(0, half, stride=2), :] = even[:, c * LANES:(c + 1) * LANES]
        o_ref[c, pl.ds(1, half, stride=2), :] = odd[:, c * LANES:(c + 1) * LANES]


def _fourier(seq, y, dft):
    t_tokens = y.shape[0]
    half = seq // 2
    const = lambda b: (0, 0)
    return pl.pallas_call(
        _fourier_kernel,
        out_shape=jax.ShapeDtypeStruct((FOURIER_WIDTH // LANES, t_tokens, LANES), _f32),
        grid=(t_tokens // seq,),
        in_specs=[pl.BlockSpec((seq, 2 * FOURIER_WIDTH), lambda b: (b, 0))]
        + [pl.BlockSpec((half, half), const)] * 4,
        out_specs=pl.BlockSpec((FOURIER_WIDTH // LANES, seq, LANES), lambda b: (0, b, 0)),
        compiler_params=pltpu.CompilerParams(dimension_semantics=("parallel",), vmem_limit_bytes=VMEM_LIMIT),
        name="fourier",
    )(y, *dft)


def _dft_tables(seq):
    half = seq // 2
    j = jnp.arange(half, dtype=jnp.int32)[:, None]
    k = jnp.arange(half, dtype=jnp.int32)[None, :]
    ang_e = ((2 * j * k) % seq).astype(_f32) * (2.0 * jnp.pi / seq)
    ang_o = (((2 * j + 1) * k) % seq).astype(_f32) * (2.0 * jnp.pi / seq)
    return (jnp.cos(ang_e).astype(_bf16), jnp.sin(ang_e).astype(_bf16),
            jnp.cos(ang_o).astype(_bf16), jnp.sin(ang_o).astype(_bf16))


def _channel_dft_fold(w_fmix, seq):
    c = jnp.arange(FGROUP_DIM, dtype=jnp.int32)
    ang = ((c[:, None] * c[None, :]) % FGROUP_DIM).astype(_f32) * (2.0 * jnp.pi / FGROUP_DIM)
    scale = (seq * FGROUP_DIM) ** -0.5
    hp = lax.Precision.HIGHEST
    a = jnp.einsum('cm,gmd->gcd', jnp.cos(ang) * scale, w_fmix, precision=hp)
    b = jnp.einsum('cm,gmd->gcd', -jnp.sin(ang) * scale, w_fmix, precision=hp)
    eye = jnp.eye(N_FGROUPS, dtype=_f32)
    bd = lambda m: jnp.einsum('gcd,gh->gchd', m, eye).reshape(FOURIER_WIDTH, FOURIER_WIDTH)
    return jnp.concatenate([bd(a), bd(b)], axis=1).astype(_bf16)


def _rope_tables(seq):
    half = ROT_DIM // 2
    inv_freq = jnp.power(ROPE_THETA, -jnp.arange(0, ROT_DIM, 2, dtype=_f32) / ROT_DIM)
    ang = jnp.arange(seq, dtype=_f32)[:, None] * inv_freq[None, :]
    cos, sin = jnp.cos(ang), jnp.sin(ang)
    ones = jnp.ones((seq, HEAD_DIM - ROT_DIM), _f32)
    zeros = jnp.zeros((seq, HEAD_DIM - ROT_DIM), _f32)
    zh = jnp.zeros((seq, half), _f32)
    c1 = jnp.concatenate([cos, cos, ones], axis=1)
    s1 = jnp.concatenate([-sin, zh, zeros], axis=1)
    s2 = jnp.concatenate([zh, sin, zeros], axis=1)
    rep = LANES // HEAD_DIM
    return tuple(jnp.tile(t, (1, rep)) for t in (c1, s1, s2))


def _attn_kernel(sink_ref, q_ref, k_ref, v_ref, o_ref, kpad, vtb, ot):
    seq = k_ref.shape[0]
    nqb = seq // Q_TILE
    nkb = seq // WINDOW

    @pl.when(pl.program_id(1) == 0)
    def _():
        zk = jnp.zeros((WINDOW, HEAD_DIM), _bf16)
        for h in range(N_KV_HEADS):
            kpad[h, :WINDOW, :] = zk
            kpad[h, WINDOW + seq:, :] = zk
            kpad[h, WINDOW:WINDOW + seq, :] = k_ref[:, h * HEAD_DIM:(h + 1) * HEAD_DIM]
        zv = jnp.zeros((KV_WIDTH, WINDOW), _bf16)
        vtb[0] = zv
        vtb[nkb + 1] = zv
        for j in range(nkb):
            vtb[j + 1] = v_ref[j * WINDOW:(j + 1) * WINDOW, :].astype(_f32).T.astype(_bf16)

    ka = lax.broadcasted_iota(jnp.int32, (WINDOW, WINDOW), 0)
    qc = lax.broadcasted_iota(jnp.int32, (WINDOW, WINDOW), 1)
    tri_ge = jnp.where(ka >= qc, 0.0, NEG_BIG).astype(_f32)
    tri_le = jnp.where(ka <= qc, 0.0, NEG_BIG).astype(_f32)
    zero_blk = jnp.zeros((WINDOW, WINDOW), _bf16)
    n_tiles = q_ref.shape[0] // Q_TILE
    qbs = [pl.program_id(1) * n_tiles + t for t in range(n_tiles)]
    bias_first = [jnp.where(qb == 0, NEG_BIG, tri_ge) for qb in qbs]
    bias_last = [jnp.where(qb == nqb - 1, NEG_BIG, tri_le) for qb in qbs]

    def softmax_col(blocks, sink):
        m = blocks[0]
        for b in blocks[1:]:
            m = jnp.maximum(m, b)
        m = jnp.maximum(jnp.max(m, axis=0, keepdims=True), sink)
        ps = [jnp.exp2(b - m) for b in blocks]
        tot = ps[0]
        for p in ps[1:]:
            tot = tot + p
        return ps, jnp.sum(tot, axis=0, keepdims=True) + jnp.exp2(sink - m)

    w = WINDOW

    def scores(t, h):
        r0 = pl.multiple_of(qbs[t] * Q_TILE, Q_TILE)
        kh = kpad[h, pl.ds(r0, KEY_SPAN), :]
        q3 = jnp.concatenate([q_ref[t * Q_TILE:(t + 1) * Q_TILE, g * HEAD_DIM:(g + 1) * HEAD_DIM]
                              for g in range(Q_PER_KV * h, Q_PER_KV * (h + 1))], axis=0)
        return lax.dot_general(kh, q3, (((1,), (1,)), ((), ())), preferred_element_type=_f32)

    work = [(t, h) for t in range(n_tiles) for h in range(N_KV_HEADS)]
    s = scores(*work[0])
    for n, (t, h) in enumerate(work):
        s_next = scores(*work[n + 1]) if n + 1 < len(work) else None
        cols, dens = [], []
        for i in range(Q_PER_KV):
            sink = sink_ref[Q_PER_KV * h + i] * LOG2E
            c0 = 2 * i * w
            c1 = c0 + w
            p0, l0 = softmax_col([s[0:w, c0:c1] + bias_first[t], s[w:2 * w, c0:c1], s[2 * w:3 * w, c0:c1] + tri_le],
                                 sink)
            p1, l1 = softmax_col([s[w:2 * w, c1:c1 + w] + tri_ge, s[2 * w:3 * w, c1:c1 + w],
                                  s[3 * w:, c1:c1 + w] + bias_last[t]], sink)
            cols.append(jnp.concatenate([p.astype(_bf16) for p in p0] + [zero_blk], axis=0))
            cols.append(jnp.concatenate([zero_blk] + [p.astype(_bf16) for p in p1], axis=0))
            dens += [l0, l1]
        pt = jnp.concatenate(cols, axis=1)
        kb0 = qbs[t] * (Q_TILE // WINDOW)
        vth = jnp.concatenate([vtb[kb0 + j, h * HEAD_DIM:(h + 1) * HEAD_DIM, :]
                               for j in range(KEY_SPAN // WINDOW)], axis=1)
        o_t = jnp.dot(vth, pt, preferred_element_type=_f32) / jnp.concatenate(dens, axis=1)
        for i in range(Q_PER_KV):
            g = Q_PER_KV * h + i
            ot[t, g * HEAD_DIM:(g + 1) * HEAD_DIM, :] = o_t[:, i * Q_TILE:(i + 1) * Q_TILE]
        if h == N_KV_HEADS - 1:
            for c in range(ATTN_WIDTH // LANES):
                o_ref[t * Q_TILE:(t + 1) * Q_TILE, c * LANES:(c + 1) * LANES] = (
                    ot[t, c * LANES:(c + 1) * LANES, :].T.astype(_bf16))
        s = s_next


def _attention(seq, q, k, v, sinks):
    t_tokens = q.shape[0]
    rows = ATTN_TILES * Q_TILE
    steps = seq // rows
    return pl.pallas_call(
        _attn_kernel,
        out_shape=jax.ShapeDtypeStruct((t_tokens, ATTN_WIDTH), _bf16),
        grid=(t_tokens // seq, steps),
        in_specs=[pl.BlockSpec(memory_space=pltpu.SMEM),
                  pl.BlockSpec((rows, ATTN_WIDTH), lambda b, j: (b * steps + j, 0)),
                  pl.BlockSpec((seq, KV_WIDTH), lambda b, j: (b, 0)),
                  pl.BlockSpec((seq, KV_WIDTH), lambda b, j: (b, 0))],
        out_specs=pl.BlockSpec((rows, ATTN_WIDTH), lambda b, j: (b * steps + j, 0)),
        scratch_shapes=[pltpu.VMEM((N_KV_HEADS, seq + 2 * WINDOW, HEAD_DIM), _bf16),
                        pltpu.VMEM((seq // WINDOW + 2, KV_WIDTH, WINDOW), _bf16),
                        pltpu.VMEM((ATTN_TILES, ATTN_WIDTH, Q_TILE), _f32)],
        compiler_params=pltpu.CompilerParams(dimension_semantics=("parallel", "arbitrary"),
                                             vmem_limit_bytes=VMEM_LIMIT),
        name="attention",
    )(sinks, q, k, v)


def _outproj_kernel(x_ref, mf_ref, at_ref, g1_ref, sh_ref, sc_ref, ng_ref, wof_ref, woa_ref, bo_ref,
                    wr_ref, br_ref, tri_ref, x1_ref, hp_ref, rt_ref, cnt_ref, carry_ref):
    i = pl.program_id(0)

    @pl.when(i == 0)
    def _():
        carry_ref[...] = jnp.zeros_like(carry_ref)

    def project(rows):
        mf = jnp.concatenate([mf_ref[c, rows, :] for c in range(FOURIER_WIDTH // LANES)], axis=1)
        return (jnp.dot(mf.astype(_bf16), wof_ref[...], preferred_element_type=_f32)
                + jnp.dot(at_ref[rows, :], woa_ref[...], preferred_element_type=_f32) + bo_ref[...])

    tiles = _sub_tiles(x_ref.shape[0])
    mix = project(tiles[0])
    logit_cols = []
    for n, rows in enumerate(tiles):
        mix_next = project(tiles[n + 1]) if n + 1 < len(tiles) else None
        x1 = x_ref[rows, :] + g1_ref[0] * mix
        x1_ref[rows, :] = x1
        h = _rms(x1, ng_ref[...]) * (1.0 + sc_ref[0]) + sh_ref[0]
        hp_ref[rows, :] = _pack_halves(h[:, :HALF], h[:, HALF:])
        logit_cols.append(lax.dot_general(wr_ref[...], h.astype(_bf16), (((1,), (1,)), ((), ())),
                                          preferred_element_type=_f32))
        mix = mix_next
    logits = jnp.concatenate(logit_cols, axis=1) + br_ref[...]
    tm = logits.shape[1]
    erow = lax.broadcasted_iota(jnp.int32, logits.shape, 0)
    work = logits
    hots, vals, idxs = [], [], []
    for _k in range(TOP_K):
        mx = jnp.max(work, axis=0, keepdims=True)
        ix = jnp.min(jnp.where(work == mx, erow, N_EXPERTS), axis=0, keepdims=True)
        hot = erow == ix
        work = jnp.where(hot, -jnp.inf, work)
        hots.append(hot)
        vals.append(mx)
        idxs.append(ix)
    es = [jnp.exp(v - vals[0]) for v in vals]
    den = es[0] + es[1] + es[2] + es[3]
    member = jnp.zeros(logits.shape, _f32)
    for hot in hots:
        member = member + hot.astype(_f32)
    carry = carry_ref[...]
    parts = []
    for c0 in range(0, tm, RANK_TILE):
        m = member[:, c0:c0 + RANK_TILE]
        parts.append(jnp.dot(m.astype(_bf16), tri_ref[...], preferred_element_type=_f32)
                     + jnp.concatenate([carry] * (RANK_TILE // LANES), axis=1))
        carry = carry + jnp.broadcast_to(jnp.sum(m, axis=1, keepdims=True), carry.shape)
    before = jnp.concatenate(parts, axis=1)
    r16 = lax.broadcasted_iota(jnp.int32, (ROUTE_ROWS, tm), 0)
    slab = jnp.zeros((ROUTE_ROWS, tm), jnp.int32)
    for k in range(TOP_K):
        rank = jnp.sum(jnp.where(hots[k], before, 0.0), axis=0, keepdims=True).astype(jnp.int32)
        slab = jnp.where(r16 == k, idxs[k], slab)
        slab = jnp.where(r16 == TOP_K + k, rank, slab)
        slab = jnp.where(r16 == 2 * TOP_K + k, lax.bitcast_convert_type(es[k] / den, jnp.int32), slab)
    rt_ref[...] = slab
    carry_ref[...] = carry
    cnt_ref[...] = carry.astype(jnp.int32)


def _outproj(seq, x, mf, attn, g1, sh2, sc2, norm_g, wo_f, wo_a, b_o, w_r, b_r, tri):
    t_tokens = x.shape[0]
    tm = TOKEN_TILE
    tiles_per_seq = seq // tm
    row = lambda i: (i, 0)
    per_seq = lambda i: (i // tiles_per_seq, 0, 0)
    const = lambda i: (0, 0)
    return pl.pallas_call(
        _outproj_kernel,
        out_shape=[jax.ShapeDtypeStruct((t_tokens, D_MODEL), _f32),
                   jax.ShapeDtypeStruct((t_tokens, HALF), jnp.int32),
                   jax.ShapeDtypeStruct((ROUTE_ROWS, t_tokens), jnp.int32),
                   jax.ShapeDtypeStruct((N_EXPERTS, LANES), jnp.int32)],
        grid=(t_tokens // tm,),
        in_specs=[pl.BlockSpec((tm, D_MODEL), row),
                  pl.BlockSpec((FOURIER_WIDTH // LANES, tm, LANES), lambda i: (0, i, 0)),
                  pl.BlockSpec((tm, ATTN_WIDTH), row),
                  pl.BlockSpec((1, 1, D_MODEL), per_seq), pl.BlockSpec((1, 1, D_MODEL), per_seq),
                  pl.BlockSpec((1, 1, D_MODEL), per_seq),
                  pl.BlockSpec((1, D_MODEL), const),
                  pl.BlockSpec((FOURIER_WIDTH, D_MODEL), const), pl.BlockSpec((ATTN_WIDTH, D_MODEL), const),
                  pl.BlockSpec((1, D_MODEL), const),
                  pl.BlockSpec((N_EXPERTS, D_MODEL), const), pl.BlockSpec((N_EXPERTS, tm), const),
                  pl.BlockSpec((RANK_TILE, RANK_TILE), const)],
        out_specs=[pl.BlockSpec((tm, D_MODEL), row), pl.BlockSpec((tm, HALF), row),
                   pl.BlockSpec((ROUTE_ROWS, tm), lambda i: (0, i)),
                   pl.BlockSpec((N_EXPERTS, LANES), const)],
        scratch_shapes=[pltpu.VMEM((N_EXPERTS, LANES), _f32)],
        compiler_params=pltpu.CompilerParams(dimension_semantics=("arbitrary",), vmem_limit_bytes=VMEM_LIMIT),
        name="outproj",
    )(x, mf, attn, g1, sh2, sc2, norm_g, wo_f, wo_a, b_o, w_r, b_r, tri)


def _expert_kernel(blk_ref, bexp_ref, active_ref, x_ref, wgu_ref, bgu_ref, wd_ref, bd_ref, y_ref, wgu_bf, wd_bf):
    i = pl.program_id(0)
    prev = bexp_ref[jnp.maximum(i - 1, 0)]

    @pl.when((i == 0) | (bexp_ref[i] != prev))
    def _():
        wgu_bf[...] = wgu_ref[...].astype(_bf16)
        wd_bf[...] = wd_ref[...].astype(_bf16)

    def gate_up(rows):
        xa, xb = _unpack_halves(x_ref[rows, :])
        return (jnp.dot(xa.astype(_bf16), wgu_bf[:HALF, :], preferred_element_type=_f32)
                + jnp.dot(xb.astype(_bf16), wgu_bf[HALF:, :], preferred_element_type=_f32) + bgu_ref[...])

    def activate(gu):
        g = jnp.minimum(gu[:, :D_FF], SWIGLU_LIMIT)
        u = jnp.clip(gu[:, D_FF:], -SWIGLU_LIMIT, SWIGLU_LIMIT)
        return ((u + 1.0) * (g * (1.0 / (1.0 + jnp.exp(-SWIGLU_ALPHA * g))))).astype(_bf16)

    def down(rows, act):
        y = jnp.dot(act, wd_bf[...], preferred_element_type=_f32) + bd_ref[...]
        y_ref[rows, :] = _pack_halves(y[:, :HALF], y[:, HALF:])

    @pl.when(active_ref[i] > 0)
    def _():
        tiles = [pl.ds(r, EXPERT_SUB) for r in range(0, EXPERT_BLOCK, EXPERT_SUB)]
        gu = gate_up(tiles[0])
        for n, rows in enumerate(tiles):
            gu_next = gate_up(tiles[n + 1]) if n + 1 < len(tiles) else None
            down(rows, activate(gu))
            gu = gu_next


def _experts(layer, xs, blk, bexp, active, w_gu, b_gu, w_down, b_down):
    p_rows = xs.shape[0]
    nblk = p_rows // EXPERT_BLOCK
    grid_spec = pltpu.PrefetchScalarGridSpec(
        num_scalar_prefetch=3,
        grid=(nblk,),
        in_specs=[pl.BlockSpec((EXPERT_BLOCK, HALF), lambda i, blk, be, na: (blk[i], 0)),
                  pl.BlockSpec((None, None, D_MODEL, 2 * D_FF), lambda i, blk, be, na: (layer, be[i], 0, 0)),
                  pl.BlockSpec((None, 1, 2 * D_FF), lambda i, blk, be, na: (be[i], 0, 0)),
                  pl.BlockSpec((None, None, D_FF, D_MODEL), lambda i, blk, be, na: (layer, be[i], 0, 0)),
                  pl.BlockSpec((None, 1, D_MODEL), lambda i, blk, be, na: (be[i], 0, 0))],
        out_specs=pl.BlockSpec((EXPERT_BLOCK, HALF), lambda i, blk, be, na: (blk[i], 0)),
        scratch_shapes=[pltpu.VMEM((D_MODEL, 2 * D_FF), _bf16), pltpu.VMEM((D_FF, D_MODEL), _bf16)])
    return pl.pallas_call(
        _expert_kernel,
        out_shape=jax.ShapeDtypeStruct((p_rows, HALF), jnp.int32),
        grid_spec=grid_spec,
        compiler_params=pltpu.CompilerParams(dimension_semantics=("arbitrary",), vmem_limit_bytes=VMEM_LIMIT),
        name="experts",
    )(blk, bexp, active, xs, w_gu, b_gu.reshape(N_EXPERTS, 1, 2 * D_FF), w_down,
      b_down.reshape(N_EXPERTS, 1, D_MODEL))


def _sc_mesh():
    return plsc.VectorSubcoreMesh(core_axis_name="c", subcore_axis_name="s")


def _sc_worker():
    return lax.axis_index("s") * 2 + lax.axis_index("c")


def _dispatch(hp, dest, p_rows):
    t_tokens = hp.shape[0]
    per_worker = t_tokens // (SC_CHUNK * SC_WORKERS)

    @functools.partial(
        pl.kernel, mesh=_sc_mesh(),
        out_type=jax.ShapeDtypeStruct((p_rows, HALF), jnp.int32),
        scratch_types=[pltpu.VMEM((TOP_K, SC_CHUNK), jnp.int32),
                       pltpu.VMEM((SC_CHUNK, HALF), jnp.int32),
                       pltpu.SemaphoreType.DMA],
        name="dispatch")
    def run(hp_hbm, dest_hbm, xs_hbm, idx_v, rows_v, sem):
        base = _sc_worker() * per_worker

        @pl.loop(0, per_worker)
        def _(j):
            chunk = base + j
            for k in range(TOP_K):
                pltpu.sync_copy(dest_hbm.at[pl.ds(k * t_tokens + chunk * SC_CHUNK, SC_CHUNK)], idx_v.at[k])
            pltpu.sync_copy(hp_hbm.at[pl.ds(chunk * SC_CHUNK, SC_CHUNK)], rows_v)
            copies = [pltpu.async_copy(rows_v, xs_hbm.at[idx_v.at[k]], sem) for k in range(TOP_K)]
            for cp in copies:
                cp.wait()

    return run(hp, dest)


def _combine_rows(y, dest, w, x1, g2, seq):
    t_tokens = x1.shape[0]
    cc = COMBINE_CHUNK
    n_chunks = t_tokens // cc
    per_worker = n_chunks // SC_WORKERS
    own = per_worker * cc
    assert per_worker % 2 == 0 and seq % own == 0
    n_vec = HALF // SC_LANES
    row_buf = pltpu.VMEM((TOP_K, cc, HALF), jnp.int32)
    x_buf = pltpu.VMEM((cc, D_MODEL), _f32)

    @functools.partial(
        pl.kernel, mesh=_sc_mesh(),
        out_type=jax.ShapeDtypeStruct((t_tokens, D_MODEL), _f32),
        scratch_types=[pltpu.VMEM((TOP_K * own,), jnp.int32),
                       pltpu.VMEM((TOP_K * own,), _f32),
                       row_buf, row_buf, x_buf, x_buf,
                       pltpu.VMEM((D_MODEL,), _f32),
                       pltpu.SemaphoreType.DMA, pltpu.SemaphoreType.DMA,
                       pltpu.SemaphoreType.DMA, pltpu.SemaphoreType.DMA],
        compiler_params=pltpu.CompilerParams(needs_layout_passes=False),
        name="combine_rows")
    def run(y_hbm, dest_hbm, w_hbm, x1_hbm, g2_hbm, x2_hbm,
            idx_v, w_v, rows0, rows1, xb0, xb1, g_v, in0, in1, out0, out1):
        base = _sc_worker() * per_worker
        bufs = ((rows0, xb0, in0, out0), (rows1, xb1, in1, out1))
        for k in range(TOP_K):
            pltpu.sync_copy(dest_hbm.at[pl.ds(k * t_tokens + base * cc, own)], idx_v.at[pl.ds(k * own, own)])
            pltpu.sync_copy(w_hbm.at[pl.ds(k * t_tokens + base * cc, own)], w_v.at[pl.ds(k * own, own)])
        pltpu.sync_copy(g2_hbm.at[(base * cc) // seq], g_v)

        def loads(c, b):
            rows, xb, isem, _ = bufs[b]
            cps = [pltpu.make_async_copy(y_hbm.at[idx_v.at[pl.ds(k * own + c * cc, cc)]], rows.at[k], isem)
                   for k in range(TOP_K)]
            cps.append(pltpu.make_async_copy(x1_hbm.at[pl.ds((base + c) * cc, cc)], xb, isem))
            return cps

        def store(c, b):
            _, xb, _, osem = bufs[b]
            return pltpu.make_async_copy(xb, x2_hbm.at[pl.ds((base + c) * cc, cc)], osem)

        def combine(c, b):
            rows, xb, _, _ = bufs[b]

            @pl.loop(0, cc)
            def _(t):
                ws = [plsc.load_gather(w_v, [jnp.full((SC_LANES,), k * own + c * cc + t, jnp.int32)])
                      for k in range(TOP_K)]

                @plsc.parallel_loop(0, n_vec)
                def _(j):
                    lo = pl.ds(j * SC_LANES, SC_LANES)
                    hi = pl.ds(HALF + j * SC_LANES, SC_LANES)
                    acc_a = None
                    acc_b = None
                    for k in range(TOP_K):
                        p = rows[k, t, lo]
                        a = lax.bitcast_convert_type(p & HI_MASK, _f32) * ws[k]
                        b_ = lax.bitcast_convert_type(p << 16, _f32) * ws[k]
                        acc_a = a if acc_a is None else acc_a + a
                        acc_b = b_ if acc_b is None else acc_b + b_
                    xb[t, lo] = xb[t, lo] + g_v[lo] * acc_a
                    xb[t, hi] = xb[t, hi] + g_v[hi] * acc_b

        for cp in loads(0, 0):
            cp.start()

        @pl.loop(0, per_worker // 2)
        def _(c2):
            for b in range(2):
                c = 2 * c2 + b

                @pl.when(c >= 1)
                def _():
                    store(c - 1, 1 - b).wait()

                @pl.when(c + 1 < per_worker)
                def _():
                    for cp in loads(c + 1, 1 - b):
                        cp.start()

                for cp in loads(c, b):
                    cp.wait()
                combine(c, b)
                store(c, b).start()

        store(per_worker - 1, 1).wait()

    return run(y, dest, w, x1, g2)


def _final_kernel(x_ref, ng_ref, o_ref):
    o_ref[...] = _rms(x_ref[...], ng_ref[...])


def _final(x, final_g):
    t_tokens = x.shape[0]
    tm = TOKEN_TILE
    row = lambda i: (i, 0)
    return pl.pallas_call(
        _final_kernel,
        out_shape=jax.ShapeDtypeStruct((t_tokens, D_MODEL), _f32),
        grid=(t_tokens // tm,),
        in_specs=[pl.BlockSpec((tm, D_MODEL), row), pl.BlockSpec((1, D_MODEL), lambda i: (0, 0))],
        out_specs=pl.BlockSpec((tm, D_MODEL), row),
        compiler_params=pltpu.CompilerParams(dimension_semantics=("parallel",), vmem_limit_bytes=VMEM_LIMIT),
        name="final",
    )(x, final_g)


def _plan(route, counts, nblk):
    cnt = counts[:, 0]
    padded = ((cnt + EXPERT_BLOCK - 1) // EXPERT_BLOCK) * EXPERT_BLOCK
    pad_end = jnp.cumsum(padded)
    pad_start = pad_end - padded
    expert = jnp.arange(N_EXPERTS, dtype=jnp.int32)[:, None, None]
    start_of = jnp.sum(jnp.where(route[None, :TOP_K] == expert, pad_start[:, None, None], 0), axis=0)
    dest = start_of + route[TOP_K:2 * TOP_K]
    nact = jnp.maximum(pad_end[-1] // EXPERT_BLOCK, 1)
    step = jnp.arange(nblk, dtype=jnp.int32)
    blk = jnp.minimum(step, nact - 1)
    bexp = jnp.minimum(jnp.sum(pad_end[None, :] <= (blk * EXPERT_BLOCK)[:, None], axis=1), N_EXPERTS - 1)
    active = step < pad_end[-1] // EXPERT_BLOCK
    w = lax.bitcast_convert_type(route[2 * TOP_K:3 * TOP_K], _f32)
    return (dest.reshape(-1), w.reshape(-1), blk.astype(jnp.int32), bexp.astype(jnp.int32),
            active.astype(jnp.int32))


def _trunk(x_all, c_all, norm1_g, ada_w, ada_b, w_in, b_in, w_fmix, sinks, w_o, b_o,
           norm2_g, w_router, b_router, w_gu, b_gu, w_down, b_down, final_g):
    nb, seq, _ = x_all.shape
    t_tokens = nb * seq
    assert seq % TOKEN_TILE == 0 and seq % (ATTN_TILES * Q_TILE) == 0 and seq >= KEY_SPAN
    assert t_tokens % (SC_CHUNK * SC_WORKERS) == 0 and seq % COMBINE_CHUNK == 0
    nblk = (t_tokens * TOP_K) // EXPERT_BLOCK + N_EXPERTS
    p_rows = nblk * EXPERT_BLOCK

    mod = _modulation(c_all, ada_w, ada_b)
    mod = mod.reshape(DEPTH, nb, 6, 1, D_MODEL)
    rope_tabs = _rope_tables(seq)
    dft = _dft_tables(seq)
    tri = jnp.triu(jnp.ones((RANK_TILE, RANK_TILE), _f32), 1).astype(_bf16)

    x = x_all.reshape(t_tokens, D_MODEL)
    for l in range(DEPTH):
        sh1, sc1, g1, sh2, sc2, g2 = (mod[l, :, j] for j in range(6))
        ab = _channel_dft_fold(w_fmix[l], seq)
        y, q, k, v = _inproj(seq, x, sh1, sc1, norm1_g[l].reshape(1, D_MODEL), w_in[l].astype(_bf16),
                             b_in[l].reshape(1, IN_WIDTH), ab, rope_tabs)
        mf = _fourier(seq, y, dft)
        attn = _attention(seq, q, k, v, sinks[l])
        w_o_bf = w_o[l].astype(_bf16)
        w_r = w_router[l].T.astype(_bf16)
        b_r = jnp.broadcast_to(b_router[l][:, None], (N_EXPERTS, TOKEN_TILE))
        x1, hp, route, counts = _outproj(seq, x, mf, attn, g1, sh2, sc2, norm2_g[l].reshape(1, D_MODEL),
                                         w_o_bf[:FOURIER_WIDTH], w_o_bf[FOURIER_WIDTH:],
                                         b_o[l].reshape(1, D_MODEL), w_r, b_r, tri)
        dest, w, blk, bexp, active = _plan(route, counts, nblk)
        xs = _dispatch(hp, dest, p_rows)
        ys = _experts(l, xs, blk, bexp, active, w_gu, b_gu[l], w_down, b_down[l])
        x = _combine_rows(ys, dest, w, x1, g2.reshape(nb, D_MODEL), seq)
    out = _final(x, final_g.reshape(1, D_MODEL))
    return out.reshape(nb, seq, D_MODEL)


def kernel(x_prompt, x_sample, c_prompt, c_sample, norm1_g, ada_w, ada_b, w_in, b_in, w_fmix, sinks, w_o, b_o,
           norm2_g, w_router, b_router, w_gu, b_gu, w_down, b_down, final_g):
    ws = (norm1_g, ada_w, ada_b, w_in, b_in, w_fmix, sinks, w_o, b_o,
          norm2_g, w_router, b_router, w_gu, b_gu, w_down, b_down, final_g)
    return _trunk(x_prompt, c_prompt, *ws), _trunk(x_sample, c_sample, *ws)
```

```python
import functools

import jax
import jax.numpy as jnp
from jax import lax
from jax.experimental import pallas as pl
from jax.experimental.pallas import tpu as pltpu
from jax.experimental.pallas import tpu_sc as plsc

D_MODEL = 1024
DEPTH = 4
FOURIER_WIDTH = 256
N_FGROUPS = 4
FGROUP_DIM = 64
HEAD_DIM = 64
N_Q_HEADS = 12
N_KV_HEADS = 4
Q_PER_KV = 3
ATTN_WIDTH = N_Q_HEADS * HEAD_DIM
KV_WIDTH = N_KV_HEADS * HEAD_DIM
IN_WIDTH = FOURIER_WIDTH + ATTN_WIDTH + 2 * KV_WIDTH
WINDOW = 128
ROPE_THETA = 500000.0
ROT_DIM = 16
N_EXPERTS = 32
TOP_K = 4
D_FF = 512
EXPERT_BLOCK = 1024
EXPERT_SUB = 512
SWIGLU_LIMIT = 7.0
SWIGLU_ALPHA = 1.702
NORM_EPS = 1e-5

LANES = 128
HALF = D_MODEL // 2
TOKEN_TILE = 1024
RANK_TILE = 512
SUB_TILE = 256
Q_TILE = 256
ATTN_TILES = 4
KEY_SPAN = Q_TILE + 2 * WINDOW
SC_CHUNK = 128
SC_WORKERS = 32
SC_LANES = 16
COMBINE_CHUNK = 16
VMEM_LIMIT = 56 * 1024 * 1024
NEG_BIG = -1e30
LOG2E = 1.4426950408889634
Q_SCALE = HEAD_DIM ** -0.5 * LOG2E
HI_MASK = -65536
ROUTE_ROWS = 16

_f32 = jnp.float32
_bf16 = jnp.bfloat16


def _pack_halves(a, b):
    ua = lax.bitcast_convert_type(a.astype(_bf16).astype(_f32), jnp.int32)
    ub = lax.bitcast_convert_type(b.astype(_bf16).astype(_f32), jnp.int32)
    return ua | lax.shift_right_logical(ub, 16)


def _unpack_halves(p):
    a = lax.bitcast_convert_type(p & HI_MASK, _f32)
    b = lax.bitcast_convert_type(lax.shift_left(p, 16), _f32)
    return a, b


def _rms(x, g):
    ms = jnp.mean(x * x, axis=-1, keepdims=True)
    return x * lax.rsqrt(ms + NORM_EPS) * g


def _mod_kernel(c_ref, w_ref, b_ref, o_ref):
    c = c_ref[...]
    s = c * (1.0 / (1.0 + jnp.exp(-c)))
    o_ref[0] = jnp.dot(s.astype(_bf16), w_ref[0].astype(_bf16), preferred_element_type=_f32) + b_ref[0]


def _modulation(c_all, ada_w, ada_b):
    nb = c_all.shape[0]
    ncol = ada_w.shape[2] // D_MODEL
    return pl.pallas_call(
        _mod_kernel,
        out_shape=jax.ShapeDtypeStruct((DEPTH, nb, 6 * D_MODEL), _f32),
        grid=(DEPTH, ncol),
        in_specs=[
            pl.BlockSpec((nb, D_MODEL), lambda l, j: (0, 0)),
            pl.BlockSpec((1, D_MODEL, D_MODEL), lambda l, j: (l, 0, j)),
            pl.BlockSpec((1, 1, D_MODEL), lambda l, j: (l, 0, j)),
        ],
        out_specs=pl.BlockSpec((1, nb, D_MODEL), lambda l, j: (l, 0, j)),
        name="modulation",
    )(c_all, ada_w, ada_b.reshape(DEPTH, 1, 6 * D_MODEL))


def _inproj_matmul(x, sh_ref, sc_ref, ng_ref, w_ref, b_ref, *_):
    h = _rms(x, ng_ref[...]) * (1.0 + sc_ref[0]) + sh_ref[0]
    return jnp.dot(h.astype(_bf16), w_ref[...], preferred_element_type=_f32) + b_ref[...]


def _inproj_finish(rows, z, sh_ref, sc_ref, ng_ref, w_ref, b_ref, ab_ref, rc_ref, rs1_ref, rs2_ref,
                   y_ref, q_ref, k_ref, v_ref):
    f = z[:, :FOURIER_WIDTH]
    y_ref[rows, :] = jnp.dot(f.astype(_bf16), ab_ref[...], preferred_element_type=_f32).astype(_bf16)
    rc = rc_ref[rows, :]
    rs1 = rs1_ref[rows, :]
    rs2 = rs2_ref[rows, :]

    def rope(t):
        return t * rc + pltpu.roll(t, LANES - ROT_DIM // 2, 1) * rs1 + pltpu.roll(t, ROT_DIM // 2, 1) * rs2

    q0 = FOURIER_WIDTH
    for c in range(ATTN_WIDTH // LANES):
        t = z[:, q0 + c * LANES:q0 + (c + 1) * LANES]
        q_ref[rows, c * LANES:(c + 1) * LANES] = (rope(t) * Q_SCALE).astype(_bf16)
    k0 = q0 + ATTN_WIDTH
    for c in range(KV_WIDTH // LANES):
        t = z[:, k0 + c * LANES:k0 + (c + 1) * LANES]
        k_ref[rows, c * LANES:(c + 1) * LANES] = rope(t).astype(_bf16)
    v_ref[rows, :] = z[:, k0 + KV_WIDTH:].astype(_bf16)


def _sub_tiles(n_rows):
    return [pl.ds(r, SUB_TILE) for r in range(0, n_rows, SUB_TILE)]


def _inproj_kernel(x_ref, *refs):
    tiles = _sub_tiles(x_ref.shape[0])
    z = _inproj_matmul(x_ref[tiles[0], :], *refs)
    for n, rows in enumerate(tiles):
        z_next = _inproj_matmul(x_ref[tiles[n + 1], :], *refs) if n + 1 < len(tiles) else None
        _inproj_finish(rows, z, *refs)
        z = z_next


def _inproj(seq, x, sh1, sc1, norm_g, w_in, b_in, ab, rope_tabs):
    t_tokens = x.shape[0]
    tm = TOKEN_TILE
    tiles_per_seq = seq // tm
    row = lambda i: (i, 0)
    per_seq = lambda i: (i // tiles_per_seq, 0, 0)
    const = lambda i: (0, 0)
    pos = lambda i: (i % tiles_per_seq, 0)
    return pl.pallas_call(
        _inproj_kernel,
        out_shape=[jax.ShapeDtypeStruct((t_tokens, 2 * FOURIER_WIDTH), _bf16),
                   jax.ShapeDtypeStruct((t_tokens, ATTN_WIDTH), _bf16),
                   jax.ShapeDtypeStruct((t_tokens, KV_WIDTH), _bf16),
                   jax.ShapeDtypeStruct((t_tokens, KV_WIDTH), _bf16)],
        grid=(t_tokens // tm,),
        in_specs=[pl.BlockSpec((tm, D_MODEL), row),
                  pl.BlockSpec((1, 1, D_MODEL), per_seq), pl.BlockSpec((1, 1, D_MODEL), per_seq),
                  pl.BlockSpec((1, D_MODEL), const),
                  pl.BlockSpec((D_MODEL, IN_WIDTH), const), pl.BlockSpec((1, IN_WIDTH), const),
                  pl.BlockSpec((FOURIER_WIDTH, 2 * FOURIER_WIDTH), const),
                  pl.BlockSpec((tm, LANES), pos), pl.BlockSpec((tm, LANES), pos), pl.BlockSpec((tm, LANES), pos)],
        out_specs=[pl.BlockSpec((tm, 2 * FOURIER_WIDTH), row), pl.BlockSpec((tm, ATTN_WIDTH), row),
                   pl.BlockSpec((tm, KV_WIDTH), row), pl.BlockSpec((tm, KV_WIDTH), row)],
        compiler_params=pltpu.CompilerParams(dimension_semantics=("parallel",), vmem_limit_bytes=VMEM_LIMIT),
        name="inproj",
    )(x, sh1, sc1, norm_g, w_in, b_in, ab, *rope_tabs)


def _fourier_kernel(y_ref, *refs):
    tabs, o_ref = refs[:-1], refs[-1]
    quarter = y_ref.shape[0] // 4
    w = FOURIER_WIDTH
    a = [y_ref[q * quarter:(q + 1) * quarter, :w].astype(_f32) for q in range(4)]
    b = [y_ref[q * quarter:(q + 1) * quarter, w:].astype(_f32) for q in range(4)]
    a_s, a_d, a_s13, a_d13 = a[0] + a[2], a[0] - a[2], a[1] + a[3], a[1] - a[3]
    b_s, b_d, b_s13, b_d13 = b[0] + b[2], b[0] - b[2], b[1] + b[3], b[1] - b[3]
    w_re = [a_s + a_s13, a_d + b_d13, a_s - a_s13, a_d - b_d13]
    w_im = [b_s + b_s13, b_d - a_d13, b_s - b_s13, b_d + a_d13]
    for r in range(4):
        out = (jnp.dot(tabs[2 * r][...], w_re[r].astype(_bf16), preferred_element_type=_f32)
               + jnp.dot(tabs[2 * r + 1][...], w_im[r].astype(_bf16), preferred_element_type=_f32))
        for c in range(FOURIER_WIDTH // LANES):
            o_ref[c, pl.ds(r, quarter, stride=4), :] = out[:, c * LANES:(c + 1) * LANES]


def _fourier(seq, y, dft):
    t_tokens = y.shape[0]
    quarter = seq // 4
    const = lambda b: (0, 0)
    return pl.pallas_call(
        _fourier_kernel,
        out_shape=jax.ShapeDtypeStruct((FOURIER_WIDTH // LANES, t_tokens, LANES), _f32),
        grid=(t_tokens // seq,),
        in_specs=[pl.BlockSpec((seq, 2 * FOURIER_WIDTH), lambda b: (b, 0))]
        + [pl.BlockSpec((quarter, quarter), const)] * 8,
        out_specs=pl.BlockSpec((FOURIER_WIDTH // LANES, seq, LANES), lambda b: (0, b, 0)),
        compiler_params=pltpu.CompilerParams(dimension_semantics=("parallel",), vmem_limit_bytes=VMEM_LIMIT),
        name="fourier",
    )(y, *dft)


def _dft_tables(seq):
    quarter = seq // 4
    j = jnp.arange(quarter, dtype=jnp.int32)[:, None]
    m = jnp.arange(quarter, dtype=jnp.int32)[None, :]
    tabs = []
    for r in range(4):
        ang = (((4 * j + r) * m) % seq).astype(_f32) * (2.0 * jnp.pi / seq)
        tabs += [jnp.cos(ang).astype(_bf16), jnp.sin(ang).astype(_bf16)]
    return tuple(tabs)


def _channel_dft_fold(w_fmix, seq):
    c = jnp.arange(FGROUP_DIM, dtype=jnp.int32)
    ang = ((c[:, None] * c[None, :]) % FGROUP_DIM).astype(_f32) * (2.0 * jnp.pi / FGROUP_DIM)
    scale = (seq * FGROUP_DIM) ** -0.5
    hp = lax.Precision.HIGHEST
    a = jnp.einsum('cm,gmd->gcd', jnp.cos(ang) * scale, w_fmix, precision=hp)
    b = jnp.einsum('cm,gmd->gcd', -jnp.sin(ang) * scale, w_fmix, precision=hp)
    eye = jnp.eye(N_FGROUPS, dtype=_f32)
    bd = lambda m: jnp.einsum('gcd,gh->gchd', m, eye).reshape(FOURIER_WIDTH, FOURIER_WIDTH)
    return jnp.concatenate([bd(a), bd(b)], axis=1).astype(_bf16)


def _rope_tables(seq):
    half = ROT_DIM // 2
    inv_freq = jnp.power(ROPE_THETA, -jnp.arange(0, ROT_DIM, 2, dtype=_f32) / ROT_DIM)
    ang = jnp.arange(seq, dtype=_f32)[:, None] * inv_freq[None, :]
    cos, sin = jnp.cos(ang), jnp.sin(ang)
    ones = jnp.ones((seq, HEAD_DIM - ROT_DIM), _f32)
    zeros = jnp.zeros((seq, HEAD_DIM - ROT_DIM), _f32)
    zh = jnp.zeros((seq, half), _f32)
    c1 = jnp.concatenate([cos, cos, ones], axis=1)
    s1 = jnp.concatenate([-sin, zh, zeros], axis=1)
    s2 = jnp.concatenate([zh, sin, zeros], axis=1)
    rep = LANES // HEAD_DIM
    return tuple(jnp.tile(t, (1, rep)) for t in (c1, s1, s2))


def _attn_kernel(sink_ref, q_ref, k_ref, v_ref, o_ref, kpad, vtb, ot):
    seq = k_ref.shape[0]
    nqb = seq // Q_TILE
    nkb = seq // WINDOW

    @pl.when(pl.program_id(1) == 0)
    def _():
        zk = jnp.zeros((WINDOW, HEAD_DIM), _bf16)
        for h in range(N_KV_HEADS):
            kpad[h, :WINDOW, :] = zk
            kpad[h, WINDOW + seq:, :] = zk
            kpad[h, WINDOW:WINDOW + seq, :] = k_ref[:, h * HEAD_DIM:(h + 1) * HEAD_DIM]
        zv = jnp.zeros((KV_WIDTH, WINDOW), _bf16)
        vtb[0] = zv
        vtb[nkb + 1] = zv
        for j in range(nkb):
            vtb[j + 1] = v_ref[j * WINDOW:(j + 1) * WINDOW, :].astype(_f32).T.astype(_bf16)

    ka = lax.broadcasted_iota(jnp.int32, (WINDOW, WINDOW), 0)
    qc = lax.broadcasted_iota(jnp.int32, (WINDOW, WINDOW), 1)
    tri_ge = jnp.where(ka >= qc, 0.0, NEG_BIG).astype(_f32)
    tri_le = jnp.where(ka <= qc, 0.0, NEG_BIG).astype(_f32)
    zero_blk = jnp.zeros((WINDOW, WINDOW), _bf16)
    n_tiles = q_ref.shape[0] // Q_TILE
    qbs = [pl.program_id(1) * n_tiles + t for t in range(n_tiles)]
    bias_first = [jnp.where(qb == 0, NEG_BIG, tri_ge) for qb in qbs]
    bias_last = [jnp.where(qb == nqb - 1, NEG_BIG, tri_le) for qb in qbs]

    def softmax_col(blocks, sink):
        m = blocks[0]
        for b in blocks[1:]:
            m = jnp.maximum(m, b)
        m = jnp.maximum(jnp.max(m, axis=0, keepdims=True), sink)
        ps = [jnp.exp2(b - m) for b in blocks]
        tot = ps[0]
        for p in ps[1:]:
            tot = tot + p
        return ps, jnp.sum(tot, axis=0, keepdims=True) + jnp.exp2(sink - m)

    w = WINDOW

    def scores(t, h):
        r0 = pl.multiple_of(qbs[t] * Q_TILE, Q_TILE)
        kh = kpad[h, pl.ds(r0, KEY_SPAN), :]
        q3 = jnp.concatenate([q_ref[t * Q_TILE:(t + 1) * Q_TILE, g * HEAD_DIM:(g + 1) * HEAD_DIM]
                              for g in range(Q_PER_KV * h, Q_PER_KV * (h + 1))], axis=0)
        return lax.dot_general(kh, q3, (((1,), (1,)), ((), ())), preferred_element_type=_f32)

    work = [(t, h) for t in range(n_tiles) for h in range(N_KV_HEADS)]
    s = scores(*work[0])
    for n, (t, h) in enumerate(work):
        s_next = scores(*work[n + 1]) if n + 1 < len(work) else None
        cols, dens = [], []
        for i in range(Q_PER_KV):
            sink = sink_ref[Q_PER_KV * h + i] * LOG2E
            c0 = 2 * i * w
            c1 = c0 + w
            p0, l0 = softmax_col([s[0:w, c0:c1] + bias_first[t], s[w:2 * w, c0:c1], s[2 * w:3 * w, c0:c1] + tri_le],
                                 sink)
            p1, l1 = softmax_col([s[w:2 * w, c1:c1 + w] + tri_ge, s[2 * w:3 * w, c1:c1 + w],
                                  s[3 * w:, c1:c1 + w] + bias_last[t]], sink)
            cols.append(jnp.concatenate([p.astype(_bf16) for p in p0] + [zero_blk], axis=0))
            cols.append(jnp.concatenate([zero_blk] + [p.astype(_bf16) for p in p1], axis=0))
            dens += [l0, l1]
        pt = jnp.concatenate(cols, axis=1)
        kb0 = qbs[t] * (Q_TILE // WINDOW)
        vth = jnp.concatenate([vtb[kb0 + j, h * HEAD_DIM:(h + 1) * HEAD_DIM, :]
                               for j in range(KEY_SPAN // WINDOW)], axis=1)
        o_t = jnp.dot(vth, pt, preferred_element_type=_f32) / jnp.concatenate(dens, axis=1)
        for i in range(Q_PER_KV):
            g = Q_PER_KV * h + i
            ot[t, g * HEAD_DIM:(g + 1) * HEAD_DIM, :] = o_t[:, i * Q_TILE:(i + 1) * Q_TILE]
        if h == N_KV_HEADS - 1:
            for c in range(ATTN_WIDTH // LANES):
                o_ref[t * Q_TILE:(t + 1) * Q_TILE, c * LANES:(c + 1) * LANES] = (
                    ot[t, c * LANES:(c + 1) * LANES, :].T.astype(_bf16))
        s = s_next


def _attention(seq, q, k, v, sinks):
    t_tokens = q.shape[0]
    rows = ATTN_TILES * Q_TILE
    steps = seq // rows
    return pl.pallas_call(
        _attn_kernel,
        out_shape=jax.ShapeDtypeStruct((t_tokens, ATTN_WIDTH), _bf16),
        grid=(t_tokens // seq, steps),
        in_specs=[pl.BlockSpec(memory_space=pltpu.SMEM),
                  pl.BlockSpec((rows, ATTN_WIDTH), lambda b, j: (b * steps + j, 0)),
                  pl.BlockSpec((seq, KV_WIDTH), lambda b, j: (b, 0)),
                  pl.BlockSpec((seq, KV_WIDTH), lambda b, j: (b, 0))],
        out_specs=pl.BlockSpec((rows, ATTN_WIDTH), lambda b, j: (b * steps + j, 0)),
        scratch_shapes=[pltpu.VMEM((N_KV_HEADS, seq + 2 * WINDOW, HEAD_DIM), _bf16),
                        pltpu.VMEM((seq // WINDOW + 2, KV_WIDTH, WINDOW), _bf16),
                        pltpu.VMEM((ATTN_TILES, ATTN_WIDTH, Q_TILE), _f32)],
        compiler_params=pltpu.CompilerParams(dimension_semantics=("parallel", "arbitrary"),
                                             vmem_limit_bytes=VMEM_LIMIT),
        name="attention",
    )(sinks, q, k, v)


def _outproj_kernel(x_ref, mf_ref, at_ref, g1_ref, sh_ref, sc_ref, ng_ref, wof_ref, woa_ref, bo_ref,
                    wr_ref, br_ref, tri_ref, x1_ref, hp_ref, rt_ref, cnt_ref, carry_ref):
    i = pl.program_id(0)

    @pl.when(i == 0)
    def _():
        carry_ref[...] = jnp.zeros_like(carry_ref)

    def project(rows):
        mf = jnp.concatenate([mf_ref[c, rows, :] for c in range(FOURIER_WIDTH // LANES)], axis=1)
        return (jnp.dot(mf.astype(_bf16), wof_ref[...], preferred_element_type=_f32)
                + jnp.dot(at_ref[rows, :], woa_ref[...], preferred_element_type=_f32) + bo_ref[...])

    tiles = _sub_tiles(x_ref.shape[0])
    mix = project(tiles[0])
    logit_cols = []
    for n, rows in enumerate(tiles):
        mix_next = project(tiles[n + 1]) if n + 1 < len(tiles) else None
        x1 = x_ref[rows, :] + g1_ref[0] * mix
        x1_ref[rows, :] = x1
        h = _rms(x1, ng_ref[...]) * (1.0 + sc_ref[0]) + sh_ref[0]
        hp_ref[rows, :] = _pack_halves(h[:, :HALF], h[:, HALF:])
        logit_cols.append(lax.dot_general(wr_ref[...], h.astype(_bf16), (((1,), (1,)), ((), ())),
                                          preferred_element_type=_f32))
        mix = mix_next
    logits = jnp.concatenate(logit_cols, axis=1) + br_ref[...]
    tm = logits.shape[1]
    erow = lax.broadcasted_iota(jnp.int32, logits.shape, 0)
    work = logits
    hots, vals, idxs = [], [], []
    for _k in range(TOP_K):
        mx = jnp.max(work, axis=0, keepdims=True)
        ix = jnp.min(jnp.where(work == mx, erow, N_EXPERTS), axis=0, keepdims=True)
        hot = erow == ix
        work = jnp.where(hot, -jnp.inf, work)
        hots.append(hot)
        vals.append(mx)
        idxs.append(ix)
    es = [jnp.exp(v - vals[0]) for v in vals]
    den = es[0] + es[1] + es[2] + es[3]
    member = jnp.zeros(logits.shape, _f32)
    for hot in hots:
        member = member + hot.astype(_f32)
    carry = carry_ref[...]
    parts = []
    for c0 in range(0, tm, RANK_TILE):
        m = member[:, c0:c0 + RANK_TILE]
        parts.append(jnp.dot(m.astype(_bf16), tri_ref[...], preferred_element_type=_f32)
                     + jnp.concatenate([carry] * (RANK_TILE // LANES), axis=1))
        carry = carry + jnp.broadcast_to(jnp.sum(m, axis=1, keepdims=True), carry.shape)
    before = jnp.concatenate(parts, axis=1)
    r16 = lax.broadcasted_iota(jnp.int32, (ROUTE_ROWS, tm), 0)
    slab = jnp.zeros((ROUTE_ROWS, tm), jnp.int32)
    for k in range(TOP_K):
        rank = jnp.sum(jnp.where(hots[k], before, 0.0), axis=0, keepdims=True).astype(jnp.int32)
        slab = jnp.where(r16 == k, idxs[k], slab)
        slab = jnp.where(r16 == TOP_K + k, rank, slab)
        slab = jnp.where(r16 == 2 * TOP_K + k, lax.bitcast_convert_type(es[k] / den, jnp.int32), slab)
    rt_ref[...] = slab
    carry_ref[...] = carry
    cnt_ref[...] = carry.astype(jnp.int32)


def _outproj(seq, x, mf, attn, g1, sh2, sc2, norm_g, wo_f, wo_a, b_o, w_r, b_r, tri):
    t_tokens = x.shape[0]
    tm = TOKEN_TILE
    tiles_per_seq = seq // tm
    row = lambda i: (i, 0)
    per_seq = lambda i: (i // tiles_per_seq, 0, 0)
    const = lambda i: (0, 0)
    return pl.pallas_call(
        _outproj_kernel,
        out_shape=[jax.ShapeDtypeStruct((t_tokens, D_MODEL), _f32),
                   jax.ShapeDtypeStruct((t_tokens, HALF), jnp.int32),
                   jax.ShapeDtypeStruct((ROUTE_ROWS, t_tokens), jnp.int32),
                   jax.ShapeDtypeStruct((N_EXPERTS, LANES), jnp.int32)],
        grid=(t_tokens // tm,),
        in_specs=[pl.BlockSpec((tm, D_MODEL), row),
                  pl.BlockSpec((FOURIER_WIDTH // LANES, tm, LANES), lambda i: (0, i, 0)),
                  pl.BlockSpec((tm, ATTN_WIDTH), row),
                  pl.BlockSpec((1, 1, D_MODEL), per_seq), pl.BlockSpec((1, 1, D_MODEL), per_seq),
                  pl.BlockSpec((1, 1, D_MODEL), per_seq),
                  pl.BlockSpec((1, D_MODEL), const),
                  pl.BlockSpec((FOURIER_WIDTH, D_MODEL), const), pl.BlockSpec((ATTN_WIDTH, D_MODEL), const),
                  pl.BlockSpec((1, D_MODEL), const),
                  pl.BlockSpec((N_EXPERTS, D_MODEL), const), pl.BlockSpec((N_EXPERTS, tm), const),
                  pl.BlockSpec((RANK_TILE, RANK_TILE), const)],
        out_specs=[pl.BlockSpec((tm, D_MODEL), row), pl.BlockSpec((tm, HALF), row),
                   pl.BlockSpec((ROUTE_ROWS, tm), lambda i: (0, i)),
                   pl.BlockSpec((N_EXPERTS, LANES), const)],
        scratch_shapes=[pltpu.VMEM((N_EXPERTS, LANES), _f32)],
        compiler_params=pltpu.CompilerParams(dimension_semantics=("arbitrary",), vmem_limit_bytes=VMEM_LIMIT),
        name="outproj",
    )(x, mf, attn, g1, sh2, sc2, norm_g, wo_f, wo_a, b_o, w_r, b_r, tri)


def _expert_kernel(blk_ref, bexp_ref, active_ref, x_ref, wgu_ref, bgu_ref, wd_ref, bd_ref, y_ref, wgu_bf, wd_bf):
    i = pl.program_id(0)
    prev = bexp_ref[jnp.maximum(i - 1, 0)]

    @pl.when((i == 0) | (bexp_ref[i] != prev))
    def _():
        wgu_bf[...] = wgu_ref[...].astype(_bf16)
        wd_bf[...] = wd_ref[...].astype(_bf16)

    def gate_up(rows):
        xa, xb = _unpack_halves(x_ref[rows, :])
        return (jnp.dot(xa.astype(_bf16), wgu_bf[:HALF, :], preferred_element_type=_f32)
                + jnp.dot(xb.astype(_bf16), wgu_bf[HALF:, :], preferred_element_type=_f32) + bgu_ref[...])

    def activate(gu):
        g = jnp.minimum(gu[:, :D_FF], SWIGLU_LIMIT)
        u = jnp.clip(gu[:, D_FF:], -SWIGLU_LIMIT, SWIGLU_LIMIT)
        return ((u + 1.0) * (g * (1.0 / (1.0 + jnp.exp(-SWIGLU_ALPHA * g))))).astype(_bf16)

    def down(rows, act):
        y = jnp.dot(act, wd_bf[...], preferred_element_type=_f32) + bd_ref[...]
        y_ref[rows, :] = _pack_halves(y[:, :HALF], y[:, HALF:])

    @pl.when(active_ref[i] > 0)
    def _():
        tiles = [pl.ds(r, EXPERT_SUB) for r in range(0, EXPERT_BLOCK, EXPERT_SUB)]
        gu = gate_up(tiles[0])
        for n, rows in enumerate(tiles):
            gu_next = gate_up(tiles[n + 1]) if n + 1 < len(tiles) else None
            down(rows, activate(gu))
            gu = gu_next


def _experts(layer, xs, blk, bexp, active, w_gu, b_gu, w_down, b_down):
    p_rows = xs.shape[0]
    nblk = p_rows // EXPERT_BLOCK
    grid_spec = pltpu.PrefetchScalarGridSpec(
        num_scalar_prefetch=3,
        grid=(nblk,),
        in_specs=[pl.BlockSpec((EXPERT_BLOCK, HALF), lambda i, blk, be, na: (blk[i], 0)),
                  pl.BlockSpec((None, None, D_MODEL, 2 * D_FF), lambda i, blk, be, na: (layer, be[i], 0, 0)),
                  pl.BlockSpec((None, 1, 2 * D_FF), lambda i, blk, be, na: (be[i], 0, 0)),
                  pl.BlockSpec((None, None, D_FF, D_MODEL), lambda i, blk, be, na: (layer, be[i], 0, 0)),
                  pl.BlockSpec((None, 1, D_MODEL), lambda i, blk, be, na: (be[i], 0, 0))],
        out_specs=pl.BlockSpec((EXPERT_BLOCK, HALF), lambda i, blk, be, na: (blk[i], 0)),
        scratch_shapes=[pltpu.VMEM((D_MODEL, 2 * D_FF), _bf16), pltpu.VMEM((D_FF, D_MODEL), _bf16)])
    return pl.pallas_call(
        _expert_kernel,
        out_shape=jax.ShapeDtypeStruct((p_rows, HALF), jnp.int32),
        grid_spec=grid_spec,
        compiler_params=pltpu.CompilerParams(dimension_semantics=("arbitrary",), vmem_limit_bytes=VMEM_LIMIT),
        name="experts",
    )(blk, bexp, active, xs, w_gu, b_gu.reshape(N_EXPERTS, 1, 2 * D_FF), w_down,
      b_down.reshape(N_EXPERTS, 1, D_MODEL))


def _sc_mesh():
    return plsc.VectorSubcoreMesh(core_axis_name="c", subcore_axis_name="s")


def _sc_worker():
    return lax.axis_index("s") * 2 + lax.axis_index("c")


def _dispatch(hp, dest, p_rows):
    t_tokens = hp.shape[0]
    per_worker = t_tokens // (SC_CHUNK * SC_WORKERS)

    @functools.partial(
        pl.kernel, mesh=_sc_mesh(),
        out_type=jax.ShapeDtypeStruct((p_rows, HALF), jnp.int32),
        scratch_types=[pltpu.VMEM((TOP_K, SC_CHUNK), jnp.int32),
                       pltpu.VMEM((SC_CHUNK, HALF), jnp.int32),
                       pltpu.SemaphoreType.DMA],
        name="dispatch")
    def run(hp_hbm, dest_hbm, xs_hbm, idx_v, rows_v, sem):
        base = _sc_worker() * per_worker

        @pl.loop(0, per_worker)
        def _(j):
            chunk = base + j
            for k in range(TOP_K):
                pltpu.sync_copy(dest_hbm.at[pl.ds(k * t_tokens + chunk * SC_CHUNK, SC_CHUNK)], idx_v.at[k])
            pltpu.sync_copy(hp_hbm.at[pl.ds(chunk * SC_CHUNK, SC_CHUNK)], rows_v)
            copies = [pltpu.async_copy(rows_v, xs_hbm.at[idx_v.at[k]], sem) for k in range(TOP_K)]
            for cp in copies:
                cp.wait()

    return run(hp, dest)


def _combine_rows(y, dest, w, x1, g2, seq):
    t_tokens = x1.shape[0]
    cc = COMBINE_CHUNK
    n_chunks = t_tokens // cc
    per_worker = n_chunks // SC_WORKERS
    own = per_worker * cc
    assert per_worker % 2 == 0 and seq % own == 0
    n_vec = HALF // SC_LANES
    row_buf = pltpu.VMEM((TOP_K, cc, HALF), jnp.int32)
    x_buf = pltpu.VMEM((cc, D_MODEL), _f32)

    @functools.partial(
        pl.kernel, mesh=_sc_mesh(),
        out_type=jax.ShapeDtypeStruct((t_tokens, D_MODEL), _f32),
        scratch_types=[pltpu.VMEM((TOP_K * own,), jnp.int32),
                       pltpu.VMEM((TOP_K * own,), _f32),
                       row_buf, row_buf, x_buf, x_buf,
                       pltpu.VMEM((D_MODEL,), _f32),
                       pltpu.SemaphoreType.DMA, pltpu.SemaphoreType.DMA,
                       pltpu.SemaphoreType.DMA, pltpu.SemaphoreType.DMA],
        compiler_params=pltpu.CompilerParams(needs_layout_passes=False),
        name="combine_rows")
    def run(y_hbm, dest_hbm, w_hbm, x1_hbm, g2_hbm, x2_hbm,
            idx_v, w_v, rows0, rows1, xb0, xb1, g_v, in0, in1, out0, out1):
        base = _sc_worker() * per_worker
        bufs = ((rows0, xb0, in0, out0), (rows1, xb1, in1, out1))
        for k in range(TOP_K):
            pltpu.sync_copy(dest_hbm.at[pl.ds(k * t_tokens + base * cc, own)], idx_v.at[pl.ds(k * own, own)])
            pltpu.sync_copy(w_hbm.at[pl.ds(k * t_tokens + base * cc, own)], w_v.at[pl.ds(k * own, own)])
        pltpu.sync_copy(g2_hbm.at[(base * cc) // seq], g_v)

        def loads(c, b):
            rows, xb, isem, _ = bufs[b]
            cps = [pltpu.make_async_copy(y_hbm.at[idx_v.at[pl.ds(k * own + c * cc, cc)]], rows.at[k], isem)
                   for k in range(TOP_K)]
            cps.append(pltpu.make_async_copy(x1_hbm.at[pl.ds((base + c) * cc, cc)], xb, isem))
            return cps

        def store(c, b):
            _, xb, _, osem = bufs[b]
            return pltpu.make_async_copy(xb, x2_hbm.at[pl.ds((base + c) * cc, cc)], osem)

        def combine(c, b):
            rows, xb, _, _ = bufs[b]

            @pl.loop(0, cc)
            def _(t):
                ws = [plsc.load_gather(w_v, [jnp.full((SC_LANES,), k * own + c * cc + t, jnp.int32)])
                      for k in range(TOP_K)]

                @plsc.parallel_loop(0, n_vec)
                def _(j):
                    lo = pl.ds(j * SC_LANES, SC_LANES)
                    hi = pl.ds(HALF + j * SC_LANES, SC_LANES)
                    acc_a = None
                    acc_b = None
                    for k in range(TOP_K):
                        p = rows[k, t, lo]
                        a = lax.bitcast_convert_type(p & HI_MASK, _f32) * ws[k]
                        b_ = lax.bitcast_convert_type(p << 16, _f32) * ws[k]
                        acc_a = a if acc_a is None else acc_a + a
                        acc_b = b_ if acc_b is None else acc_b + b_
                    xb[t, lo] = xb[t, lo] + g_v[lo] * acc_a
                    xb[t, hi] = xb[t, hi] + g_v[hi] * acc_b

        for cp in loads(0, 0):
            cp.start()

        @pl.loop(0, per_worker // 2)
        def _(c2):
            for b in range(2):
                c = 2 * c2 + b

                @pl.when(c >= 1)
                def _():
                    store(c - 1, 1 - b).wait()

                @pl.when(c + 1 < per_worker)
                def _():
                    for cp in loads(c + 1, 1 - b):
                        cp.start()

                for cp in loads(c, b):
                    cp.wait()
                combine(c, b)
                store(c, b).start()

        store(per_worker - 1, 1).wait()

    return run(y, dest, w, x1, g2)


def _final_kernel(x_ref, ng_ref, o_ref):
    o_ref[...] = _rms(x_ref[...], ng_ref[...])


def _final(x, final_g):
    t_tokens = x.shape[0]
    tm = TOKEN_TILE
    row = lambda i: (i, 0)
    return pl.pallas_call(
        _final_kernel,
        out_shape=jax.ShapeDtypeStruct((t_tokens, D_MODEL), _f32),
        grid=(t_tokens // tm,),
        in_specs=[pl.BlockSpec((tm, D_MODEL), row), pl.BlockSpec((1, D_MODEL), lambda i: (0, 0))],
        out_specs=pl.BlockSpec((tm, D_MODEL), row),
        compiler_params=pltpu.CompilerParams(dimension_semantics=("parallel",), vmem_limit_bytes=VMEM_LIMIT),
        name="final",
    )(x, final_g)


def _plan(route, counts, nblk):
    cnt = counts[:, 0]
    padded = ((cnt + EXPERT_BLOCK - 1) // EXPERT_BLOCK) * EXPERT_BLOCK
    pad_end = jnp.cumsum(padded)
    pad_start = pad_end - padded
    expert = jnp.arange(N_EXPERTS, dtype=jnp.int32)[:, None, None]
    start_of = jnp.sum(jnp.where(route[None, :TOP_K] == expert, pad_start[:, None, None], 0), axis=0)
    dest = start_of + route[TOP_K:2 * TOP_K]
    nact = jnp.maximum(pad_end[-1] // EXPERT_BLOCK, 1)
    step = jnp.arange(nblk, dtype=jnp.int32)
    blk = jnp.minimum(step, nact - 1)
    bexp = jnp.minimum(jnp.sum(pad_end[None, :] <= (blk * EXPERT_BLOCK)[:, None], axis=1), N_EXPERTS - 1)
    active = step < pad_end[-1] // EXPERT_BLOCK
    w = lax.bitcast_convert_type(route[2 * TOP_K:3 * TOP_K], _f32)
    return (dest.reshape(-1), w.reshape(-1), blk.astype(jnp.int32), bexp.astype(jnp.int32),
            active.astype(jnp.int32))


def _trunk(x_all, c_all, norm1_g, ada_w, ada_b, w_in, b_in, w_fmix, sinks, w_o, b_o,
           norm2_g, w_router, b_router, w_gu, b_gu, w_down, b_down, final_g):
    nb, seq, _ = x_all.shape
    t_tokens = nb * seq
    assert seq % TOKEN_TILE == 0 and seq % (ATTN_TILES * Q_TILE) == 0 and seq >= KEY_SPAN
    assert t_tokens % (SC_CHUNK * SC_WORKERS) == 0 and seq % COMBINE_CHUNK == 0
    nblk = (t_tokens * TOP_K) // EXPERT_BLOCK + N_EXPERTS
    p_rows = nblk * EXPERT_BLOCK

    mod = _modulation(c_all, ada_w, ada_b)
    mod = mod.reshape(DEPTH, nb, 6, 1, D_MODEL)
    rope_tabs = _rope_tables(seq)
    dft = _dft_tables(seq)
    tri = jnp.triu(jnp.ones((RANK_TILE, RANK_TILE), _f32), 1).astype(_bf16)

    x = x_all.reshape(t_tokens, D_MODEL)
    for l in range(DEPTH):
        sh1, sc1, g1, sh2, sc2, g2 = (mod[l, :, j] for j in range(6))
        ab = _channel_dft_fold(w_fmix[l], seq)
        y, q, k, v = _inproj(seq, x, sh1, sc1, norm1_g[l].reshape(1, D_MODEL), w_in[l].astype(_bf16),
                             b_in[l].reshape(1, IN_WIDTH), ab, rope_tabs)
        mf = _fourier(seq, y, dft)
        attn = _attention(seq, q, k, v, sinks[l])
        w_o_bf = w_o[l].astype(_bf16)
        w_r = w_router[l].T.astype(_bf16)
        b_r = jnp.broadcast_to(b_router[l][:, None], (N_EXPERTS, TOKEN_TILE))
        x1, hp, route, counts = _outproj(seq, x, mf, attn, g1, sh2, sc2, norm2_g[l].reshape(1, D_MODEL),
                                         w_o_bf[:FOURIER_WIDTH], w_o_bf[FOURIER_WIDTH:],
                                         b_o[l].reshape(1, D_MODEL), w_r, b_r, tri)
        dest, w, blk, bexp, active = _plan(route, counts, nblk)
        xs = _dispatch(hp, dest, p_rows)
        ys = _experts(l, xs, blk, bexp, active, w_gu, b_gu[l], w_down, b_down[l])
        x = _combine_rows(ys, dest, w, x1, g2.reshape(nb, D_MODEL), seq)
    out = _final(x, final_g.reshape(1, D_MODEL))
    return out.reshape(nb, seq, D_MODEL)


def kernel(x_prompt, x_sample, c_prompt, c_sample, norm1_g, ada_w, ada_b, w_in, b_in, w_fmix, sinks, w_o, b_o,
           norm2_g, w_router, b_router, w_gu, b_gu, w_down, b_down, final_g):
    ws = (norm1_g, ada_w, ada_b, w_in, b_in, w_fmix, sinks, w_o, b_o,
          norm2_g, w_router, b_router, w_gu, b_gu, w_down, b_down, final_g)
    return _trunk(x_prompt, c_prompt, *ws), _trunk(x_sample, c_sample, *ws)
```

```python
import functools

import jax
import jax.numpy as jnp
from jax import lax
from jax.experimental import pallas as pl
from jax.experimental.pallas import tpu as pltpu
from jax.experimental.pallas import tpu_sc as plsc

D_MODEL = 1024
DEPTH = 4
FOURIER_WIDTH = 256
N_FGROUPS = 4
FGROUP_DIM = 64
HEAD_DIM = 64
N_Q_HEADS = 12
N_KV_HEADS = 4
Q_PER_KV = 3
ATTN_WIDTH = N_Q_HEADS * HEAD_DIM
KV_WIDTH = N_KV_HEADS * HEAD_DIM
IN_WIDTH = FOURIER_WIDTH + ATTN_WIDTH + 2 * KV_WIDTH
WINDOW = 128
ROPE_THETA = 500000.0
ROT_DIM = 16
N_EXPERTS = 32
TOP_K = 4
D_FF = 512
EXPERT_BLOCK = 2048
EXPERT_UNIT = 1024
EXPERT_SUB = 512
SWIGLU_LIMIT = 7.0
SWIGLU_ALPHA = 1.702
NORM_EPS = 1e-5

LANES = 128
HALF = D_MODEL // 2
TOKEN_TILE = 1024
RANK_TILE = 512
SUB_TILE = 256
Q_TILE = 256
ATTN_TILES = 4
KEY_SPAN = Q_TILE + 2 * WINDOW
SC_CHUNK = 128
SC_WORKERS = 32
SC_LANES = 16
COMBINE_CHUNK = 16
VMEM_LIMIT = 56 * 1024 * 1024
NEG_BIG = -1e30
LOG2E = 1.4426950408889634
Q_SCALE = HEAD_DIM ** -0.5 * LOG2E
HI_MASK = -65536
ROUTE_ROWS = 16

_f32 = jnp.float32
_bf16 = jnp.bfloat16


def _pack_halves(a, b):
    ua = lax.bitcast_convert_type(a.astype(_bf16).astype(_f32), jnp.int32)
    ub = lax.bitcast_convert_type(b.astype(_bf16).astype(_f32), jnp.int32)
    return ua | lax.shift_right_logical(ub, 16)


def _unpack_halves(p):
    a = lax.bitcast_convert_type(p & HI_MASK, _f32)
    b = lax.bitcast_convert_type(lax.shift_left(p, 16), _f32)
    return a, b


def _rms(x, g):
    ms = jnp.mean(x * x, axis=-1, keepdims=True)
    return x * lax.rsqrt(ms + NORM_EPS) * g


def _mod_kernel(c_ref, w_ref, b_ref, o_ref):
    c = c_ref[...]
    s = c * (1.0 / (1.0 + jnp.exp(-c)))
    o_ref[0] = jnp.dot(s.astype(_bf16), w_ref[0].astype(_bf16), preferred_element_type=_f32) + b_ref[0]


def _modulation(c_all, ada_w, ada_b):
    nb = c_all.shape[0]
    ncol = ada_w.shape[2] // D_MODEL
    return pl.pallas_call(
        _mod_kernel,
        out_shape=jax.ShapeDtypeStruct((DEPTH, nb, 6 * D_MODEL), _f32),
        grid=(DEPTH, ncol),
        in_specs=[
            pl.BlockSpec((nb, D_MODEL), lambda l, j: (0, 0)),
            pl.BlockSpec((1, D_MODEL, D_MODEL), lambda l, j: (l, 0, j)),
            pl.BlockSpec((1, 1, D_MODEL), lambda l, j: (l, 0, j)),
        ],
        out_specs=pl.BlockSpec((1, nb, D_MODEL), lambda l, j: (l, 0, j)),
        name="modulation",
    )(c_all, ada_w, ada_b.reshape(DEPTH, 1, 6 * D_MODEL))


def _inproj_matmul(x, sh_ref, sc_ref, ng_ref, w_ref, b_ref, *_):
    h = _rms(x, ng_ref[...]) * (1.0 + sc_ref[0]) + sh_ref[0]
    return jnp.dot(h.astype(_bf16), w_ref[...], preferred_element_type=_f32) + b_ref[...]


def _inproj_finish(rows, z, sh_ref, sc_ref, ng_ref, w_ref, b_ref, ab_ref, rc_ref, rs1_ref, rs2_ref,
                   y_ref, q_ref, k_ref, v_ref):
    f = z[:, :FOURIER_WIDTH]
    y_ref[rows, :] = jnp.dot(f.astype(_bf16), ab_ref[...], preferred_element_type=_f32).astype(_bf16)
    rc = rc_ref[rows, :]
    rs1 = rs1_ref[rows, :]
    rs2 = rs2_ref[rows, :]

    def rope(t):
        return t * rc + pltpu.roll(t, LANES - ROT_DIM // 2, 1) * rs1 + pltpu.roll(t, ROT_DIM // 2, 1) * rs2

    q0 = FOURIER_WIDTH
    for c in range(ATTN_WIDTH // LANES):
        t = z[:, q0 + c * LANES:q0 + (c + 1) * LANES]
        q_ref[rows, c * LANES:(c + 1) * LANES] = (rope(t) * Q_SCALE).astype(_bf16)
    k0 = q0 + ATTN_WIDTH
    for c in range(KV_WIDTH // LANES):
        t = z[:, k0 + c * LANES:k0 + (c + 1) * LANES]
        k_ref[rows, c * LANES:(c + 1) * LANES] = rope(t).astype(_bf16)
    v_ref[rows, :] = z[:, k0 + KV_WIDTH:].astype(_bf16)


def _sub_tiles(n_rows):
    return [pl.ds(r, SUB_TILE) for r in range(0, n_rows, SUB_TILE)]


def _inproj_kernel(x_ref, *refs):
    tiles = _sub_tiles(x_ref.shape[0])
    z = _inproj_matmul(x_ref[tiles[0], :], *refs)
    for n, rows in enumerate(tiles):
        z_next = _inproj_matmul(x_ref[tiles[n + 1], :], *refs) if n + 1 < len(tiles) else None
        _inproj_finish(rows, z, *refs)
        z = z_next


def _inproj(seq, x, sh1, sc1, norm_g, w_in, b_in, ab, rope_tabs):
    t_tokens = x.shape[0]
    tm = TOKEN_TILE
    tiles_per_seq = seq // tm
    row = lambda i: (i, 0)
    per_seq = lambda i: (i // tiles_per_seq, 0, 0)
    const = lambda i: (0, 0)
    pos = lambda i: (i % tiles_per_seq, 0)
    return pl.pallas_call(
        _inproj_kernel,
        out_shape=[jax.ShapeDtypeStruct((t_tokens, 2 * FOURIER_WIDTH), _bf16),
                   jax.ShapeDtypeStruct((t_tokens, ATTN_WIDTH), _bf16),
                   jax.ShapeDtypeStruct((t_tokens, KV_WIDTH), _bf16),
                   jax.ShapeDtypeStruct((t_tokens, KV_WIDTH), _bf16)],
        grid=(t_tokens // tm,),
        in_specs=[pl.BlockSpec((tm, D_MODEL), row),
                  pl.BlockSpec((1, 1, D_MODEL), per_seq), pl.BlockSpec((1, 1, D_MODEL), per_seq),
                  pl.BlockSpec((1, D_MODEL), const),
                  pl.BlockSpec((D_MODEL, IN_WIDTH), const), pl.BlockSpec((1, IN_WIDTH), const),
                  pl.BlockSpec((FOURIER_WIDTH, 2 * FOURIER_WIDTH), const),
                  pl.BlockSpec((tm, LANES), pos), pl.BlockSpec((tm, LANES), pos), pl.BlockSpec((tm, LANES), pos)],
        out_specs=[pl.BlockSpec((tm, 2 * FOURIER_WIDTH), row), pl.BlockSpec((tm, ATTN_WIDTH), row),
                   pl.BlockSpec((tm, KV_WIDTH), row), pl.BlockSpec((tm, KV_WIDTH), row)],
        compiler_params=pltpu.CompilerParams(dimension_semantics=("parallel",), vmem_limit_bytes=VMEM_LIMIT),
        name="inproj",
    )(x, sh1, sc1, norm_g, w_in, b_in, ab, *rope_tabs)


def _fourier_kernel(y_ref, *refs):
    tabs, o_ref = refs[:-1], refs[-1]
    quarter = y_ref.shape[0] // 4
    w = FOURIER_WIDTH
    a = [y_ref[q * quarter:(q + 1) * quarter, :w].astype(_f32) for q in range(4)]
    b = [y_ref[q * quarter:(q + 1) * quarter, w:].astype(_f32) for q in range(4)]
    a_s, a_d, a_s13, a_d13 = a[0] + a[2], a[0] - a[2], a[1] + a[3], a[1] - a[3]
    b_s, b_d, b_s13, b_d13 = b[0] + b[2], b[0] - b[2], b[1] + b[3], b[1] - b[3]
    w_re = [a_s + a_s13, a_d + b_d13, a_s - a_s13, a_d - b_d13]
    w_im = [b_s + b_s13, b_d - a_d13, b_s - b_s13, b_d + a_d13]
    for r in range(4):
        out = (jnp.dot(tabs[2 * r][...], w_re[r].astype(_bf16), preferred_element_type=_f32)
               + jnp.dot(tabs[2 * r + 1][...], w_im[r].astype(_bf16), preferred_element_type=_f32))
        for c in range(FOURIER_WIDTH // LANES):
            o_ref[c, pl.ds(r, quarter, stride=4), :] = out[:, c * LANES:(c + 1) * LANES]


def _fourier(seq, y, dft):
    t_tokens = y.shape[0]
    quarter = seq // 4
    const = lambda b: (0, 0)
    return pl.pallas_call(
        _fourier_kernel,
        out_shape=jax.ShapeDtypeStruct((FOURIER_WIDTH // LANES, t_tokens, LANES), _f32),
        grid=(t_tokens // seq,),
        in_specs=[pl.BlockSpec((seq, 2 * FOURIER_WIDTH), lambda b: (b, 0))]
        + [pl.BlockSpec((quarter, quarter), const)] * 8,
        out_specs=pl.BlockSpec((FOURIER_WIDTH // LANES, seq, LANES), lambda b: (0, b, 0)),
        compiler_params=pltpu.CompilerParams(dimension_semantics=("parallel",), vmem_limit_bytes=VMEM_LIMIT),
        name="fourier",
    )(y, *dft)


def _dft_tables(seq):
    quarter = seq // 4
    j = jnp.arange(quarter, dtype=jnp.int32)[:, None]
    m = jnp.arange(quarter, dtype=jnp.int32)[None, :]
    tabs = []
    for r in range(4):
        ang = (((4 * j + r) * m) % seq).astype(_f32) * (2.0 * jnp.pi / seq)
        tabs += [jnp.cos(ang).astype(_bf16), jnp.sin(ang).astype(_bf16)]
    return tuple(tabs)


def _channel_dft_fold(w_fmix, seq):
    c = jnp.arange(FGROUP_DIM, dtype=jnp.int32)
    ang = ((c[:, None] * c[None, :]) % FGROUP_DIM).astype(_f32) * (2.0 * jnp.pi / FGROUP_DIM)
    scale = (seq * FGROUP_DIM) ** -0.5
    hp = lax.Precision.HIGHEST
    a = jnp.einsum('cm,gmd->gcd', jnp.cos(ang) * scale, w_fmix, precision=hp)
    b = jnp.einsum('cm,gmd->gcd', -jnp.sin(ang) * scale, w_fmix, precision=hp)
    eye = jnp.eye(N_FGROUPS, dtype=_f32)
    bd = lambda m: jnp.einsum('gcd,gh->gchd', m, eye).reshape(FOURIER_WIDTH, FOURIER_WIDTH)
    return jnp.concatenate([bd(a), bd(b)], axis=1).astype(_bf16)


def _rope_tables(seq):
    half = ROT_DIM // 2
    inv_freq = jnp.power(ROPE_THETA, -jnp.arange(0, ROT_DIM, 2, dtype=_f32) / ROT_DIM)
    ang = jnp.arange(seq, dtype=_f32)[:, None] * inv_freq[None, :]
    cos, sin = jnp.cos(ang), jnp.sin(ang)
    ones = jnp.ones((seq, HEAD_DIM - ROT_DIM), _f32)
    zeros = jnp.zeros((seq, HEAD_DIM - ROT_DIM), _f32)
    zh = jnp.zeros((seq, half), _f32)
    c1 = jnp.concatenate([cos, cos, ones], axis=1)
    s1 = jnp.concatenate([-sin, zh, zeros], axis=1)
    s2 = jnp.concatenate([zh, sin, zeros], axis=1)
    rep = LANES // HEAD_DIM
    return tuple(jnp.tile(t, (1, rep)) for t in (c1, s1, s2))


def _attn_kernel(sink_ref, q_ref, k_ref, v_ref, o_ref, kpad, vtb, ot):
    seq = k_ref.shape[0]
    nqb = seq // Q_TILE
    nkb = seq // WINDOW

    @pl.when(pl.program_id(1) == 0)
    def _():
        zk = jnp.zeros((WINDOW, HEAD_DIM), _bf16)
        for h in range(N_KV_HEADS):
            kpad[h, :WINDOW, :] = zk
            kpad[h, WINDOW + seq:, :] = zk
            kpad[h, WINDOW:WINDOW + seq, :] = k_ref[:, h * HEAD_DIM:(h + 1) * HEAD_DIM]
        zv = jnp.zeros((KV_WIDTH, WINDOW), _bf16)
        vtb[0] = zv
        vtb[nkb + 1] = zv
        for j in range(nkb):
            vtb[j + 1] = v_ref[j * WINDOW:(j + 1) * WINDOW, :].astype(_f32).T.astype(_bf16)

    ka = lax.broadcasted_iota(jnp.int32, (WINDOW, WINDOW), 0)
    qc = lax.broadcasted_iota(jnp.int32, (WINDOW, WINDOW), 1)
    tri_ge = jnp.where(ka >= qc, 0.0, NEG_BIG).astype(_f32)
    tri_le = jnp.where(ka <= qc, 0.0, NEG_BIG).astype(_f32)
    zero_blk = jnp.zeros((WINDOW, WINDOW), _bf16)
    n_tiles = q_ref.shape[0] // Q_TILE
    qbs = [pl.program_id(1) * n_tiles + t for t in range(n_tiles)]
    bias_first = [jnp.where(qb == 0, NEG_BIG, tri_ge) for qb in qbs]
    bias_last = [jnp.where(qb == nqb - 1, NEG_BIG, tri_le) for qb in qbs]

    def softmax_col(blocks, sink):
        m = blocks[0]
        for b in blocks[1:]:
            m = jnp.maximum(m, b)
        m = jnp.maximum(jnp.max(m, axis=0, keepdims=True), sink)
        ps = [jnp.exp2(b - m) for b in blocks]
        tot = ps[0]
        for p in ps[1:]:
            tot = tot + p
        return ps, jnp.sum(tot, axis=0, keepdims=True) + jnp.exp2(sink - m)

    w = WINDOW

    def scores(t, h):
        r0 = pl.multiple_of(qbs[t] * Q_TILE, Q_TILE)
        kh = kpad[h, pl.ds(r0, KEY_SPAN), :]
        q3 = jnp.concatenate([q_ref[t * Q_TILE:(t + 1) * Q_TILE, g * HEAD_DIM:(g + 1) * HEAD_DIM]
                              for g in range(Q_PER_KV * h, Q_PER_KV * (h + 1))], axis=0)
        return lax.dot_general(kh, q3, (((1,), (1,)), ((), ())), preferred_element_type=_f32)

    work = [(t, h) for t in range(n_tiles) for h in range(N_KV_HEADS)]
    s = scores(*work[0])
    for n, (t, h) in enumerate(work):
        s_next = scores(*work[n + 1]) if n + 1 < len(work) else None
        cols, dens = [], []
        for i in range(Q_PER_KV):
            sink = sink_ref[Q_PER_KV * h + i] * LOG2E
            c0 = 2 * i * w
            c1 = c0 + w
            p0, l0 = softmax_col([s[0:w, c0:c1] + bias_first[t], s[w:2 * w, c0:c1], s[2 * w:3 * w, c0:c1] + tri_le],
                                 sink)
            p1, l1 = softmax_col([s[w:2 * w, c1:c1 + w] + tri_ge, s[2 * w:3 * w, c1:c1 + w],
                                  s[3 * w:, c1:c1 + w] + bias_last[t]], sink)
            cols.append(jnp.concatenate([p.astype(_bf16) for p in p0] + [zero_blk], axis=0))
            cols.append(jnp.concatenate([zero_blk] + [p.astype(_bf16) for p in p1], axis=0))
            dens += [l0, l1]
        pt = jnp.concatenate(cols, axis=1)
        kb0 = qbs[t] * (Q_TILE // WINDOW)
        vth = jnp.concatenate([vtb[kb0 + j, h * HEAD_DIM:(h + 1) * HEAD_DIM, :]
                               for j in range(KEY_SPAN // WINDOW)], axis=1)
        o_t = jnp.dot(vth, pt, preferred_element_type=_f32) / jnp.concatenate(dens, axis=1)
        for i in range(Q_PER_KV):
            g = Q_PER_KV * h + i
            ot[t, g * HEAD_DIM:(g + 1) * HEAD_DIM, :] = o_t[:, i * Q_TILE:(i + 1) * Q_TILE]
        if h == N_KV_HEADS - 1:
            for c in range(ATTN_WIDTH // LANES):
                o_ref[t * Q_TILE:(t + 1) * Q_TILE, c * LANES:(c + 1) * LANES] = (
                    ot[t, c * LANES:(c + 1) * LANES, :].T.astype(_bf16))
        s = s_next


def _attention(seq, q, k, v, sinks):
    t_tokens = q.shape[0]
    rows = ATTN_TILES * Q_TILE
    steps = seq // rows
    return pl.pallas_call(
        _attn_kernel,
        out_shape=jax.ShapeDtypeStruct((t_tokens, ATTN_WIDTH), _bf16),
        grid=(t_tokens // seq, steps),
        in_specs=[pl.BlockSpec(memory_space=pltpu.SMEM),
                  pl.BlockSpec((rows, ATTN_WIDTH), lambda b, j: (b * steps + j, 0)),
                  pl.BlockSpec((seq, KV_WIDTH), lambda b, j: (b, 0)),
                  pl.BlockSpec((seq, KV_WIDTH), lambda b, j: (b, 0))],
        out_specs=pl.BlockSpec((rows, ATTN_WIDTH), lambda b, j: (b * steps + j, 0)),
        scratch_shapes=[pltpu.VMEM((N_KV_HEADS, seq + 2 * WINDOW, HEAD_DIM), _bf16),
                        pltpu.VMEM((seq // WINDOW + 2, KV_WIDTH, WINDOW), _bf16),
                        pltpu.VMEM((ATTN_TILES, ATTN_WIDTH, Q_TILE), _f32)],
        compiler_params=pltpu.CompilerParams(dimension_semantics=("parallel", "arbitrary"),
                                             vmem_limit_bytes=VMEM_LIMIT),
        name="attention",
    )(sinks, q, k, v)


def _outproj_kernel(x_ref, mf_ref, at_ref, g1_ref, sh_ref, sc_ref, ng_ref, wof_ref, woa_ref, bo_ref,
                    wr_ref, br_ref, tri_ref, x1_ref, hp_ref, rt_ref, cnt_ref, carry_ref):
    i = pl.program_id(0)

    @pl.when(i == 0)
    def _():
        carry_ref[...] = jnp.zeros_like(carry_ref)

    def project(rows):
        mf = jnp.concatenate([mf_ref[c, rows, :] for c in range(FOURIER_WIDTH // LANES)], axis=1)
        return (jnp.dot(mf.astype(_bf16), wof_ref[...], preferred_element_type=_f32)
                + jnp.dot(at_ref[rows, :], woa_ref[...], preferred_element_type=_f32) + bo_ref[...])

    tiles = _sub_tiles(x_ref.shape[0])
    mix = project(tiles[0])
    logit_cols = []
    for n, rows in enumerate(tiles):
        mix_next = project(tiles[n + 1]) if n + 1 < len(tiles) else None
        x1 = x_ref[rows, :] + g1_ref[0] * mix
        x1_ref[rows, :] = x1
        h = _rms(x1, ng_ref[...]) * (1.0 + sc_ref[0]) + sh_ref[0]
        hp_ref[rows, :] = _pack_halves(h[:, :HALF], h[:, HALF:])
        logit_cols.append(lax.dot_general(wr_ref[...], h.astype(_bf16), (((1,), (1,)), ((), ())),
                                          preferred_element_type=_f32))
        mix = mix_next
    logits = jnp.concatenate(logit_cols, axis=1) + br_ref[...]
    tm = logits.shape[1]
    erow = lax.broadcasted_iota(jnp.int32, logits.shape, 0)
    work = logits
    hots, vals, idxs = [], [], []
    for _k in range(TOP_K):
        mx = jnp.max(work, axis=0, keepdims=True)
        ix = jnp.min(jnp.where(work == mx, erow, N_EXPERTS), axis=0, keepdims=True)
        hot = erow == ix
        work = jnp.where(hot, -jnp.inf, work)
        hots.append(hot)
        vals.append(mx)
        idxs.append(ix)
    es = [jnp.exp(v - vals[0]) for v in vals]
    den = es[0] + es[1] + es[2] + es[3]
    member = jnp.zeros(logits.shape, _f32)
    for hot in hots:
        member = member + hot.astype(_f32)
    carry = carry_ref[...]
    parts = []
    for c0 in range(0, tm, RANK_TILE):
        m = member[:, c0:c0 + RANK_TILE]
        parts.append(jnp.dot(m.astype(_bf16), tri_ref[...], preferred_element_type=_f32)
                     + jnp.concatenate([carry] * (RANK_TILE // LANES), axis=1))
        carry = carry + jnp.broadcast_to(jnp.sum(m, axis=1, keepdims=True), carry.shape)
    before = jnp.concatenate(parts, axis=1)
    r16 = lax.broadcasted_iota(jnp.int32, (ROUTE_ROWS, tm), 0)
    slab = jnp.zeros((ROUTE_ROWS, tm), jnp.int32)
    for k in range(TOP_K):
        rank = jnp.sum(jnp.where(hots[k], before, 0.0), axis=0, keepdims=True).astype(jnp.int32)
        slab = jnp.where(r16 == k, idxs[k], slab)
        slab = jnp.where(r16 == TOP_K + k, rank, slab)
        slab = jnp.where(r16 == 2 * TOP_K + k, lax.bitcast_convert_type(es[k] / den, jnp.int32), slab)
    rt_ref[...] = slab
    carry_ref[...] = carry
    cnt_ref[...] = carry.astype(jnp.int32)


def _outproj(seq, x, mf, attn, g1, sh2, sc2, norm_g, wo_f, wo_a, b_o, w_r, b_r, tri):
    t_tokens = x.shape[0]
    tm = TOKEN_TILE
    tiles_per_seq = seq // tm
    row = lambda i: (i, 0)
    per_seq = lambda i: (i // tiles_per_seq, 0, 0)
    const = lambda i: (0, 0)
    return pl.pallas_call(
        _outproj_kernel,
        out_shape=[jax.ShapeDtypeStruct((t_tokens, D_MODEL), _f32),
                   jax.ShapeDtypeStruct((t_tokens, HALF), jnp.int32),
                   jax.ShapeDtypeStruct((ROUTE_ROWS, t_tokens), jnp.int32),
                   jax.ShapeDtypeStruct((N_EXPERTS, LANES), jnp.int32)],
        grid=(t_tokens // tm,),
        in_specs=[pl.BlockSpec((tm, D_MODEL), row),
                  pl.BlockSpec((FOURIER_WIDTH // LANES, tm, LANES), lambda i: (0, i, 0)),
                  pl.BlockSpec((tm, ATTN_WIDTH), row),
                  pl.BlockSpec((1, 1, D_MODEL), per_seq), pl.BlockSpec((1, 1, D_MODEL), per_seq),
                  pl.BlockSpec((1, 1, D_MODEL), per_seq),
                  pl.BlockSpec((1, D_MODEL), const),
                  pl.BlockSpec((FOURIER_WIDTH, D_MODEL), const), pl.BlockSpec((ATTN_WIDTH, D_MODEL), const),
                  pl.BlockSpec((1, D_MODEL), const),
                  pl.BlockSpec((N_EXPERTS, D_MODEL), const), pl.BlockSpec((N_EXPERTS, tm), const),
                  pl.BlockSpec((RANK_TILE, RANK_TILE), const)],
        out_specs=[pl.BlockSpec((tm, D_MODEL), row), pl.BlockSpec((tm, HALF), row),
                   pl.BlockSpec((ROUTE_ROWS, tm), lambda i: (0, i)),
                   pl.BlockSpec((N_EXPERTS, LANES), const)],
        scratch_shapes=[pltpu.VMEM((N_EXPERTS, LANES), _f32)],
        compiler_params=pltpu.CompilerParams(dimension_semantics=("arbitrary",), vmem_limit_bytes=VMEM_LIMIT),
        name="outproj",
    )(x, mf, attn, g1, sh2, sc2, norm_g, wo_f, wo_a, b_o, w_r, b_r, tri)


def _expert_kernel(blk_ref, bexp_ref, units_ref, x_ref, wgu_ref, bgu_ref, wd_ref, bd_ref, y_ref, wgu_bf, wd_bf):
    i = pl.program_id(0)
    prev = bexp_ref[jnp.maximum(i - 1, 0)]

    @pl.when((i == 0) | (bexp_ref[i] != prev))
    def _():
        wgu_bf[...] = wgu_ref[...].astype(_bf16)
        wd_bf[...] = wd_ref[...].astype(_bf16)

    def gate_up(rows):
        xa, xb = _unpack_halves(x_ref[rows, :])
        return (jnp.dot(xa.astype(_bf16), wgu_bf[:HALF, :], preferred_element_type=_f32)
                + jnp.dot(xb.astype(_bf16), wgu_bf[HALF:, :], preferred_element_type=_f32) + bgu_ref[...])

    def activate(gu):
        g = jnp.minimum(gu[:, :D_FF], SWIGLU_LIMIT)
        u = jnp.clip(gu[:, D_FF:], -SWIGLU_LIMIT, SWIGLU_LIMIT)
        return ((u + 1.0) * (g * (1.0 / (1.0 + jnp.exp(-SWIGLU_ALPHA * g))))).astype(_bf16)

    def down(rows, act):
        y = jnp.dot(act, wd_bf[...], preferred_element_type=_f32) + bd_ref[...]
        y_ref[rows, :] = _pack_halves(y[:, :HALF], y[:, HALF:])

    @pl.loop(0, units_ref[i])
    def _(u):
        base = pl.multiple_of(u * EXPERT_UNIT, EXPERT_UNIT)
        tiles = [pl.ds(base + r, EXPERT_SUB) for r in range(0, EXPERT_UNIT, EXPERT_SUB)]
        gu = gate_up(tiles[0])
        for n, rows in enumerate(tiles):
            gu_next = gate_up(tiles[n + 1]) if n + 1 < len(tiles) else None
            down(rows, activate(gu))
            gu = gu_next


def _experts(layer, xs, blk, bexp, units, w_gu, b_gu, w_down, b_down):
    p_rows = xs.shape[0]
    nblk = p_rows // EXPERT_BLOCK
    grid_spec = pltpu.PrefetchScalarGridSpec(
        num_scalar_prefetch=3,
        grid=(nblk,),
        in_specs=[pl.BlockSpec((EXPERT_BLOCK, HALF), lambda i, blk, be, na: (blk[i], 0)),
                  pl.BlockSpec((None, None, D_MODEL, 2 * D_FF), lambda i, blk, be, na: (layer, be[i], 0, 0)),
                  pl.BlockSpec((None, 1, 2 * D_FF), lambda i, blk, be, na: (be[i], 0, 0)),
                  pl.BlockSpec((None, None, D_FF, D_MODEL), lambda i, blk, be, na: (layer, be[i], 0, 0)),
                  pl.BlockSpec((None, 1, D_MODEL), lambda i, blk, be, na: (be[i], 0, 0))],
        out_specs=pl.BlockSpec((EXPERT_BLOCK, HALF), lambda i, blk, be, na: (blk[i], 0)),
        scratch_shapes=[pltpu.VMEM((D_MODEL, 2 * D_FF), _bf16), pltpu.VMEM((D_FF, D_MODEL), _bf16)])
    return pl.pallas_call(
        _expert_kernel,
        out_shape=jax.ShapeDtypeStruct((p_rows, HALF), jnp.int32),
        grid_spec=grid_spec,
        compiler_params=pltpu.CompilerParams(dimension_semantics=("arbitrary",), vmem_limit_bytes=VMEM_LIMIT),
        name="experts",
    )(blk, bexp, units, xs, w_gu, b_gu.reshape(N_EXPERTS, 1, 2 * D_FF), w_down,
      b_down.reshape(N_EXPERTS, 1, D_MODEL))


def _sc_mesh():
    return plsc.VectorSubcoreMesh(core_axis_name="c", subcore_axis_name="s")


def _sc_worker():
    return lax.axis_index("s") * 2 + lax.axis_index("c")


def _dispatch(hp, dest, p_rows):
    t_tokens = hp.shape[0]
    per_worker = t_tokens // (SC_CHUNK * SC_WORKERS)

    @functools.partial(
        pl.kernel, mesh=_sc_mesh(),
        out_type=jax.ShapeDtypeStruct((p_rows, HALF), jnp.int32),
        scratch_types=[pltpu.VMEM((TOP_K, SC_CHUNK), jnp.int32),
                       pltpu.VMEM((SC_CHUNK, HALF), jnp.int32),
                       pltpu.SemaphoreType.DMA],
        name="dispatch")
    def run(hp_hbm, dest_hbm, xs_hbm, idx_v, rows_v, sem):
        base = _sc_worker() * per_worker

        @pl.loop(0, per_worker)
        def _(j):
            chunk = base + j
            for k in range(TOP_K):
                pltpu.sync_copy(dest_hbm.at[pl.ds(k * t_tokens + chunk * SC_CHUNK, SC_CHUNK)], idx_v.at[k])
            pltpu.sync_copy(hp_hbm.at[pl.ds(chunk * SC_CHUNK, SC_CHUNK)], rows_v)
            copies = [pltpu.async_copy(rows_v, xs_hbm.at[idx_v.at[k]], sem) for k in range(TOP_K)]
            for cp in copies:
                cp.wait()

    return run(hp, dest)


def _combine_rows(y, dest, w, x1, g2, seq):
    t_tokens = x1.shape[0]
    cc = COMBINE_CHUNK
    n_chunks = t_tokens // cc
    per_worker = n_chunks // SC_WORKERS
    own = per_worker * cc
    assert per_worker % 2 == 0 and seq % own == 0
    n_vec = HALF // SC_LANES
    row_buf = pltpu.VMEM((TOP_K, cc, HALF), jnp.int32)
    x_buf = pltpu.VMEM((cc, D_MODEL), _f32)

    @functools.partial(
        pl.kernel, mesh=_sc_mesh(),
        out_type=jax.ShapeDtypeStruct((t_tokens, D_MODEL), _f32),
        scratch_types=[pltpu.VMEM((TOP_K * own,), jnp.int32),
                       pltpu.VMEM((TOP_K * own,), _f32),
                       row_buf, row_buf, x_buf, x_buf,
                       pltpu.VMEM((D_MODEL,), _f32),
                       pltpu.SemaphoreType.DMA, pltpu.SemaphoreType.DMA,
                       pltpu.SemaphoreType.DMA, pltpu.SemaphoreType.DMA],
        compiler_params=pltpu.CompilerParams(needs_layout_passes=False),
        name="combine_rows")
    def run(y_hbm, dest_hbm, w_hbm, x1_hbm, g2_hbm, x2_hbm,
            idx_v, w_v, rows0, rows1, xb0, xb1, g_v, in0, in1, out0, out1):
        base = _sc_worker() * per_worker
        bufs = ((rows0, xb0, in0, out0), (rows1, xb1, in1, out1))
        for k in range(TOP_K):
            pltpu.sync_copy(dest_hbm.at[pl.ds(k * t_tokens + base * cc, own)], idx_v.at[pl.ds(k * own, own)])
            pltpu.sync_copy(w_hbm.at[pl.ds(k * t_tokens + base * cc, own)], w_v.at[pl.ds(k * own, own)])
        pltpu.sync_copy(g2_hbm.at[(base * cc) // seq], g_v)

        def loads(c, b):
            rows, xb, isem, _ = bufs[b]
            cps = [pltpu.make_async_copy(y_hbm.at[idx_v.at[pl.ds(k * own + c * cc, cc)]], rows.at[k], isem)
                   for k in range(TOP_K)]
            cps.append(pltpu.make_async_copy(x1_hbm.at[pl.ds((base + c) * cc, cc)], xb, isem))
            return cps

        def store(c, b):
            _, xb, _, osem = bufs[b]
            return pltpu.make_async_copy(xb, x2_hbm.at[pl.ds((base + c) * cc, cc)], osem)

        def combine(c, b):
            rows, xb, _, _ = bufs[b]

            @pl.loop(0, cc)
            def _(t):
                ws = [plsc.load_gather(w_v, [jnp.full((SC_LANES,), k * own + c * cc + t, jnp.int32)])
                      for k in range(TOP_K)]

                @plsc.parallel_loop(0, n_vec)
                def _(j):
                    lo = pl.ds(j * SC_LANES, SC_LANES)
                    hi = pl.ds(HALF + j * SC_LANES, SC_LANES)
                    acc_a = None
                    acc_b = None
                    for k in range(TOP_K):
                        p = rows[k, t, lo]
                        a = lax.bitcast_convert_type(p & HI_MASK, _f32) * ws[k]
                        b_ = lax.bitcast_convert_type(p << 16, _f32) * ws[k]
                        acc_a = a if acc_a is None else acc_a + a
                        acc_b = b_ if acc_b is None else acc_b + b_
                    xb[t, lo] = xb[t, lo] + g_v[lo] * acc_a
                    xb[t, hi] = xb[t, hi] + g_v[hi] * acc_b

        for cp in loads(0, 0):
            cp.start()

        @pl.loop(0, per_worker // 2)
        def _(c2):
            for b in range(2):
                c = 2 * c2 + b

                @pl.when(c >= 1)
                def _():
                    store(c - 1, 1 - b).wait()

                @pl.when(c + 1 < per_worker)
                def _():
                    for cp in loads(c + 1, 1 - b):
                        cp.start()

                for cp in loads(c, b):
                    cp.wait()
                combine(c, b)
                store(c, b).start()

        store(per_worker - 1, 1).wait()

    return run(y, dest, w, x1, g2)


def _final_kernel(x_ref, ng_ref, o_ref):
    o_ref[...] = _rms(x_ref[...], ng_ref[...])


def _final(x, final_g):
    t_tokens = x.shape[0]
    tm = TOKEN_TILE
    row = lambda i: (i, 0)
    return pl.pallas_call(
        _final_kernel,
        out_shape=jax.ShapeDtypeStruct((t_tokens, D_MODEL), _f32),
        grid=(t_tokens // tm,),
        in_specs=[pl.BlockSpec((tm, D_MODEL), row), pl.BlockSpec((1, D_MODEL), lambda i: (0, 0))],
        out_specs=pl.BlockSpec((tm, D_MODEL), row),
        compiler_params=pltpu.CompilerParams(dimension_semantics=("parallel",), vmem_limit_bytes=VMEM_LIMIT),
        name="final",
    )(x, final_g)


def _plan(route, counts, nblk):
    cnt = counts[:, 0]
    padded = ((cnt + EXPERT_BLOCK - 1) // EXPERT_BLOCK) * EXPERT_BLOCK
    pad_end = jnp.cumsum(padded)
    pad_start = pad_end - padded
    expert = jnp.arange(N_EXPERTS, dtype=jnp.int32)[:, None, None]
    start_of = jnp.sum(jnp.where(route[None, :TOP_K] == expert, pad_start[:, None, None], 0), axis=0)
    dest = start_of + route[TOP_K:2 * TOP_K]
    nact = jnp.maximum(pad_end[-1] // EXPERT_BLOCK, 1)
    step = jnp.arange(nblk, dtype=jnp.int32)
    blk = jnp.minimum(step, nact - 1)
    bexp = jnp.minimum(jnp.sum(pad_end[None, :] <= (blk * EXPERT_BLOCK)[:, None], axis=1), N_EXPERTS - 1)
    onehot = bexp[:, None] == jnp.arange(N_EXPERTS, dtype=jnp.int32)[None, :]
    routed = jnp.sum(jnp.where(onehot, cnt + pad_start - (blk * EXPERT_BLOCK)[:, None], 0), axis=1)
    routed = jnp.clip(routed, 0, EXPERT_BLOCK)
    units = jnp.where(step < pad_end[-1] // EXPERT_BLOCK, (routed + EXPERT_UNIT - 1) // EXPERT_UNIT, 0)
    w = lax.bitcast_convert_type(route[2 * TOP_K:3 * TOP_K], _f32)
    return (dest.reshape(-1), w.reshape(-1), blk.astype(jnp.int32), bexp.astype(jnp.int32), units.astype(jnp.int32))


def _trunk(x_all, c_all, norm1_g, ada_w, ada_b, w_in, b_in, w_fmix, sinks, w_o, b_o,
           norm2_g, w_router, b_router, w_gu, b_gu, w_down, b_down, final_g):
    nb, seq, _ = x_all.shape
    t_tokens = nb * seq
    assert seq % TOKEN_TILE == 0 and seq % (ATTN_TILES * Q_TILE) == 0 and seq >= KEY_SPAN
    assert t_tokens % (SC_CHUNK * SC_WORKERS) == 0 and seq % COMBINE_CHUNK == 0
    nblk = (t_tokens * TOP_K) // EXPERT_BLOCK + N_EXPERTS
    p_rows = nblk * EXPERT_BLOCK

    mod = _modulation(c_all, ada_w, ada_b)
    mod = mod.reshape(DEPTH, nb, 6, 1, D_MODEL)
    rope_tabs = _rope_tables(seq)
    dft = _dft_tables(seq)
    tri = jnp.triu(jnp.ones((RANK_TILE, RANK_TILE), _f32), 1).astype(_bf16)

    x = x_all.reshape(t_tokens, D_MODEL)
    for l in range(DEPTH):
        sh1, sc1, g1, sh2, sc2, g2 = (mod[l, :, j] for j in range(6))
        ab = _channel_dft_fold(w_fmix[l], seq)
        y, q, k, v = _inproj(seq, x, sh1, sc1, norm1_g[l].reshape(1, D_MODEL), w_in[l].astype(_bf16),
                             b_in[l].reshape(1, IN_WIDTH), ab, rope_tabs)
        mf = _fourier(seq, y, dft)
        attn = _attention(seq, q, k, v, sinks[l])
        w_o_bf = w_o[l].astype(_bf16)
        w_r = w_router[l].T.astype(_bf16)
        b_r = jnp.broadcast_to(b_router[l][:, None], (N_EXPERTS, TOKEN_TILE))
        x1, hp, route, counts = _outproj(seq, x, mf, attn, g1, sh2, sc2, norm2_g[l].reshape(1, D_MODEL),
                                         w_o_bf[:FOURIER_WIDTH], w_o_bf[FOURIER_WIDTH:],
                                         b_o[l].reshape(1, D_MODEL), w_r, b_r, tri)
        dest, w, blk, bexp, units = _plan(route, counts, nblk)
        xs = _dispatch(hp, dest, p_rows)
        ys = _experts(l, xs, blk, bexp, units, w_gu, b_gu[l], w_down, b_down[l])
        x = _combine_rows(ys, dest, w, x1, g2.reshape(nb, D_MODEL), seq)
    out = _final(x, final_g.reshape(1, D_MODEL))
    return out.reshape(nb, seq, D_MODEL)


def kernel(x_prompt, x_sample, c_prompt, c_sample, norm1_g, ada_w, ada_b, w_in, b_in, w_fmix, sinks, w_o, b_o,
           norm2_g, w_router, b_router, w_gu, b_gu, w_down, b_down, final_g):
    ws = (norm1_g, ada_w, ada_b, w_in, b_in, w_fmix, sinks, w_o, b_o,
          norm2_g, w_router, b_router, w_gu, b_gu, w_down, b_down, final_g)
    return _trunk(x_prompt, c_prompt, *ws), _trunk(x_sample, c_sample, *ws)
```

```python
import functools

import jax
import jax.numpy as jnp
from jax import lax
from jax.experimental import pallas as pl
from jax.experimental.pallas import tpu as pltpu
from jax.experimental.pallas import tpu_sc as plsc

D_MODEL = 1024
DEPTH = 4
FOURIER_WIDTH = 256
N_FGROUPS = 4
FGROUP_DIM = 64
HEAD_DIM = 64
N_Q_HEADS = 12
N_KV_HEADS = 4
Q_PER_KV = 3
ATTN_WIDTH = N_Q_HEADS * HEAD_DIM
KV_WIDTH = N_KV_HEADS * HEAD_DIM
IN_WIDTH = FOURIER_WIDTH + ATTN_WIDTH + 2 * KV_WIDTH
WINDOW = 128
ROPE_THETA = 500000.0
ROT_DIM = 16
N_EXPERTS = 32
TOP_K = 4
D_FF = 512
EXPERT_BLOCK = 2048
EXPERT_UNIT = 512
EXPERT_SUB = 512
SWIGLU_LIMIT = 7.0
SWIGLU_ALPHA = 1.702
NORM_EPS = 1e-5

LANES = 128
HALF = D_MODEL // 2
TOKEN_TILE = 1024
RANK_TILE = 512
SUB_TILE = 256
Q_TILE = 256
ATTN_TILES = 4
KEY_SPAN = Q_TILE + 2 * WINDOW
SC_CHUNK = 128
SC_WORKERS = 32
SC_LANES = 16
COMBINE_CHUNK = 16
VMEM_LIMIT = 56 * 1024 * 1024
NEG_BIG = -1e30
LOG2E = 1.4426950408889634
Q_SCALE = HEAD_DIM ** -0.5 * LOG2E
HI_MASK = -65536
ROUTE_ROWS = 16

_f32 = jnp.float32
_bf16 = jnp.bfloat16


def _pack_halves(a, b):
    ua = lax.bitcast_convert_type(a.astype(_bf16).astype(_f32), jnp.int32)
    ub = lax.bitcast_convert_type(b.astype(_bf16).astype(_f32), jnp.int32)
    return ua | lax.shift_right_logical(ub, 16)


def _unpack_halves(p):
    a = lax.bitcast_convert_type(p & HI_MASK, _f32)
    b = lax.bitcast_convert_type(lax.shift_left(p, 16), _f32)
    return a, b


def _rms(x, g):
    ms = jnp.mean(x * x, axis=-1, keepdims=True)
    return x * lax.rsqrt(ms + NORM_EPS) * g


def _mod_kernel(c_ref, w_ref, b_ref, o_ref):
    c = c_ref[...]
    s = c * (1.0 / (1.0 + jnp.exp(-c)))
    o_ref[0] = jnp.dot(s.astype(_bf16), w_ref[0].astype(_bf16), preferred_element_type=_f32) + b_ref[0]


def _modulation(c_all, ada_w, ada_b):
    nb = c_all.shape[0]
    ncol = ada_w.shape[2] // D_MODEL
    return pl.pallas_call(
        _mod_kernel,
        out_shape=jax.ShapeDtypeStruct((DEPTH, nb, 6 * D_MODEL), _f32),
        grid=(DEPTH, ncol),
        in_specs=[
            pl.BlockSpec((nb, D_MODEL), lambda l, j: (0, 0)),
            pl.BlockSpec((1, D_MODEL, D_MODEL), lambda l, j: (l, 0, j)),
            pl.BlockSpec((1, 1, D_MODEL), lambda l, j: (l, 0, j)),
        ],
        out_specs=pl.BlockSpec((1, nb, D_MODEL), lambda l, j: (l, 0, j)),
        name="modulation",
    )(c_all, ada_w, ada_b.reshape(DEPTH, 1, 6 * D_MODEL))


def _inproj_matmul(x, sh_ref, sc_ref, ng_ref, w_ref, b_ref, *_):
    h = _rms(x, ng_ref[...]) * (1.0 + sc_ref[0]) + sh_ref[0]
    return jnp.dot(h.astype(_bf16), w_ref[...], preferred_element_type=_f32) + b_ref[...]


def _inproj_finish(rows, z, sh_ref, sc_ref, ng_ref, w_ref, b_ref, ab_ref, rc_ref, rs1_ref, rs2_ref,
                   y_ref, q_ref, k_ref, v_ref):
    f = z[:, :FOURIER_WIDTH]
    y_ref[rows, :] = jnp.dot(f.astype(_bf16), ab_ref[...], preferred_element_type=_f32).astype(_bf16)
    rc = rc_ref[rows, :]
    rs1 = rs1_ref[rows, :]
    rs2 = rs2_ref[rows, :]

    def rope(t):
        return t * rc + pltpu.roll(t, LANES - ROT_DIM // 2, 1) * rs1 + pltpu.roll(t, ROT_DIM // 2, 1) * rs2

    q0 = FOURIER_WIDTH
    for c in range(ATTN_WIDTH // LANES):
        t = z[:, q0 + c * LANES:q0 + (c + 1) * LANES]
        q_ref[rows, c * LANES:(c + 1) * LANES] = (rope(t) * Q_SCALE).astype(_bf16)
    k0 = q0 + ATTN_WIDTH
    for c in range(KV_WIDTH // LANES):
        t = z[:, k0 + c * LANES:k0 + (c + 1) * LANES]
        k_ref[rows, c * LANES:(c + 1) * LANES] = rope(t).astype(_bf16)
    v_ref[rows, :] = z[:, k0 + KV_WIDTH:].astype(_bf16)


def _sub_tiles(n_rows):
    return [pl.ds(r, SUB_TILE) for r in range(0, n_rows, SUB_TILE)]


def _inproj_kernel(x_ref, *refs):
    tiles = _sub_tiles(x_ref.shape[0])
    z = _inproj_matmul(x_ref[tiles[0], :], *refs)
    for n, rows in enumerate(tiles):
        z_next = _inproj_matmul(x_ref[tiles[n + 1], :], *refs) if n + 1 < len(tiles) else None
        _inproj_finish(rows, z, *refs)
        z = z_next


def _inproj(seq, x, sh1, sc1, norm_g, w_in, b_in, ab, rope_tabs):
    t_tokens = x.shape[0]
    tm = TOKEN_TILE
    tiles_per_seq = seq // tm
    row = lambda i: (i, 0)
    per_seq = lambda i: (i // tiles_per_seq, 0, 0)
    const = lambda i: (0, 0)
    pos = lambda i: (i % tiles_per_seq, 0)
    return pl.pallas_call(
        _inproj_kernel,
        out_shape=[jax.ShapeDtypeStruct((t_tokens, 2 * FOURIER_WIDTH), _bf16),
                   jax.ShapeDtypeStruct((t_tokens, ATTN_WIDTH), _bf16),
                   jax.ShapeDtypeStruct((t_tokens, KV_WIDTH), _bf16),
                   jax.ShapeDtypeStruct((t_tokens, KV_WIDTH), _bf16)],
        grid=(t_tokens // tm,),
        in_specs=[pl.BlockSpec((tm, D_MODEL), row),
                  pl.BlockSpec((1, 1, D_MODEL), per_seq), pl.BlockSpec((1, 1, D_MODEL), per_seq),
                  pl.BlockSpec((1, D_MODEL), const),
                  pl.BlockSpec((D_MODEL, IN_WIDTH), const), pl.BlockSpec((1, IN_WIDTH), const),
                  pl.BlockSpec((FOURIER_WIDTH, 2 * FOURIER_WIDTH), const),
                  pl.BlockSpec((tm, LANES), pos), pl.BlockSpec((tm, LANES), pos), pl.BlockSpec((tm, LANES), pos)],
        out_specs=[pl.BlockSpec((tm, 2 * FOURIER_WIDTH), row), pl.BlockSpec((tm, ATTN_WIDTH), row),
                   pl.BlockSpec((tm, KV_WIDTH), row), pl.BlockSpec((tm, KV_WIDTH), row)],
        compiler_params=pltpu.CompilerParams(dimension_semantics=("parallel",), vmem_limit_bytes=VMEM_LIMIT),
        name="inproj",
    )(x, sh1, sc1, norm_g, w_in, b_in, ab, *rope_tabs)


def _fourier_kernel(y_ref, *refs):
    tabs, o_ref = refs[:-1], refs[-1]
    quarter = y_ref.shape[0] // 4
    w = FOURIER_WIDTH
    a = [y_ref[q * quarter:(q + 1) * quarter, :w].astype(_f32) for q in range(4)]
    b = [y_ref[q * quarter:(q + 1) * quarter, w:].astype(_f32) for q in range(4)]
    a_s, a_d, a_s13, a_d13 = a[0] + a[2], a[0] - a[2], a[1] + a[3], a[1] - a[3]
    b_s, b_d, b_s13, b_d13 = b[0] + b[2], b[0] - b[2], b[1] + b[3], b[1] - b[3]
    w_re = [a_s + a_s13, a_d + b_d13, a_s - a_s13, a_d - b_d13]
    w_im = [b_s + b_s13, b_d - a_d13, b_s - b_s13, b_d + a_d13]
    for r in range(4):
        out = (jnp.dot(tabs[2 * r][...], w_re[r].astype(_bf16), preferred_element_type=_f32)
               + jnp.dot(tabs[2 * r + 1][...], w_im[r].astype(_bf16), preferred_element_type=_f32))
        for c in range(FOURIER_WIDTH // LANES):
            o_ref[c, pl.ds(r, quarter, stride=4), :] = out[:, c * LANES:(c + 1) * LANES]


def _fourier(seq, y, dft):
    t_tokens = y.shape[0]
    quarter = seq // 4
    const = lambda b: (0, 0)
    return pl.pallas_call(
        _fourier_kernel,
        out_shape=jax.ShapeDtypeStruct((FOURIER_WIDTH // LANES, t_tokens, LANES), _f32),
        grid=(t_tokens // seq,),
        in_specs=[pl.BlockSpec((seq, 2 * FOURIER_WIDTH), lambda b: (b, 0))]
        + [pl.BlockSpec((quarter, quarter), const)] * 8,
        out_specs=pl.BlockSpec((FOURIER_WIDTH // LANES, seq, LANES), lambda b: (0, b, 0)),
        compiler_params=pltpu.CompilerParams(dimension_semantics=("parallel",), vmem_limit_bytes=VMEM_LIMIT),
        name="fourier",
    )(y, *dft)


def _dft_tables(seq):
    quarter = seq // 4
    j = jnp.arange(quarter, dtype=jnp.int32)[:, None]
    m = jnp.arange(quarter, dtype=jnp.int32)[None, :]
    tabs = []
    for r in range(4):
        ang = (((4 * j + r) * m) % seq).astype(_f32) * (2.0 * jnp.pi / seq)
        tabs += [jnp.cos(ang).astype(_bf16), jnp.sin(ang).astype(_bf16)]
    return tuple(tabs)


def _channel_dft_fold(w_fmix, seq):
    c = jnp.arange(FGROUP_DIM, dtype=jnp.int32)
    ang = ((c[:, None] * c[None, :]) % FGROUP_DIM).astype(_f32) * (2.0 * jnp.pi / FGROUP_DIM)
    scale = (seq * FGROUP_DIM) ** -0.5
    hp = lax.Precision.HIGHEST
    a = jnp.einsum('cm,gmd->gcd', jnp.cos(ang) * scale, w_fmix, precision=hp)
    b = jnp.einsum('cm,gmd->gcd', -jnp.sin(ang) * scale, w_fmix, precision=hp)
    eye = jnp.eye(N_FGROUPS, dtype=_f32)
    bd = lambda m: jnp.einsum('gcd,gh->gchd', m, eye).reshape(FOURIER_WIDTH, FOURIER_WIDTH)
    return jnp.concatenate([bd(a), bd(b)], axis=1).astype(_bf16)


def _rope_tables(seq):
    half = ROT_DIM // 2
    inv_freq = jnp.power(ROPE_THETA, -jnp.arange(0, ROT_DIM, 2, dtype=_f32) / ROT_DIM)
    ang = jnp.arange(seq, dtype=_f32)[:, None] * inv_freq[None, :]
    cos, sin = jnp.cos(ang), jnp.sin(ang)
    ones = jnp.ones((seq, HEAD_DIM - ROT_DIM), _f32)
    zeros = jnp.zeros((seq, HEAD_DIM - ROT_DIM), _f32)
    zh = jnp.zeros((seq, half), _f32)
    c1 = jnp.concatenate([cos, cos, ones], axis=1)
    s1 = jnp.concatenate([-sin, zh, zeros], axis=1)
    s2 = jnp.concatenate([zh, sin, zeros], axis=1)
    rep = LANES // HEAD_DIM
    return tuple(jnp.tile(t, (1, rep)) for t in (c1, s1, s2))


def _attn_kernel(sink_ref, q_ref, k_ref, v_ref, o_ref, kpad, vtb, ot):
    seq = k_ref.shape[0]
    nqb = seq // Q_TILE
    nkb = seq // WINDOW

    @pl.when(pl.program_id(1) == 0)
    def _():
        zk = jnp.zeros((WINDOW, HEAD_DIM), _bf16)
        for h in range(N_KV_HEADS):
            kpad[h, :WINDOW, :] = zk
            kpad[h, WINDOW + seq:, :] = zk
            kpad[h, WINDOW:WINDOW + seq, :] = k_ref[:, h * HEAD_DIM:(h + 1) * HEAD_DIM]
        zv = jnp.zeros((KV_WIDTH, WINDOW), _bf16)
        vtb[0] = zv
        vtb[nkb + 1] = zv
        for j in range(nkb):
            vtb[j + 1] = v_ref[j * WINDOW:(j + 1) * WINDOW, :].astype(_f32).T.astype(_bf16)

    ka = lax.broadcasted_iota(jnp.int32, (WINDOW, WINDOW), 0)
    qc = lax.broadcasted_iota(jnp.int32, (WINDOW, WINDOW), 1)
    tri_ge = jnp.where(ka >= qc, 0.0, NEG_BIG).astype(_f32)
    tri_le = jnp.where(ka <= qc, 0.0, NEG_BIG).astype(_f32)
    zero_blk = jnp.zeros((WINDOW, WINDOW), _bf16)
    n_tiles = q_ref.shape[0] // Q_TILE
    qbs = [pl.program_id(1) * n_tiles + t for t in range(n_tiles)]
    bias_first = [jnp.where(qb == 0, NEG_BIG, tri_ge) for qb in qbs]
    bias_last = [jnp.where(qb == nqb - 1, NEG_BIG, tri_le) for qb in qbs]

    def softmax_col(blocks, sink):
        m = blocks[0]
        for b in blocks[1:]:
            m = jnp.maximum(m, b)
        m = jnp.maximum(jnp.max(m, axis=0, keepdims=True), sink)
        ps = [jnp.exp2(b - m) for b in blocks]
        tot = ps[0]
        for p in ps[1:]:
            tot = tot + p
        return ps, jnp.sum(tot, axis=0, keepdims=True) + jnp.exp2(sink - m)

    w = WINDOW

    def scores(t, h):
        r0 = pl.multiple_of(qbs[t] * Q_TILE, Q_TILE)
        kh = kpad[h, pl.ds(r0, KEY_SPAN), :]
        q3 = jnp.concatenate([q_ref[t * Q_TILE:(t + 1) * Q_TILE, g * HEAD_DIM:(g + 1) * HEAD_DIM]
                              for g in range(Q_PER_KV * h, Q_PER_KV * (h + 1))], axis=0)
        return lax.dot_general(kh, q3, (((1,), (1,)), ((), ())), preferred_element_type=_f32)

    work = [(t, h) for t in range(n_tiles) for h in range(N_KV_HEADS)]
    s = scores(*work[0])
    for n, (t, h) in enumerate(work):
        s_next = scores(*work[n + 1]) if n + 1 < len(work) else None
        cols, dens = [], []
        for i in range(Q_PER_KV):
            sink = sink_ref[Q_PER_KV * h + i] * LOG2E
            c0 = 2 * i * w
            c1 = c0 + w
            p0, l0 = softmax_col([s[0:w, c0:c1] + bias_first[t], s[w:2 * w, c0:c1], s[2 * w:3 * w, c0:c1] + tri_le],
                                 sink)
            p1, l1 = softmax_col([s[w:2 * w, c1:c1 + w] + tri_ge, s[2 * w:3 * w, c1:c1 + w],
                                  s[3 * w:, c1:c1 + w] + bias_last[t]], sink)
            cols.append(jnp.concatenate([p.astype(_bf16) for p in p0] + [zero_blk], axis=0))
            cols.append(jnp.concatenate([zero_blk] + [p.astype(_bf16) for p in p1], axis=0))
            dens += [l0, l1]
        pt = jnp.concatenate(cols, axis=1)
        kb0 = qbs[t] * (Q_TILE // WINDOW)
        vth = jnp.concatenate([vtb[kb0 + j, h * HEAD_DIM:(h + 1) * HEAD_DIM, :]
                               for j in range(KEY_SPAN // WINDOW)], axis=1)
        o_t = jnp.dot(vth, pt, preferred_element_type=_f32) / jnp.concatenate(dens, axis=1)
        for i in range(Q_PER_KV):
            g = Q_PER_KV * h + i
            ot[t, g * HEAD_DIM:(g + 1) * HEAD_DIM, :] = o_t[:, i * Q_TILE:(i + 1) * Q_TILE]
        if h == N_KV_HEADS - 1:
            for c in range(ATTN_WIDTH // LANES):
                o_ref[t * Q_TILE:(t + 1) * Q_TILE, c * LANES:(c + 1) * LANES] = (
                    ot[t, c * LANES:(c + 1) * LANES, :].T.astype(_bf16))
        s = s_next


def _attention(seq, q, k, v, sinks):
    t_tokens = q.shape[0]
    rows = ATTN_TILES * Q_TILE
    steps = seq // rows
    return pl.pallas_call(
        _attn_kernel,
        out_shape=jax.ShapeDtypeStruct((t_tokens, ATTN_WIDTH), _bf16),
        grid=(t_tokens // seq, steps),
        in_specs=[pl.BlockSpec(memory_space=pltpu.SMEM),
                  pl.BlockSpec((rows, ATTN_WIDTH), lambda b, j: (b * steps + j, 0)),
                  pl.BlockSpec((seq, KV_WIDTH), lambda b, j: (b, 0)),
                  pl.BlockSpec((seq, KV_WIDTH), lambda b, j: (b, 0))],
        out_specs=pl.BlockSpec((rows, ATTN_WIDTH), lambda b, j: (b * steps + j, 0)),
        scratch_shapes=[pltpu.VMEM((N_KV_HEADS, seq + 2 * WINDOW, HEAD_DIM), _bf16),
                        pltpu.VMEM((seq // WINDOW + 2, KV_WIDTH, WINDOW), _bf16),
                        pltpu.VMEM((ATTN_TILES, ATTN_WIDTH, Q_TILE), _f32)],
        compiler_params=pltpu.CompilerParams(dimension_semantics=("parallel", "arbitrary"),
                                             vmem_limit_bytes=VMEM_LIMIT),
        name="attention",
    )(sinks, q, k, v)


def _outproj_kernel(x_ref, mf_ref, at_ref, g1_ref, sh_ref, sc_ref, ng_ref, wof_ref, woa_ref, bo_ref,
                    wr_ref, br_ref, tri_ref, x1_ref, hp_ref, rt_ref, cnt_ref, carry_ref):
    i = pl.program_id(0)

    @pl.when(i == 0)
    def _():
        carry_ref[...] = jnp.zeros_like(carry_ref)

    def project(rows):
        mf = jnp.concatenate([mf_ref[c, rows, :] for c in range(FOURIER_WIDTH // LANES)], axis=1)
        return (jnp.dot(mf.astype(_bf16), wof_ref[...], preferred_element_type=_f32)
                + jnp.dot(at_ref[rows, :], woa_ref[...], preferred_element_type=_f32) + bo_ref[...])

    tiles = _sub_tiles(x_ref.shape[0])
    mix = project(tiles[0])
    logit_cols = []
    for n, rows in enumerate(tiles):
        mix_next = project(tiles[n + 1]) if n + 1 < len(tiles) else None
        x1 = x_ref[rows, :] + g1_ref[0] * mix
        x1_ref[rows, :] = x1
        h = _rms(x1, ng_ref[...]) * (1.0 + sc_ref[0]) + sh_ref[0]
        hp_ref[rows, :] = _pack_halves(h[:, :HALF], h[:, HALF:])
        logit_cols.append(lax.dot_general(wr_ref[...], h.astype(_bf16), (((1,), (1,)), ((), ())),
                                          preferred_element_type=_f32))
        mix = mix_next
    logits = jnp.concatenate(logit_cols, axis=1) + br_ref[...]
    tm = logits.shape[1]
    erow = lax.broadcasted_iota(jnp.int32, logits.shape, 0)
    work = logits
    hots, vals, idxs = [], [], []
    for _k in range(TOP_K):
        mx = jnp.max(work, axis=0, keepdims=True)
        ix = jnp.min(jnp.where(work == mx, erow, N_EXPERTS), axis=0, keepdims=True)
        hot = erow == ix
        work = jnp.where(hot, -jnp.inf, work)
        hots.append(hot)
        vals.append(mx)
        idxs.append(ix)
    es = [jnp.exp(v - vals[0]) for v in vals]
    den = es[0] + es[1] + es[2] + es[3]
    member = jnp.zeros(logits.shape, _f32)
    for hot in hots:
        member = member + hot.astype(_f32)
    carry = carry_ref[...]
    parts = []
    for c0 in range(0, tm, RANK_TILE):
        m = member[:, c0:c0 + RANK_TILE]
        parts.append(jnp.dot(m.astype(_bf16), tri_ref[...], preferred_element_type=_f32)
                     + jnp.concatenate([carry] * (RANK_TILE // LANES), axis=1))
        carry = carry + jnp.broadcast_to(jnp.sum(m, axis=1, keepdims=True), carry.shape)
    before = jnp.concatenate(parts, axis=1)
    r16 = lax.broadcasted_iota(jnp.int32, (ROUTE_ROWS, tm), 0)
    slab = jnp.zeros((ROUTE_ROWS, tm), jnp.int32)
    for k in range(TOP_K):
        rank = jnp.sum(jnp.where(hots[k], before, 0.0), axis=0, keepdims=True).astype(jnp.int32)
        slab = jnp.where(r16 == k, idxs[k], slab)
        slab = jnp.where(r16 == TOP_K + k, rank, slab)
        slab = jnp.where(r16 == 2 * TOP_K + k, lax.bitcast_convert_type(es[k] / den, jnp.int32), slab)
    rt_ref[...] = slab
    carry_ref[...] = carry
    cnt_ref[...] = carry.astype(jnp.int32)


def _outproj(seq, x, mf, attn, g1, sh2, sc2, norm_g, wo_f, wo_a, b_o, w_r, b_r, tri):
    t_tokens = x.shape[0]
    tm = TOKEN_TILE
    tiles_per_seq = seq // tm
    row = lambda i: (i, 0)
    per_seq = lambda i: (i // tiles_per_seq, 0, 0)
    const = lambda i: (0, 0)
    return pl.pallas_call(
        _outproj_kernel,
        out_shape=[jax.ShapeDtypeStruct((t_tokens, D_MODEL), _f32),
                   jax.ShapeDtypeStruct((t_tokens, HALF), jnp.int32),
                   jax.ShapeDtypeStruct((ROUTE_ROWS, t_tokens), jnp.int32),
                   jax.ShapeDtypeStruct((N_EXPERTS, LANES), jnp.int32)],
        grid=(t_tokens // tm,),
        in_specs=[pl.BlockSpec((tm, D_MODEL), row),
                  pl.BlockSpec((FOURIER_WIDTH // LANES, tm, LANES), lambda i: (0, i, 0)),
                  pl.BlockSpec((tm, ATTN_WIDTH), row),
                  pl.BlockSpec((1, 1, D_MODEL), per_seq), pl.BlockSpec((1, 1, D_MODEL), per_seq),
                  pl.BlockSpec((1, 1, D_MODEL), per_seq),
                  pl.BlockSpec((1, D_MODEL), const),
                  pl.BlockSpec((FOURIER_WIDTH, D_MODEL), const), pl.BlockSpec((ATTN_WIDTH, D_MODEL), const),
                  pl.BlockSpec((1, D_MODEL), const),
                  pl.BlockSpec((N_EXPERTS, D_MODEL), const), pl.BlockSpec((N_EXPERTS, tm), const),
                  pl.BlockSpec((RANK_TILE, RANK_TILE), const)],
        out_specs=[pl.BlockSpec((tm, D_MODEL), row), pl.BlockSpec((tm, HALF), row),
                   pl.BlockSpec((ROUTE_ROWS, tm), lambda i: (0, i)),
                   pl.BlockSpec((N_EXPERTS, LANES), const)],
        scratch_shapes=[pltpu.VMEM((N_EXPERTS, LANES), _f32)],
        compiler_params=pltpu.CompilerParams(dimension_semantics=("arbitrary",), vmem_limit_bytes=VMEM_LIMIT),
        name="outproj",
    )(x, mf, attn, g1, sh2, sc2, norm_g, wo_f, wo_a, b_o, w_r, b_r, tri)


def _expert_kernel(blk_ref, bexp_ref, units_ref, x_ref, wgu_ref, bgu_ref, wd_ref, bd_ref, y_ref, wgu_bf, wd_bf):
    i = pl.program_id(0)
    prev = bexp_ref[jnp.maximum(i - 1, 0)]

    @pl.when((i == 0) | (bexp_ref[i] != prev))
    def _():
        wgu_bf[...] = wgu_ref[...].astype(_bf16)
        wd_bf[...] = wd_ref[...].astype(_bf16)

    def gate_up(rows):
        xa, xb = _unpack_halves(x_ref[rows, :])
        return (jnp.dot(xa.astype(_bf16), wgu_bf[:HALF, :], preferred_element_type=_f32)
                + jnp.dot(xb.astype(_bf16), wgu_bf[HALF:, :], preferred_element_type=_f32) + bgu_ref[...])

    def activate(gu):
        g = jnp.minimum(gu[:, :D_FF], SWIGLU_LIMIT)
        u = jnp.clip(gu[:, D_FF:], -SWIGLU_LIMIT, SWIGLU_LIMIT)
        return ((u + 1.0) * (g * (1.0 / (1.0 + jnp.exp(-SWIGLU_ALPHA * g))))).astype(_bf16)

    def down(rows, act):
        y = jnp.dot(act, wd_bf[...], preferred_element_type=_f32) + bd_ref[...]
        y_ref[rows, :] = _pack_halves(y[:, :HALF], y[:, HALF:])

    @pl.loop(0, units_ref[i])
    def _(u):
        base = pl.multiple_of(u * EXPERT_UNIT, EXPERT_UNIT)
        tiles = [pl.ds(base + r, EXPERT_SUB) for r in range(0, EXPERT_UNIT, EXPERT_SUB)]
        gu = gate_up(tiles[0])
        for n, rows in enumerate(tiles):
            gu_next = gate_up(tiles[n + 1]) if n + 1 < len(tiles) else None
            down(rows, activate(gu))
            gu = gu_next


def _experts(layer, xs, blk, bexp, units, w_gu, b_gu, w_down, b_down):
    p_rows = xs.shape[0]
    nblk = p_rows // EXPERT_BLOCK
    grid_spec = pltpu.PrefetchScalarGridSpec(
        num_scalar_prefetch=3,
        grid=(nblk,),
        in_specs=[pl.BlockSpec((EXPERT_BLOCK, HALF), lambda i, blk, be, na: (blk[i], 0)),
                  pl.BlockSpec((None, None, D_MODEL, 2 * D_FF), lambda i, blk, be, na: (layer, be[i], 0, 0)),
                  pl.BlockSpec((None, 1, 2 * D_FF), lambda i, blk, be, na: (be[i], 0, 0)),
                  pl.BlockSpec((None, None, D_FF, D_MODEL), lambda i, blk, be, na: (layer, be[i], 0, 0)),
                  pl.BlockSpec((None, 1, D_MODEL), lambda i, blk, be, na: (be[i], 0, 0))],
        out_specs=pl.BlockSpec((EXPERT_BLOCK, HALF), lambda i, blk, be, na: (blk[i], 0)),
        scratch_shapes=[pltpu.VMEM((D_MODEL, 2 * D_FF), _bf16), pltpu.VMEM((D_FF, D_MODEL), _bf16)])
    return pl.pallas_call(
        _expert_kernel,
        out_shape=jax.ShapeDtypeStruct((p_rows, HALF), jnp.int32),
        grid_spec=grid_spec,
        compiler_params=pltpu.CompilerParams(dimension_semantics=("arbitrary",), vmem_limit_bytes=VMEM_LIMIT),
        name="experts",
    )(blk, bexp, units, xs, w_gu, b_gu.reshape(N_EXPERTS, 1, 2 * D_FF), w_down,
      b_down.reshape(N_EXPERTS, 1, D_MODEL))


def _sc_mesh():
    return plsc.VectorSubcoreMesh(core_axis_name="c", subcore_axis_name="s")


def _sc_worker():
    return lax.axis_index("s") * 2 + lax.axis_index("c")


def _dispatch(hp, dest, p_rows):
    t_tokens = hp.shape[0]
    per_worker = t_tokens // (SC_CHUNK * SC_WORKERS)

    @functools.partial(
        pl.kernel, mesh=_sc_mesh(),
        out_type=jax.ShapeDtypeStruct((p_rows, HALF), jnp.int32),
        scratch_types=[pltpu.VMEM((TOP_K, SC_CHUNK), jnp.int32),
                       pltpu.VMEM((SC_CHUNK, HALF), jnp.int32),
                       pltpu.SemaphoreType.DMA],
        name="dispatch")
    def run(hp_hbm, dest_hbm, xs_hbm, idx_v, rows_v, sem):
        base = _sc_worker() * per_worker

        @pl.loop(0, per_worker)
        def _(j):
            chunk = base + j
            for k in range(TOP_K):
                pltpu.sync_copy(dest_hbm.at[pl.ds(k * t_tokens + chunk * SC_CHUNK, SC_CHUNK)], idx_v.at[k])
            pltpu.sync_copy(hp_hbm.at[pl.ds(chunk * SC_CHUNK, SC_CHUNK)], rows_v)
            copies = [pltpu.async_copy(rows_v, xs_hbm.at[idx_v.at[k]], sem) for k in range(TOP_K)]
            for cp in copies:
                cp.wait()

    return run(hp, dest)


def _combine_rows(y, dest, w, x1, g2, seq):
    t_tokens = x1.shape[0]
    cc = COMBINE_CHUNK
    n_chunks = t_tokens // cc
    per_worker = n_chunks // SC_WORKERS
    own = per_worker * cc
    assert per_worker % 2 == 0 and seq % own == 0
    n_vec = HALF // SC_LANES
    row_buf = pltpu.VMEM((TOP_K, cc, HALF), jnp.int32)
    x_buf = pltpu.VMEM((cc, D_MODEL), _f32)

    @functools.partial(
        pl.kernel, mesh=_sc_mesh(),
        out_type=jax.ShapeDtypeStruct((t_tokens, D_MODEL), _f32),
        scratch_types=[pltpu.VMEM((TOP_K * own,), jnp.int32),
                       pltpu.VMEM((TOP_K * own,), _f32),
                       row_buf, row_buf, x_buf, x_buf,
                       pltpu.VMEM((D_MODEL,), _f32),
                       pltpu.SemaphoreType.DMA, pltpu.SemaphoreType.DMA,
                       pltpu.SemaphoreType.DMA, pltpu.SemaphoreType.DMA],
        compiler_params=pltpu.CompilerParams(needs_layout_passes=False),
        name="combine_rows")
    def run(y_hbm, dest_hbm, w_hbm, x1_hbm, g2_hbm, x2_hbm,
            idx_v, w_v, rows0, rows1, xb0, xb1, g_v, in0, in1, out0, out1):
        base = _sc_worker() * per_worker
        bufs = ((rows0, xb0, in0, out0), (rows1, xb1, in1, out1))
        for k in range(TOP_K):
            pltpu.sync_copy(dest_hbm.at[pl.ds(k * t_tokens + base * cc, own)], idx_v.at[pl.ds(k * own, own)])
            pltpu.sync_copy(w_hbm.at[pl.ds(k * t_tokens + base * cc, own)], w_v.at[pl.ds(k * own, own)])
        pltpu.sync_copy(g2_hbm.at[(base * cc) // seq], g_v)

        def loads(c, b):
            rows, xb, isem, _ = bufs[b]
            cps = [pltpu.make_async_copy(y_hbm.at[idx_v.at[pl.ds(k * own + c * cc, cc)]], rows.at[k], isem)
                   for k in range(TOP_K)]
            cps.append(pltpu.make_async_copy(x1_hbm.at[pl.ds((base + c) * cc, cc)], xb, isem))
            return cps

        def store(c, b):
            _, xb, _, osem = bufs[b]
            return pltpu.make_async_copy(xb, x2_hbm.at[pl.ds((base + c) * cc, cc)], osem)

        def combine(c, b):
            rows, xb, _, _ = bufs[b]

            @pl.loop(0, cc)
            def _(t):
                ws = [plsc.load_gather(w_v, [jnp.full((SC_LANES,), k * own + c * cc + t, jnp.int32)])
                      for k in range(TOP_K)]

                @plsc.parallel_loop(0, n_vec)
                def _(j):
                    lo = pl.ds(j * SC_LANES, SC_LANES)
                    hi = pl.ds(HALF + j * SC_LANES, SC_LANES)
                    acc_a = None
                    acc_b = None
                    for k in range(TOP_K):
                        p = rows[k, t, lo]
                        a = lax.bitcast_convert_type(p & HI_MASK, _f32) * ws[k]
                        b_ = lax.bitcast_convert_type(p << 16, _f32) * ws[k]
                        acc_a = a if acc_a is None else acc_a + a
                        acc_b = b_ if acc_b is None else acc_b + b_
                    xb[t, lo] = xb[t, lo] + g_v[lo] * acc_a
                    xb[t, hi] = xb[t, hi] + g_v[hi] * acc_b

        for cp in loads(0, 0):
            cp.start()

        @pl.loop(0, per_worker // 2)
        def _(c2):
            for b in range(2):
                c = 2 * c2 + b

                @pl.when(c >= 1)
                def _():
                    store(c - 1, 1 - b).wait()

                @pl.when(c + 1 < per_worker)
                def _():
                    for cp in loads(c + 1, 1 - b):
                        cp.start()

                for cp in loads(c, b):
                    cp.wait()
                combine(c, b)
                store(c, b).start()

        store(per_worker - 1, 1).wait()

    return run(y, dest, w, x1, g2)


def _final_kernel(x_ref, ng_ref, o_ref):
    o_ref[...] = _rms(x_ref[...], ng_ref[...])


def _final(x, final_g):
    t_tokens = x.shape[0]
    tm = TOKEN_TILE
    row = lambda i: (i, 0)
    return pl.pallas_call(
        _final_kernel,
        out_shape=jax.ShapeDtypeStruct((t_tokens, D_MODEL), _f32),
        grid=(t_tokens // tm,),
        in_specs=[pl.BlockSpec((tm, D_MODEL), row), pl.BlockSpec((1, D_MODEL), lambda i: (0, 0))],
        out_specs=pl.BlockSpec((tm, D_MODEL), row),
        compiler_params=pltpu.CompilerParams(dimension_semantics=("parallel",), vmem_limit_bytes=VMEM_LIMIT),
        name="final",
    )(x, final_g)


def _plan(route, counts, nblk):
    cnt = counts[:, 0]
    padded = ((cnt + EXPERT_BLOCK - 1) // EXPERT_BLOCK) * EXPERT_BLOCK
    pad_end = jnp.cumsum(padded)
    pad_start = pad_end - padded
    expert = jnp.arange(N_EXPERTS, dtype=jnp.int32)[:, None, None]
    start_of = jnp.sum(jnp.where(route[None, :TOP_K] == expert, pad_start[:, None, None], 0), axis=0)
    dest = start_of + route[TOP_K:2 * TOP_K]
    nact = jnp.maximum(pad_end[-1] // EXPERT_BLOCK, 1)
    step = jnp.arange(nblk, dtype=jnp.int32)
    blk = jnp.minimum(step, nact - 1)
    bexp = jnp.minimum(jnp.sum(pad_end[None, :] <= (blk * EXPERT_BLOCK)[:, None], axis=1), N_EXPERTS - 1)
    onehot = bexp[:, None] == jnp.arange(N_EXPERTS, dtype=jnp.int32)[None, :]
    routed = jnp.sum(jnp.where(onehot, cnt + pad_start - (blk * EXPERT_BLOCK)[:, None], 0), axis=1)
    routed = jnp.clip(routed, 0, EXPERT_BLOCK)
    units = jnp.where(step < pad_end[-1] // EXPERT_BLOCK, (routed + EXPERT_UNIT - 1) // EXPERT_UNIT, 0)
    w = lax.bitcast_convert_type(route[2 * TOP_K:3 * TOP_K], _f32)
    return (dest.reshape(-1), w.reshape(-1), blk.astype(jnp.int32), bexp.astype(jnp.int32), units.astype(jnp.int32))


def _trunk(x_all, c_all, norm1_g, ada_w, ada_b, w_in, b_in, w_fmix, sinks, w_o, b_o,
           norm2_g, w_router, b_router, w_gu, b_gu, w_down, b_down, final_g):
    nb, seq, _ = x_all.shape
    t_tokens = nb * seq
    assert seq % TOKEN_TILE == 0 and seq % (ATTN_TILES * Q_TILE) == 0 and seq >= KEY_SPAN
    assert t_tokens % (SC_CHUNK * SC_WORKERS) == 0 and seq % COMBINE_CHUNK == 0
    nblk = (t_tokens * TOP_K) // EXPERT_BLOCK + N_EXPERTS
    p_rows = nblk * EXPERT_BLOCK

    mod = _modulation(c_all, ada_w, ada_b)
    mod = mod.reshape(DEPTH, nb, 6, 1, D_MODEL)
    rope_tabs = _rope_tables(seq)
    dft = _dft_tables(seq)
    tri = jnp.triu(jnp.ones((RANK_TILE, RANK_TILE), _f32), 1).astype(_bf16)

    x = x_all.reshape(t_tokens, D_MODEL)
    for l in range(DEPTH):
        sh1, sc1, g1, sh2, sc2, g2 = (mod[l, :, j] for j in range(6))
        ab = _channel_dft_fold(w_fmix[l], seq)
        y, q, k, v = _inproj(seq, x, sh1, sc1, norm1_g[l].reshape(1, D_MODEL), w_in[l].astype(_bf16),
                             b_in[l].reshape(1, IN_WIDTH), ab, rope_tabs)
        mf = _fourier(seq, y, dft)
        attn = _attention(seq, q, k, v, sinks[l])
        w_o_bf = w_o[l].astype(_bf16)
        w_r = w_router[l].T.astype(_bf16)
        b_r = jnp.broadcast_to(b_router[l][:, None], (N_EXPERTS, TOKEN_TILE))
        x1, hp, route, counts = _outproj(seq, x, mf, attn, g1, sh2, sc2, norm2_g[l].reshape(1, D_MODEL),
                                         w_o_bf[:FOURIER_WIDTH], w_o_bf[FOURIER_WIDTH:],
                                         b_o[l].reshape(1, D_MODEL), w_r, b_r, tri)
        dest, w, blk, bexp, units = _plan(route, counts, nblk)
        xs = _dispatch(hp, dest, p_rows)
        ys = _experts(l, xs, blk, bexp, units, w_gu, b_gu[l], w_down, b_down[l])
        x = _combine_rows(ys, dest, w, x1, g2.reshape(nb, D_MODEL), seq)
    out = _final(x, final_g.reshape(1, D_MODEL))
    return out.reshape(nb, seq, D_MODEL)


def kernel(x_prompt, x_sample, c_prompt, c_sample, norm1_g, ada_w, ada_b, w_in, b_in, w_fmix, sinks, w_o, b_o,
           norm2_g, w_router, b_router, w_gu, b_gu, w_down, b_down, final_g):
    ws = (norm1_g, ada_w, ada_b, w_in, b_in, w_fmix, sinks, w_o, b_o,
          norm2_g, w_router, b_router, w_gu, b_gu, w_down, b_down, final_g)
    return _trunk(x_prompt, c_prompt, *ws), _trunk(x_sample, c_sample, *ws)
```

```python
import functools

import jax
import jax.numpy as jnp
from jax import lax
from jax.experimental import pallas as pl
from jax.experimental.pallas import tpu as pltpu
from jax.experimental.pallas import tpu_sc as plsc

D_MODEL = 1024
DEPTH = 4
FOURIER_WIDTH = 256
N_FGROUPS = 4
FGROUP_DIM = 64
HEAD_DIM = 64
N_Q_HEADS = 12
N_KV_HEADS = 4
Q_PER_KV = 3
ATTN_WIDTH = N_Q_HEADS * HEAD_DIM
KV_WIDTH = N_KV_HEADS * HEAD_DIM
IN_WIDTH = FOURIER_WIDTH + ATTN_WIDTH + 2 * KV_WIDTH
WINDOW = 128
ROPE_THETA = 500000.0
ROT_DIM = 16
N_EXPERTS = 32
TOP_K = 4
D_FF = 512
EXPERT_BLOCK = 1024
EXPERT_SUB = 512
SWIGLU_LIMIT = 7.0
SWIGLU_ALPHA = 1.702
NORM_EPS = 1e-5

LANES = 128
HALF = D_MODEL // 2
TOKEN_TILE = 1024
RANK_TILE = 512
SUB_TILE = 256
Q_TILE = 256
ATTN_TILES = 4
KEY_SPAN = Q_TILE + 2 * WINDOW
SC_CHUNK = 128
SC_CORES = 2
SC_WORKERS = SC_CORES * 16
SC_LANES = 16
COMBINE_CHUNK = 16
VMEM_LIMIT = 56 * 1024 * 1024
NEG_BIG = -1e30
LOG2E = 1.4426950408889634
Q_SCALE = HEAD_DIM ** -0.5 * LOG2E
HALF_WORD = 16
HI_MASK = -(1 << HALF_WORD)
ROUTE_ROWS = 16

_f32 = jnp.float32
_bf16 = jnp.bfloat16


def _pack_halves(a, b):
    ua = lax.bitcast_convert_type(a.astype(_bf16).astype(_f32), jnp.int32)
    ub = lax.bitcast_convert_type(b.astype(_bf16).astype(_f32), jnp.int32)
    return ua | lax.shift_right_logical(ub, HALF_WORD)


def _unpack_halves(p):
    a = lax.bitcast_convert_type(p & HI_MASK, _f32)
    b = lax.bitcast_convert_type(lax.shift_left(p, HALF_WORD), _f32)
    return a, b


def _rms(x, g):
    ms = jnp.mean(x * x, axis=-1, keepdims=True)
    return x * lax.rsqrt(ms + NORM_EPS) * g


def _mod_kernel(c_ref, w_ref, b_ref, o_ref):
    c = c_ref[...]
    s = c * (1.0 / (1.0 + jnp.exp(-c)))
    o_ref[0] = jnp.dot(s.astype(_bf16), w_ref[0].astype(_bf16), preferred_element_type=_f32) + b_ref[0]


def _modulation(c_all, ada_w, ada_b):
    nb = c_all.shape[0]
    ncol = ada_w.shape[2] // D_MODEL
    return pl.pallas_call(
        _mod_kernel,
        out_shape=jax.ShapeDtypeStruct((DEPTH, nb, 6 * D_MODEL), _f32),
        grid=(DEPTH, ncol),
        in_specs=[
            pl.BlockSpec((nb, D_MODEL), lambda l, j: (0, 0)),
            pl.BlockSpec((1, D_MODEL, D_MODEL), lambda l, j: (l, 0, j)),
            pl.BlockSpec((1, 1, D_MODEL), lambda l, j: (l, 0, j)),
        ],
        out_specs=pl.BlockSpec((1, nb, D_MODEL), lambda l, j: (l, 0, j)),
        name="modulation",
    )(c_all, ada_w, ada_b.reshape(DEPTH, 1, 6 * D_MODEL))


def _inproj_matmul(x, sh_ref, sc_ref, ng_ref, w_ref, b_ref, *_):
    h = _rms(x, ng_ref[...]) * (1.0 + sc_ref[0]) + sh_ref[0]
    return jnp.dot(h.astype(_bf16), w_ref[...], preferred_element_type=_f32) + b_ref[...]


def _inproj_finish(rows, z, sh_ref, sc_ref, ng_ref, w_ref, b_ref, ab_ref, rc_ref, rs1_ref, rs2_ref,
                   y_ref, q_ref, k_ref, v_ref):
    f = z[:, :FOURIER_WIDTH]
    y_ref[rows, :] = jnp.dot(f.astype(_bf16), ab_ref[...], preferred_element_type=_f32).astype(_bf16)
    rc = rc_ref[rows, :]
    rs1 = rs1_ref[rows, :]
    rs2 = rs2_ref[rows, :]

    def rope(t):
        return t * rc + pltpu.roll(t, LANES - ROT_DIM // 2, 1) * rs1 + pltpu.roll(t, ROT_DIM // 2, 1) * rs2

    q0 = FOURIER_WIDTH
    for c in range(ATTN_WIDTH // LANES):
        t = z[:, q0 + c * LANES:q0 + (c + 1) * LANES]
        q_ref[rows, c * LANES:(c + 1) * LANES] = (rope(t) * Q_SCALE).astype(_bf16)
    k0 = q0 + ATTN_WIDTH
    for c in range(KV_WIDTH // LANES):
        t = z[:, k0 + c * LANES:k0 + (c + 1) * LANES]
        k_ref[rows, c * LANES:(c + 1) * LANES] = rope(t).astype(_bf16)
    v_ref[rows, :] = z[:, k0 + KV_WIDTH:].astype(_bf16)


def _sub_tiles(n_rows):
    return [pl.ds(r, SUB_TILE) for r in range(0, n_rows, SUB_TILE)]


def _inproj_kernel(x_ref, *refs):
    tiles = _sub_tiles(x_ref.shape[0])
    z = _inproj_matmul(x_ref[tiles[0], :], *refs)
    for n, rows in enumerate(tiles):
        z_next = _inproj_matmul(x_ref[tiles[n + 1], :], *refs) if n + 1 < len(tiles) else None
        _inproj_finish(rows, z, *refs)
        z = z_next


def _inproj(seq, x, sh1, sc1, norm_g, w_in, b_in, ab, rope_tabs):
    t_tokens = x.shape[0]
    tm = TOKEN_TILE
    tiles_per_seq = seq // tm
    row = lambda i: (i, 0)
    per_seq = lambda i: (i // tiles_per_seq, 0, 0)
    const = lambda i: (0, 0)
    pos = lambda i: (i % tiles_per_seq, 0)
    return pl.pallas_call(
        _inproj_kernel,
        out_shape=[jax.ShapeDtypeStruct((t_tokens, 2 * FOURIER_WIDTH), _bf16),
                   jax.ShapeDtypeStruct((t_tokens, ATTN_WIDTH), _bf16),
                   jax.ShapeDtypeStruct((t_tokens, KV_WIDTH), _bf16),
                   jax.ShapeDtypeStruct((t_tokens, KV_WIDTH), _bf16)],
        grid=(t_tokens // tm,),
        in_specs=[pl.BlockSpec((tm, D_MODEL), row),
                  pl.BlockSpec((1, 1, D_MODEL), per_seq), pl.BlockSpec((1, 1, D_MODEL), per_seq),
                  pl.BlockSpec((1, D_MODEL), const),
                  pl.BlockSpec((D_MODEL, IN_WIDTH), const), pl.BlockSpec((1, IN_WIDTH), const),
                  pl.BlockSpec((FOURIER_WIDTH, 2 * FOURIER_WIDTH), const),
                  pl.BlockSpec((tm, LANES), pos), pl.BlockSpec((tm, LANES), pos), pl.BlockSpec((tm, LANES), pos)],
        out_specs=[pl.BlockSpec((tm, 2 * FOURIER_WIDTH), row), pl.BlockSpec((tm, ATTN_WIDTH), row),
                   pl.BlockSpec((tm, KV_WIDTH), row), pl.BlockSpec((tm, KV_WIDTH), row)],
        compiler_params=pltpu.CompilerParams(dimension_semantics=("parallel",), vmem_limit_bytes=VMEM_LIMIT),
        name="inproj",
    )(x, sh1, sc1, norm_g, w_in, b_in, ab, *rope_tabs)


def _fourier_kernel(y_ref, *refs):
    tabs, o_ref = refs[:-1], refs[-1]
    quarter = y_ref.shape[0] // 4
    w = FOURIER_WIDTH
    a = [y_ref[q * quarter:(q + 1) * quarter, :w].astype(_f32) for q in range(4)]
    b = [y_ref[q * quarter:(q + 1) * quarter, w:].astype(_f32) for q in range(4)]
    a_s, a_d, a_s13, a_d13 = a[0] + a[2], a[0] - a[2], a[1] + a[3], a[1] - a[3]
    b_s, b_d, b_s13, b_d13 = b[0] + b[2], b[0] - b[2], b[1] + b[3], b[1] - b[3]
    w_re = [a_s + a_s13, a_d + b_d13, a_s - a_s13, a_d - b_d13]
    w_im = [b_s + b_s13, b_d - a_d13, b_s - b_s13, b_d + a_d13]
    for r in range(4):
        out = (jnp.dot(tabs[2 * r][...], w_re[r].astype(_bf16), preferred_element_type=_f32)
               + jnp.dot(tabs[2 * r + 1][...], w_im[r].astype(_bf16), preferred_element_type=_f32))
        for c in range(FOURIER_WIDTH // LANES):
            o_ref[c, pl.ds(r, quarter, stride=4), :] = out[:, c * LANES:(c + 1) * LANES]


def _fourier(seq, y, dft):
    t_tokens = y.shape[0]
    quarter = seq // 4
    const = lambda b: (0, 0)
    return pl.pallas_call(
        _fourier_kernel,
        out_shape=jax.ShapeDtypeStruct((FOURIER_WIDTH // LANES, t_tokens, LANES), _f32),
        grid=(t_tokens // seq,),
        in_specs=[pl.BlockSpec((seq, 2 * FOURIER_WIDTH), lambda b: (b, 0))]
        + [pl.BlockSpec((quarter, quarter), const)] * 8,
        out_specs=pl.BlockSpec((FOURIER_WIDTH // LANES, seq, LANES), lambda b: (0, b, 0)),
        compiler_params=pltpu.CompilerParams(dimension_semantics=("parallel",), vmem_limit_bytes=VMEM_LIMIT),
        name="fourier",
    )(y, *dft)


def _dft_tables(seq):
    quarter = seq // 4
    j = jnp.arange(quarter, dtype=jnp.int32)[:, None]
    m = jnp.arange(quarter, dtype=jnp.int32)[None, :]
    tabs = []
    for r in range(4):
        ang = (((4 * j + r) * m) % seq).astype(_f32) * (2.0 * jnp.pi / seq)
        tabs += [jnp.cos(ang).astype(_bf16), jnp.sin(ang).astype(_bf16)]
    return tuple(tabs)


def _channel_dft_fold(w_fmix, seq):
    c = jnp.arange(FGROUP_DIM, dtype=jnp.int32)
    ang = ((c[:, None] * c[None, :]) % FGROUP_DIM).astype(_f32) * (2.0 * jnp.pi / FGROUP_DIM)
    scale = (seq * FGROUP_DIM) ** -0.5
    hp = lax.Precision.HIGHEST
    a = jnp.einsum('cm,gmd->gcd', jnp.cos(ang) * scale, w_fmix, precision=hp)
    b = jnp.einsum('cm,gmd->gcd', -jnp.sin(ang) * scale, w_fmix, precision=hp)
    eye = jnp.eye(N_FGROUPS, dtype=_f32)
    bd = lambda m: jnp.einsum('gcd,gh->gchd', m, eye).reshape(FOURIER_WIDTH, FOURIER_WIDTH)
    return jnp.concatenate([bd(a), bd(b)], axis=1).astype(_bf16)


def _rope_tables(seq):
    half = ROT_DIM // 2
    inv_freq = jnp.power(ROPE_THETA, -jnp.arange(0, ROT_DIM, 2, dtype=_f32) / ROT_DIM)
    ang = jnp.arange(seq, dtype=_f32)[:, None] * inv_freq[None, :]
    cos, sin = jnp.cos(ang), jnp.sin(ang)
    ones = jnp.ones((seq, HEAD_DIM - ROT_DIM), _f32)
    zeros = jnp.zeros((seq, HEAD_DIM - ROT_DIM), _f32)
    zh = jnp.zeros((seq, half), _f32)
    c1 = jnp.concatenate([cos, cos, ones], axis=1)
    s1 = jnp.concatenate([-sin, zh, zeros], axis=1)
    s2 = jnp.concatenate([zh, sin, zeros], axis=1)
    rep = LANES // HEAD_DIM
    return tuple(jnp.tile(t, (1, rep)) for t in (c1, s1, s2))


def _attn_kernel(sink_ref, q_ref, k_ref, v_ref, o_ref, kpad, vtb, ot):
    seq = k_ref.shape[0]
    nqb = seq // Q_TILE
    nkb = seq // WINDOW

    @pl.when(pl.program_id(1) == 0)
    def _():
        zk = jnp.zeros((WINDOW, HEAD_DIM), _bf16)
        for h in range(N_KV_HEADS):
            kpad[h, :WINDOW, :] = zk
            kpad[h, WINDOW + seq:, :] = zk
            kpad[h, WINDOW:WINDOW + seq, :] = k_ref[:, h * HEAD_DIM:(h + 1) * HEAD_DIM]
        zv = jnp.zeros((KV_WIDTH, WINDOW), _bf16)
        vtb[0] = zv
        vtb[nkb + 1] = zv
        for j in range(nkb):
            vtb[j + 1] = v_ref[j * WINDOW:(j + 1) * WINDOW, :].astype(_f32).T.astype(_bf16)

    ka = lax.broadcasted_iota(jnp.int32, (WINDOW, WINDOW), 0)
    qc = lax.broadcasted_iota(jnp.int32, (WINDOW, WINDOW), 1)
    tri_ge = jnp.where(ka >= qc, 0.0, NEG_BIG).astype(_f32)
    tri_le = jnp.where(ka <= qc, 0.0, NEG_BIG).astype(_f32)
    zero_blk = jnp.zeros((WINDOW, WINDOW), _bf16)
    n_tiles = q_ref.shape[0] // Q_TILE
    qbs = [pl.program_id(1) * n_tiles + t for t in range(n_tiles)]
    bias_first = [jnp.where(qb == 0, NEG_BIG, tri_ge) for qb in qbs]
    bias_last = [jnp.where(qb == nqb - 1, NEG_BIG, tri_le) for qb in qbs]

    def softmax_col(blocks, sink):
        m = blocks[0]
        for b in blocks[1:]:
            m = jnp.maximum(m, b)
        m = jnp.maximum(jnp.max(m, axis=0, keepdims=True), sink)
        ps = [jnp.exp2(b - m) for b in blocks]
        tot = ps[0]
        for p in ps[1:]:
            tot = tot + p
        return ps, jnp.sum(tot, axis=0, keepdims=True) + jnp.exp2(sink - m)

    w = WINDOW

    def scores(t, h):
        r0 = pl.multiple_of(qbs[t] * Q_TILE, Q_TILE)
        kh = kpad[h, pl.ds(r0, KEY_SPAN), :]
        q3 = jnp.concatenate([q_ref[t * Q_TILE:(t + 1) * Q_TILE, g * HEAD_DIM:(g + 1) * HEAD_DIM]
                              for g in range(Q_PER_KV * h, Q_PER_KV * (h + 1))], axis=0)
        return lax.dot_general(kh, q3, (((1,), (1,)), ((), ())), preferred_element_type=_f32)

    work = [(t, h) for t in range(n_tiles) for h in range(N_KV_HEADS)]
    s = scores(*work[0])
    for n, (t, h) in enumerate(work):
        s_next = scores(*work[n + 1]) if n + 1 < len(work) else None
        cols, dens = [], []
        for i in range(Q_PER_KV):
            sink = sink_ref[Q_PER_KV * h + i] * LOG2E
            c0 = 2 * i * w
            c1 = c0 + w
            p0, l0 = softmax_col([s[0:w, c0:c1] + bias_first[t], s[w:2 * w, c0:c1], s[2 * w:3 * w, c0:c1] + tri_le],
                                 sink)
            p1, l1 = softmax_col([s[w:2 * w, c1:c1 + w] + tri_ge, s[2 * w:3 * w, c1:c1 + w],
                                  s[3 * w:, c1:c1 + w] + bias_last[t]], sink)
            cols.append(jnp.concatenate([p.astype(_bf16) for p in p0] + [zero_blk], axis=0))
            cols.append(jnp.concatenate([zero_blk] + [p.astype(_bf16) for p in p1], axis=0))
            dens += [l0, l1]
        pt = jnp.concatenate(cols, axis=1)
        kb0 = qbs[t] * (Q_TILE // WINDOW)
        vth = jnp.concatenate([vtb[kb0 + j, h * HEAD_DIM:(h + 1) * HEAD_DIM, :]
                               for j in range(KEY_SPAN // WINDOW)], axis=1)
        o_t = jnp.dot(vth, pt, preferred_element_type=_f32) / jnp.concatenate(dens, axis=1)
        for i in range(Q_PER_KV):
            g = Q_PER_KV * h + i
            ot[t, g * HEAD_DIM:(g + 1) * HEAD_DIM, :] = o_t[:, i * Q_TILE:(i + 1) * Q_TILE]
        if h == N_KV_HEADS - 1:
            for c in range(ATTN_WIDTH // LANES):
                o_ref[t * Q_TILE:(t + 1) * Q_TILE, c * LANES:(c + 1) * LANES] = (
                    ot[t, c * LANES:(c + 1) * LANES, :].T.astype(_bf16))
        s = s_next


def _attention(seq, q, k, v, sinks):
    t_tokens = q.shape[0]
    rows = ATTN_TILES * Q_TILE
    steps = seq // rows
    return pl.pallas_call(
        _attn_kernel,
        out_shape=jax.ShapeDtypeStruct((t_tokens, ATTN_WIDTH), _bf16),
        grid=(t_tokens // seq, steps),
        in_specs=[pl.BlockSpec(memory_space=pltpu.SMEM),
                  pl.BlockSpec((rows, ATTN_WIDTH), lambda b, j: (b * steps + j, 0)),
                  pl.BlockSpec((seq, KV_WIDTH), lambda b, j: (b, 0)),
                  pl.BlockSpec((seq, KV_WIDTH), lambda b, j: (b, 0))],
        out_specs=pl.BlockSpec((rows, ATTN_WIDTH), lambda b, j: (b * steps + j, 0)),
        scratch_shapes=[pltpu.VMEM((N_KV_HEADS, seq + 2 * WINDOW, HEAD_DIM), _bf16),
                        pltpu.VMEM((seq // WINDOW + 2, KV_WIDTH, WINDOW), _bf16),
                        pltpu.VMEM((ATTN_TILES, ATTN_WIDTH, Q_TILE), _f32)],
        compiler_params=pltpu.CompilerParams(dimension_semantics=("parallel", "arbitrary"),
                                             vmem_limit_bytes=VMEM_LIMIT),
        name="attention",
    )(sinks, q, k, v)


def _outproj_kernel(x_ref, mf_ref, at_ref, g1_ref, sh_ref, sc_ref, ng_ref, wof_ref, woa_ref, bo_ref,
                    wr_ref, br_ref, tri_ref, x1_ref, hp_ref, rt_ref, cnt_ref, carry_ref):
    i = pl.program_id(0)

    @pl.when(i == 0)
    def _():
        carry_ref[...] = jnp.zeros_like(carry_ref)

    def project(rows):
        mf = jnp.concatenate([mf_ref[c, rows, :] for c in range(FOURIER_WIDTH // LANES)], axis=1)
        return (jnp.dot(mf.astype(_bf16), wof_ref[...], preferred_element_type=_f32)
                + jnp.dot(at_ref[rows, :], woa_ref[...], preferred_element_type=_f32) + bo_ref[...])

    tiles = _sub_tiles(x_ref.shape[0])
    mix = project(tiles[0])
    logit_cols = []
    for n, rows in enumerate(tiles):
        mix_next = project(tiles[n + 1]) if n + 1 < len(tiles) else None
        x1 = x_ref[rows, :] + g1_ref[0] * mix
        x1_ref[rows, :] = x1
        h = _rms(x1, ng_ref[...]) * (1.0 + sc_ref[0]) + sh_ref[0]
        hp_ref[rows, :] = _pack_halves(h[:, :HALF], h[:, HALF:])
        logit_cols.append(lax.dot_general(wr_ref[...], h.astype(_bf16), (((1,), (1,)), ((), ())),
                                          preferred_element_type=_f32))
        mix = mix_next
    logits = jnp.concatenate(logit_cols, axis=1) + br_ref[...]
    tm = logits.shape[1]
    erow = lax.broadcasted_iota(jnp.int32, logits.shape, 0)
    work = logits
    hots, vals, idxs = [], [], []
    for _k in range(TOP_K):
        mx = jnp.max(work, axis=0, keepdims=True)
        ix = jnp.min(jnp.where(work == mx, erow, N_EXPERTS), axis=0, keepdims=True)
        hot = erow == ix
        work = jnp.where(hot, -jnp.inf, work)
        hots.append(hot)
        vals.append(mx)
        idxs.append(ix)
    es = [jnp.exp(v - vals[0]) for v in vals]
    den = es[0] + es[1] + es[2] + es[3]
    member = jnp.zeros(logits.shape, _f32)
    for hot in hots:
        member = member + hot.astype(_f32)
    carry = carry_ref[...]
    parts = []
    for c0 in range(0, tm, RANK_TILE):
        m = member[:, c0:c0 + RANK_TILE]
        parts.append(jnp.dot(m.astype(_bf16), tri_ref[...], preferred_element_type=_f32)
                     + jnp.concatenate([carry] * (RANK_TILE // LANES), axis=1))
        carry = carry + jnp.broadcast_to(jnp.sum(m, axis=1, keepdims=True), carry.shape)
    before = jnp.concatenate(parts, axis=1)
    r16 = lax.broadcasted_iota(jnp.int32, (ROUTE_ROWS, tm), 0)
    slab = jnp.zeros((ROUTE_ROWS, tm), jnp.int32)
    for k in range(TOP_K):
        rank = jnp.sum(jnp.where(hots[k], before, 0.0), axis=0, keepdims=True).astype(jnp.int32)
        slab = jnp.where(r16 == k, idxs[k], slab)
        slab = jnp.where(r16 == TOP_K + k, rank, slab)
        slab = jnp.where(r16 == 2 * TOP_K + k, lax.bitcast_convert_type(es[k] / den, jnp.int32), slab)
    rt_ref[...] = slab
    carry_ref[...] = carry
    cnt_ref[...] = carry.astype(jnp.int32)


def _outproj(seq, x, mf, attn, g1, sh2, sc2, norm_g, wo_f, wo_a, b_o, w_r, b_r, tri):
    t_tokens = x.shape[0]
    tm = TOKEN_TILE
    tiles_per_seq = seq // tm
    row = lambda i: (i, 0)
    per_seq = lambda i: (i // tiles_per_seq, 0, 0)
    const = lambda i: (0, 0)
    return pl.pallas_call(
        _outproj_kernel,
        out_shape=[jax.ShapeDtypeStruct((t_tokens, D_MODEL), _f32),
                   jax.ShapeDtypeStruct((t_tokens, HALF), jnp.int32),
                   jax.ShapeDtypeStruct((ROUTE_ROWS, t_tokens), jnp.int32),
                   jax.ShapeDtypeStruct((N_EXPERTS, LANES), jnp.int32)],
        grid=(t_tokens // tm,),
        in_specs=[pl.BlockSpec((tm, D_MODEL), row),
                  pl.BlockSpec((FOURIER_WIDTH // LANES, tm, LANES), lambda i: (0, i, 0)),
                  pl.BlockSpec((tm, ATTN_WIDTH), row),
                  pl.BlockSpec((1, 1, D_MODEL), per_seq), pl.BlockSpec((1, 1, D_MODEL), per_seq),
                  pl.BlockSpec((1, 1, D_MODEL), per_seq),
                  pl.BlockSpec((1, D_MODEL), const),
                  pl.BlockSpec((FOURIER_WIDTH, D_MODEL), const), pl.BlockSpec((ATTN_WIDTH, D_MODEL), const),
                  pl.BlockSpec((1, D_MODEL), const),
                  pl.BlockSpec((N_EXPERTS, D_MODEL), const), pl.BlockSpec((N_EXPERTS, tm), const),
                  pl.BlockSpec((RANK_TILE, RANK_TILE), const)],
        out_specs=[pl.BlockSpec((tm, D_MODEL), row), pl.BlockSpec((tm, HALF), row),
                   pl.BlockSpec((ROUTE_ROWS, tm), lambda i: (0, i)),
                   pl.BlockSpec((N_EXPERTS, LANES), const)],
        scratch_shapes=[pltpu.VMEM((N_EXPERTS, LANES), _f32)],
        compiler_params=pltpu.CompilerParams(dimension_semantics=("arbitrary",), vmem_limit_bytes=VMEM_LIMIT),
        name="outproj",
    )(x, mf, attn, g1, sh2, sc2, norm_g, wo_f, wo_a, b_o, w_r, b_r, tri)


def _expert_kernel(blk_ref, bexp_ref, active_ref, x_ref, wgu_ref, bgu_ref, wd_ref, bd_ref, y_ref, wgu_bf, wd_bf):
    i = pl.program_id(0)
    prev = bexp_ref[jnp.maximum(i - 1, 0)]

    @pl.when((i == 0) | (bexp_ref[i] != prev))
    def _():
        wgu_bf[...] = wgu_ref[...].astype(_bf16)
        wd_bf[...] = wd_ref[...].astype(_bf16)

    def gate_up(rows):
        xa, xb = _unpack_halves(x_ref[rows, :])
        return (jnp.dot(xa.astype(_bf16), wgu_bf[:HALF, :], preferred_element_type=_f32)
                + jnp.dot(xb.astype(_bf16), wgu_bf[HALF:, :], preferred_element_type=_f32) + bgu_ref[...])

    def activate(gu):
        g = jnp.minimum(gu[:, :D_FF], SWIGLU_LIMIT)
        u = jnp.clip(gu[:, D_FF:], -SWIGLU_LIMIT, SWIGLU_LIMIT)
        return ((u + 1.0) * (g * (1.0 / (1.0 + jnp.exp(-SWIGLU_ALPHA * g))))).astype(_bf16)

    def down(rows, act):
        y = jnp.dot(act, wd_bf[...], preferred_element_type=_f32) + bd_ref[...]
        y_ref[rows, :] = _pack_halves(y[:, :HALF], y[:, HALF:])

    @pl.when(active_ref[i] > 0)
    def _():
        tiles = [pl.ds(r, EXPERT_SUB) for r in range(0, EXPERT_BLOCK, EXPERT_SUB)]
        gu = gate_up(tiles[0])
        for n, rows in enumerate(tiles):
            gu_next = gate_up(tiles[n + 1]) if n + 1 < len(tiles) else None
            down(rows, activate(gu))
            gu = gu_next


def _experts(layer, xs, blk, bexp, active, w_gu, b_gu, w_down, b_down):
    p_rows = xs.shape[0]
    nblk = p_rows // EXPERT_BLOCK
    grid_spec = pltpu.PrefetchScalarGridSpec(
        num_scalar_prefetch=3,
        grid=(nblk,),
        in_specs=[pl.BlockSpec((EXPERT_BLOCK, HALF), lambda i, blk, be, ac: (blk[i], 0)),
                  pl.BlockSpec((None, None, D_MODEL, 2 * D_FF), lambda i, blk, be, ac: (layer, be[i], 0, 0)),
                  pl.BlockSpec((None, 1, 2 * D_FF), lambda i, blk, be, ac: (be[i], 0, 0)),
                  pl.BlockSpec((None, None, D_FF, D_MODEL), lambda i, blk, be, ac: (layer, be[i], 0, 0)),
                  pl.BlockSpec((None, 1, D_MODEL), lambda i, blk, be, ac: (be[i], 0, 0))],
        out_specs=pl.BlockSpec((EXPERT_BLOCK, HALF), lambda i, blk, be, ac: (blk[i], 0)),
        scratch_shapes=[pltpu.VMEM((D_MODEL, 2 * D_FF), _bf16), pltpu.VMEM((D_FF, D_MODEL), _bf16)])
    return pl.pallas_call(
        _expert_kernel,
        out_shape=jax.ShapeDtypeStruct((p_rows, HALF), jnp.int32),
        grid_spec=grid_spec,
        compiler_params=pltpu.CompilerParams(dimension_semantics=("arbitrary",), vmem_limit_bytes=VMEM_LIMIT),
        name="experts",
    )(blk, bexp, active, xs, w_gu, b_gu.reshape(N_EXPERTS, 1, 2 * D_FF), w_down,
      b_down.reshape(N_EXPERTS, 1, D_MODEL))


def _sc_mesh():
    return plsc.VectorSubcoreMesh(core_axis_name="c", subcore_axis_name="s")


def _sc_worker():
    return lax.axis_index("s") * SC_CORES + lax.axis_index("c")


def _dispatch(hp, dest, p_rows):
    t_tokens = hp.shape[0]
    per_worker = t_tokens // (SC_CHUNK * SC_WORKERS)

    @functools.partial(
        pl.kernel, mesh=_sc_mesh(),
        out_type=jax.ShapeDtypeStruct((p_rows, HALF), jnp.int32),
        scratch_types=[pltpu.VMEM((TOP_K, SC_CHUNK), jnp.int32),
                       pltpu.VMEM((SC_CHUNK, HALF), jnp.int32),
                       pltpu.SemaphoreType.DMA],
        name="dispatch")
    def run(hp_hbm, dest_hbm, xs_hbm, idx_v, rows_v, sem):
        base = _sc_worker() * per_worker

        @pl.loop(0, per_worker)
        def _(j):
            chunk = base + j
            for k in range(TOP_K):
                pltpu.sync_copy(dest_hbm.at[pl.ds(k * t_tokens + chunk * SC_CHUNK, SC_CHUNK)], idx_v.at[k])
            pltpu.sync_copy(hp_hbm.at[pl.ds(chunk * SC_CHUNK, SC_CHUNK)], rows_v)
            copies = [pltpu.async_copy(rows_v, xs_hbm.at[idx_v.at[k]], sem) for k in range(TOP_K)]
            for cp in copies:
                cp.wait()

    return run(hp, dest)


def _combine_rows(y, dest, w, x1, g2, seq):
    t_tokens = x1.shape[0]
    cc = COMBINE_CHUNK
    n_chunks = t_tokens // cc
    per_worker = n_chunks // SC_WORKERS
    own = per_worker * cc
    assert per_worker % 2 == 0 and seq % own == 0
    n_vec = HALF // SC_LANES
    row_buf = pltpu.VMEM((TOP_K, cc, HALF), jnp.int32)
    x_buf = pltpu.VMEM((cc, D_MODEL), _f32)

    @functools.partial(
        pl.kernel, mesh=_sc_mesh(),
        out_type=jax.ShapeDtypeStruct((t_tokens, D_MODEL), _f32),
        scratch_types=[pltpu.VMEM((TOP_K * own,), jnp.int32),
                       pltpu.VMEM((TOP_K * own,), _f32),
                       row_buf, row_buf, x_buf, x_buf,
                       pltpu.VMEM((D_MODEL,), _f32),
                       pltpu.SemaphoreType.DMA, pltpu.SemaphoreType.DMA,
                       pltpu.SemaphoreType.DMA, pltpu.SemaphoreType.DMA],
        compiler_params=pltpu.CompilerParams(needs_layout_passes=False),
        name="combine_rows")
    def run(y_hbm, dest_hbm, w_hbm, x1_hbm, g2_hbm, x2_hbm,
            idx_v, w_v, rows0, rows1, xb0, xb1, g_v, in0, in1, out0, out1):
        base = _sc_worker() * per_worker
        bufs = ((rows0, xb0, in0, out0), (rows1, xb1, in1, out1))
        for k in range(TOP_K):
            pltpu.sync_copy(dest_hbm.at[pl.ds(k * t_tokens + base * cc, own)], idx_v.at[pl.ds(k * own, own)])
            pltpu.sync_copy(w_hbm.at[pl.ds(k * t_tokens + base * cc, own)], w_v.at[pl.ds(k * own, own)])
        pltpu.sync_copy(g2_hbm.at[(base * cc) // seq], g_v)

        def loads(c, b):
            rows, xb, isem, _ = bufs[b]
            cps = [pltpu.make_async_copy(y_hbm.at[idx_v.at[pl.ds(k * own + c * cc, cc)]], rows.at[k], isem)
                   for k in range(TOP_K)]
            cps.append(pltpu.make_async_copy(x1_hbm.at[pl.ds((base + c) * cc, cc)], xb, isem))
            return cps

        def store(c, b):
            _, xb, _, osem = bufs[b]
            return pltpu.make_async_copy(xb, x2_hbm.at[pl.ds((base + c) * cc, cc)], osem)

        def combine(c, b):
            rows, xb, _, _ = bufs[b]

            @pl.loop(0, cc)
            def _(t):
                ws = [plsc.load_gather(w_v, [jnp.full((SC_LANES,), k * own + c * cc + t, jnp.int32)])
                      for k in range(TOP_K)]

                @plsc.parallel_loop(0, n_vec)
                def _(j):
                    lo = pl.ds(j * SC_LANES, SC_LANES)
                    hi = pl.ds(HALF + j * SC_LANES, SC_LANES)
                    acc_a = None
                    acc_b = None
                    for k in range(TOP_K):
                        p = rows[k, t, lo]
                        a = lax.bitcast_convert_type(p & HI_MASK, _f32) * ws[k]
                        b_ = lax.bitcast_convert_type(p << HALF_WORD, _f32) * ws[k]
                        acc_a = a if acc_a is None else acc_a + a
                        acc_b = b_ if acc_b is None else acc_b + b_
                    xb[t, lo] = xb[t, lo] + g_v[lo] * acc_a
                    xb[t, hi] = xb[t, hi] + g_v[hi] * acc_b

        for cp in loads(0, 0):
            cp.start()

        @pl.loop(0, per_worker // 2)
        def _(c2):
            for b in range(2):
                c = 2 * c2 + b

                @pl.when(c >= 1)
                def _():
                    store(c - 1, 1 - b).wait()

                @pl.when(c + 1 < per_worker)
                def _():
                    for cp in loads(c + 1, 1 - b):
                        cp.start()

                for cp in loads(c, b):
                    cp.wait()
                combine(c, b)
                store(c, b).start()

        store(per_worker - 1, 1).wait()

    return run(y, dest, w, x1, g2)


def _final_kernel(x_ref, ng_ref, o_ref):
    o_ref[...] = _rms(x_ref[...], ng_ref[...])


def _final(x, final_g):
    t_tokens = x.shape[0]
    tm = TOKEN_TILE
    row = lambda i: (i, 0)
    return pl.pallas_call(
        _final_kernel,
        out_shape=jax.ShapeDtypeStruct((t_tokens, D_MODEL), _f32),
        grid=(t_tokens // tm,),
        in_specs=[pl.BlockSpec((tm, D_MODEL), row), pl.BlockSpec((1, D_MODEL), lambda i: (0, 0))],
        out_specs=pl.BlockSpec((tm, D_MODEL), row),
        compiler_params=pltpu.CompilerParams(dimension_semantics=("parallel",), vmem_limit_bytes=VMEM_LIMIT),
        name="final",
    )(x, final_g)


def _plan(route, counts, nblk):
    cnt = counts[:, 0]
    padded = ((cnt + EXPERT_BLOCK - 1) // EXPERT_BLOCK) * EXPERT_BLOCK
    pad_end = jnp.cumsum(padded)
    pad_start = pad_end - padded
    expert = jnp.arange(N_EXPERTS, dtype=jnp.int32)[:, None, None]
    start_of = jnp.sum(jnp.where(route[None, :TOP_K] == expert, pad_start[:, None, None], 0), axis=0)
    dest = start_of + route[TOP_K:2 * TOP_K]
    nact = jnp.maximum(pad_end[-1] // EXPERT_BLOCK, 1)
    step = jnp.arange(nblk, dtype=jnp.int32)
    blk = jnp.minimum(step, nact - 1)
    bexp = jnp.minimum(jnp.sum(pad_end[None, :] <= (blk * EXPERT_BLOCK)[:, None], axis=1), N_EXPERTS - 1)
    active = step < pad_end[-1] // EXPERT_BLOCK
    w = lax.bitcast_convert_type(route[2 * TOP_K:3 * TOP_K], _f32)
    return (dest.reshape(-1), w.reshape(-1), blk.astype(jnp.int32), bexp.astype(jnp.int32),
            active.astype(jnp.int32))


def _trunk(x_all, c_all, norm1_g, ada_w, ada_b, w_in, b_in, w_fmix, sinks, w_o, b_o,
           norm2_g, w_router, b_router, w_gu, b_gu, w_down, b_down, final_g):
    nb, seq, _ = x_all.shape
    t_tokens = nb * seq
    assert seq % TOKEN_TILE == 0 and seq % (ATTN_TILES * Q_TILE) == 0 and seq >= KEY_SPAN
    assert t_tokens % (SC_CHUNK * SC_WORKERS) == 0 and seq % COMBINE_CHUNK == 0
    nblk = (t_tokens * TOP_K) // EXPERT_BLOCK + N_EXPERTS
    p_rows = nblk * EXPERT_BLOCK

    mod = _modulation(c_all, ada_w, ada_b)
    mod = mod.reshape(DEPTH, nb, 6, 1, D_MODEL)
    rope_tabs = _rope_tables(seq)
    dft = _dft_tables(seq)
    tri = jnp.triu(jnp.ones((RANK_TILE, RANK_TILE), _f32), 1).astype(_bf16)

    x = x_all.reshape(t_tokens, D_MODEL)
    for l in range(DEPTH):
        sh1, sc1, g1, sh2, sc2, g2 = (mod[l, :, j] for j in range(6))
        ab = _channel_dft_fold(w_fmix[l], seq)
        y, q, k, v = _inproj(seq, x, sh1, sc1, norm1_g[l].reshape(1, D_MODEL), w_in[l].astype(_bf16),
                             b_in[l].reshape(1, IN_WIDTH), ab, rope_tabs)
        mf = _fourier(seq, y, dft)
        attn = _attention(seq, q, k, v, sinks[l])
        w_o_bf = w_o[l].astype(_bf16)
        w_r = w_router[l].T.astype(_bf16)
        b_r = jnp.broadcast_to(b_router[l][:, None], (N_EXPERTS, TOKEN_TILE))
        x1, hp, route, counts = _outproj(seq, x, mf, attn, g1, sh2, sc2, norm2_g[l].reshape(1, D_MODEL),
                                         w_o_bf[:FOURIER_WIDTH], w_o_bf[FOURIER_WIDTH:],
                                         b_o[l].reshape(1, D_MODEL), w_r, b_r, tri)
        dest, w, blk, bexp, active = _plan(route, counts, nblk)
        xs = _dispatch(hp, dest, p_rows)
        ys = _experts(l, xs, blk, bexp, active, w_gu, b_gu[l], w_down, b_down[l])
        x = _combine_rows(ys, dest, w, x1, g2.reshape(nb, D_MODEL), seq)
    out = _final(x, final_g.reshape(1, D_MODEL))
    return out.reshape(nb, seq, D_MODEL)


def kernel(x_prompt, x_sample, c_prompt, c_sample, norm1_g, ada_w, ada_b, w_in, b_in, w_fmix, sinks, w_o, b_o,
           norm2_g, w_router, b_router, w_gu, b_gu, w_down, b_down, final_g):
    ws = (norm1_g, ada_w, ada_b, w_in, b_in, w_fmix, sinks, w_o, b_o,
          norm2_g, w_router, b_router, w_gu, b_gu, w_down, b_down, final_g)
    return _trunk(x_prompt, c_prompt, *ws), _trunk(x_sample, c_sample, *ws)
```

```python
import functools

import jax
import jax.numpy as jnp
from jax import lax
from jax.experimental import pallas as pl
from jax.experimental.pallas import tpu as pltpu
from jax.experimental.pallas import tpu_sc as plsc

D_MODEL = 1024
DEPTH = 4
FOURIER_WIDTH = 256
N_FGROUPS = 4
FGROUP_DIM = 64
HEAD_DIM = 64
N_Q_HEADS = 12
N_KV_HEADS = 4
Q_PER_KV = 3
ATTN_WIDTH = N_Q_HEADS * HEAD_DIM
KV_WIDTH = N_KV_HEADS * HEAD_DIM
IN_WIDTH = FOURIER_WIDTH + ATTN_WIDTH + 2 * KV_WIDTH
WINDOW = 128
ROPE_THETA = 500000.0
ROT_DIM = 16
N_EXPERTS = 32
TOP_K = 4
D_FF = 512
EXPERT_BLOCK = 1024
EXPERT_SUB = 512
SWIGLU_LIMIT = 7.0
SWIGLU_ALPHA = 1.702
NORM_EPS = 1e-5

LANES = 128
HALF = D_MODEL // 2
TOKEN_TILE = 1024
RANK_TILE = 512
SUB_TILE = 256
Q_TILE = 256
ATTN_TILES = 8
KEY_SPAN = Q_TILE + 2 * WINDOW
SC_CHUNK = 128
SC_CORES = 2
SC_WORKERS = SC_CORES * 16
SC_LANES = 16
COMBINE_CHUNK = 16
VMEM_LIMIT = 56 * 1024 * 1024
NEG_BIG = -1e30
LOG2E = 1.4426950408889634
Q_SCALE = HEAD_DIM ** -0.5 * LOG2E
HALF_WORD = 16
HI_MASK = -(1 << HALF_WORD)
ROUTE_ROWS = 16

_f32 = jnp.float32
_bf16 = jnp.bfloat16


def _pack_halves(a, b):
    ua = lax.bitcast_convert_type(a.astype(_bf16).astype(_f32), jnp.int32)
    ub = lax.bitcast_convert_type(b.astype(_bf16).astype(_f32), jnp.int32)
    return ua | lax.shift_right_logical(ub, HALF_WORD)


def _unpack_halves(p):
    a = lax.bitcast_convert_type(p & HI_MASK, _f32)
    b = lax.bitcast_convert_type(lax.shift_left(p, HALF_WORD), _f32)
    return a, b


def _rms(x, g):
    ms = jnp.mean(x * x, axis=-1, keepdims=True)
    return x * lax.rsqrt(ms + NORM_EPS) * g


def _mod_kernel(c_ref, w_ref, b_ref, o_ref):
    c = c_ref[...]
    s = c * (1.0 / (1.0 + jnp.exp(-c)))
    o_ref[0] = jnp.dot(s.astype(_bf16), w_ref[0].astype(_bf16), preferred_element_type=_f32) + b_ref[0]


def _modulation(c_all, ada_w, ada_b):
    nb = c_all.shape[0]
    ncol = ada_w.shape[2] // D_MODEL
    return pl.pallas_call(
        _mod_kernel,
        out_shape=jax.ShapeDtypeStruct((DEPTH, nb, 6 * D_MODEL), _f32),
        grid=(DEPTH, ncol),
        in_specs=[
            pl.BlockSpec((nb, D_MODEL), lambda l, j: (0, 0)),
            pl.BlockSpec((1, D_MODEL, D_MODEL), lambda l, j: (l, 0, j)),
            pl.BlockSpec((1, 1, D_MODEL), lambda l, j: (l, 0, j)),
        ],
        out_specs=pl.BlockSpec((1, nb, D_MODEL), lambda l, j: (l, 0, j)),
        name="modulation",
    )(c_all, ada_w, ada_b.reshape(DEPTH, 1, 6 * D_MODEL))


def _inproj_matmul(x, sh_ref, sc_ref, ng_ref, w_ref, b_ref, *_):
    h = _rms(x, ng_ref[...]) * (1.0 + sc_ref[0]) + sh_ref[0]
    return jnp.dot(h.astype(_bf16), w_ref[...], preferred_element_type=_f32) + b_ref[...]


def _inproj_finish(rows, z, sh_ref, sc_ref, ng_ref, w_ref, b_ref, ab_ref, rc_ref, rs1_ref, rs2_ref,
                   y_ref, q_ref, k_ref, v_ref):
    f = z[:, :FOURIER_WIDTH]
    y_ref[rows, :] = jnp.dot(f.astype(_bf16), ab_ref[...], preferred_element_type=_f32).astype(_bf16)
    rc = rc_ref[rows, :]
    rs1 = rs1_ref[rows, :]
    rs2 = rs2_ref[rows, :]

    def rope(t):
        return t * rc + pltpu.roll(t, LANES - ROT_DIM // 2, 1) * rs1 + pltpu.roll(t, ROT_DIM // 2, 1) * rs2

    q0 = FOURIER_WIDTH
    for c in range(ATTN_WIDTH // LANES):
        t = z[:, q0 + c * LANES:q0 + (c + 1) * LANES]
        q_ref[rows, c * LANES:(c + 1) * LANES] = (rope(t) * Q_SCALE).astype(_bf16)
    k0 = q0 + ATTN_WIDTH
    for c in range(KV_WIDTH // LANES):
        t = z[:, k0 + c * LANES:k0 + (c + 1) * LANES]
        k_ref[rows, c * LANES:(c + 1) * LANES] = rope(t).astype(_bf16)
    v_ref[rows, :] = z[:, k0 + KV_WIDTH:].astype(_bf16)


def _sub_tiles(n_rows):
    return [pl.ds(r, SUB_TILE) for r in range(0, n_rows, SUB_TILE)]


def _inproj_kernel(x_ref, *refs):
    tiles = _sub_tiles(x_ref.shape[0])
    z = _inproj_matmul(x_ref[tiles[0], :], *refs)
    for n, rows in enumerate(tiles):
        z_next = _inproj_matmul(x_ref[tiles[n + 1], :], *refs) if n + 1 < len(tiles) else None
        _inproj_finish(rows, z, *refs)
        z = z_next


def _inproj(seq, x, sh1, sc1, norm_g, w_in, b_in, ab, rope_tabs):
    t_tokens = x.shape[0]
    tm = TOKEN_TILE
    tiles_per_seq = seq // tm
    row = lambda i: (i, 0)
    per_seq = lambda i: (i // tiles_per_seq, 0, 0)
    const = lambda i: (0, 0)
    pos = lambda i: (i % tiles_per_seq, 0)
    return pl.pallas_call(
        _inproj_kernel,
        out_shape=[jax.ShapeDtypeStruct((t_tokens, 2 * FOURIER_WIDTH), _bf16),
                   jax.ShapeDtypeStruct((t_tokens, ATTN_WIDTH), _bf16),
                   jax.ShapeDtypeStruct((t_tokens, KV_WIDTH), _bf16),
                   jax.ShapeDtypeStruct((t_tokens, KV_WIDTH), _bf16)],
        grid=(t_tokens // tm,),
        in_specs=[pl.BlockSpec((tm, D_MODEL), row),
                  pl.BlockSpec((1, 1, D_MODEL), per_seq), pl.BlockSpec((1, 1, D_MODEL), per_seq),
                  pl.BlockSpec((1, D_MODEL), const),
                  pl.BlockSpec((D_MODEL, IN_WIDTH), const), pl.BlockSpec((1, IN_WIDTH), const),
                  pl.BlockSpec((FOURIER_WIDTH, 2 * FOURIER_WIDTH), const),
                  pl.BlockSpec((tm, LANES), pos), pl.BlockSpec((tm, LANES), pos), pl.BlockSpec((tm, LANES), pos)],
        out_specs=[pl.BlockSpec((tm, 2 * FOURIER_WIDTH), row), pl.BlockSpec((tm, ATTN_WIDTH), row),
                   pl.BlockSpec((tm, KV_WIDTH), row), pl.BlockSpec((tm, KV_WIDTH), row)],
        compiler_params=pltpu.CompilerParams(dimension_semantics=("parallel",), vmem_limit_bytes=VMEM_LIMIT),
        name="inproj",
    )(x, sh1, sc1, norm_g, w_in, b_in, ab, *rope_tabs)


def _fourier_kernel(y_ref, *refs):
    tabs, o_ref = refs[:-1], refs[-1]
    quarter = y_ref.shape[0] // 4
    w = FOURIER_WIDTH
    a = [y_ref[q * quarter:(q + 1) * quarter, :w].astype(_f32) for q in range(4)]
    b = [y_ref[q * quarter:(q + 1) * quarter, w:].astype(_f32) for q in range(4)]
    a_s, a_d, a_s13, a_d13 = a[0] + a[2], a[0] - a[2], a[1] + a[3], a[1] - a[3]
    b_s, b_d, b_s13, b_d13 = b[0] + b[2], b[0] - b[2], b[1] + b[3], b[1] - b[3]
    w_re = [a_s + a_s13, a_d + b_d13, a_s - a_s13, a_d - b_d13]
    w_im = [b_s + b_s13, b_d - a_d13, b_s - b_s13, b_d + a_d13]
    for r in range(4):
        out = (jnp.dot(tabs[2 * r][...], w_re[r].astype(_bf16), preferred_element_type=_f32)
               + jnp.dot(tabs[2 * r + 1][...], w_im[r].astype(_bf16), preferred_element_type=_f32))
        for c in range(FOURIER_WIDTH // LANES):
            o_ref[c, pl.ds(r, quarter, stride=4), :] = out[:, c * LANES:(c + 1) * LANES]


def _fourier(seq, y, dft):
    t_tokens = y.shape[0]
    quarter = seq // 4
    const = lambda b: (0, 0)
    return pl.pallas_call(
        _fourier_kernel,
        out_shape=jax.ShapeDtypeStruct((FOURIER_WIDTH // LANES, t_tokens, LANES), _f32),
        grid=(t_tokens // seq,),
        in_specs=[pl.BlockSpec((seq, 2 * FOURIER_WIDTH), lambda b: (b, 0))]
        + [pl.BlockSpec((quarter, quarter), const)] * 8,
        out_specs=pl.BlockSpec((FOURIER_WIDTH // LANES, seq, LANES), lambda b: (0, b, 0)),
        compiler_params=pltpu.CompilerParams(dimension_semantics=("parallel",), vmem_limit_bytes=VMEM_LIMIT),
        name="fourier",
    )(y, *dft)


def _dft_tables(seq):
    quarter = seq // 4
    j = jnp.arange(quarter, dtype=jnp.int32)[:, None]
    m = jnp.arange(quarter, dtype=jnp.int32)[None, :]
    tabs = []
    for r in range(4):
        ang = (((4 * j + r) * m) % seq).astype(_f32) * (2.0 * jnp.pi / seq)
        tabs += [jnp.cos(ang).astype(_bf16), jnp.sin(ang).astype(_bf16)]
    return tuple(tabs)


def _channel_dft_fold(w_fmix, seq):
    c = jnp.arange(FGROUP_DIM, dtype=jnp.int32)
    ang = ((c[:, None] * c[None, :]) % FGROUP_DIM).astype(_f32) * (2.0 * jnp.pi / FGROUP_DIM)
    scale = (seq * FGROUP_DIM) ** -0.5
    hp = lax.Precision.HIGHEST
    a = jnp.einsum('cm,gmd->gcd', jnp.cos(ang) * scale, w_fmix, precision=hp)
    b = jnp.einsum('cm,gmd->gcd', -jnp.sin(ang) * scale, w_fmix, precision=hp)
    eye = jnp.eye(N_FGROUPS, dtype=_f32)
    bd = lambda m: jnp.einsum('gcd,gh->gchd', m, eye).reshape(FOURIER_WIDTH, FOURIER_WIDTH)
    return jnp.concatenate([bd(a), bd(b)], axis=1).astype(_bf16)


def _rope_tables(seq):
    half = ROT_DIM // 2
    inv_freq = jnp.power(ROPE_THETA, -jnp.arange(0, ROT_DIM, 2, dtype=_f32) / ROT_DIM)
    ang = jnp.arange(seq, dtype=_f32)[:, None] * inv_freq[None, :]
    cos, sin = jnp.cos(ang), jnp.sin(ang)
    ones = jnp.ones((seq, HEAD_DIM - ROT_DIM), _f32)
    zeros = jnp.zeros((seq, HEAD_DIM - ROT_DIM), _f32)
    zh = jnp.zeros((seq, half), _f32)
    c1 = jnp.concatenate([cos, cos, ones], axis=1)
    s1 = jnp.concatenate([-sin, zh, zeros], axis=1)
    s2 = jnp.concatenate([zh, sin, zeros], axis=1)
    rep = LANES // HEAD_DIM
    return tuple(jnp.tile(t, (1, rep)) for t in (c1, s1, s2))


def _attn_kernel(sink_ref, q_ref, k_ref, v_ref, o_ref, kpad, vtb, ot):
    seq = k_ref.shape[0]
    nqb = seq // Q_TILE
    nkb = seq // WINDOW

    @pl.when(pl.program_id(1) == 0)
    def _():
        zk = jnp.zeros((WINDOW, HEAD_DIM), _bf16)
        for h in range(N_KV_HEADS):
            kpad[h, :WINDOW, :] = zk
            kpad[h, WINDOW + seq:, :] = zk
            kpad[h, WINDOW:WINDOW + seq, :] = k_ref[:, h * HEAD_DIM:(h + 1) * HEAD_DIM]
        zv = jnp.zeros((KV_WIDTH, WINDOW), _bf16)
        vtb[0] = zv
        vtb[nkb + 1] = zv
        for j in range(nkb):
            vtb[j + 1] = v_ref[j * WINDOW:(j + 1) * WINDOW, :].astype(_f32).T.astype(_bf16)

    ka = lax.broadcasted_iota(jnp.int32, (WINDOW, WINDOW), 0)
    qc = lax.broadcasted_iota(jnp.int32, (WINDOW, WINDOW), 1)
    tri_ge = jnp.where(ka >= qc, 0.0, NEG_BIG).astype(_f32)
    tri_le = jnp.where(ka <= qc, 0.0, NEG_BIG).astype(_f32)
    zero_blk = jnp.zeros((WINDOW, WINDOW), _bf16)
    n_tiles = q_ref.shape[0] // Q_TILE
    qbs = [pl.program_id(1) * n_tiles + t for t in range(n_tiles)]
    bias_first = [jnp.where(qb == 0, NEG_BIG, tri_ge) for qb in qbs]
    bias_last = [jnp.where(qb == nqb - 1, NEG_BIG, tri_le) for qb in qbs]

    def softmax_col(blocks, sink):
        m = blocks[0]
        for b in blocks[1:]:
            m = jnp.maximum(m, b)
        m = jnp.maximum(jnp.max(m, axis=0, keepdims=True), sink)
        ps = [jnp.exp2(b - m) for b in blocks]
        tot = ps[0]
        for p in ps[1:]:
            tot = tot + p
        return ps, jnp.sum(tot, axis=0, keepdims=True) + jnp.exp2(sink - m)

    w = WINDOW

    def scores(t, h):
        r0 = pl.multiple_of(qbs[t] * Q_TILE, Q_TILE)
        kh = kpad[h, pl.ds(r0, KEY_SPAN), :]
        q3 = jnp.concatenate([q_ref[t * Q_TILE:(t + 1) * Q_TILE, g * HEAD_DIM:(g + 1) * HEAD_DIM]
                              for g in range(Q_PER_KV * h, Q_PER_KV * (h + 1))], axis=0)
        return lax.dot_general(kh, q3, (((1,), (1,)), ((), ())), preferred_element_type=_f32)

    work = [(t, h) for t in range(n_tiles) for h in range(N_KV_HEADS)]
    s = scores(*work[0])
    for n, (t, h) in enumerate(work):
        s_next = scores(*work[n + 1]) if n + 1 < len(work) else None
        cols, dens = [], []
        for i in range(Q_PER_KV):
            sink = sink_ref[Q_PER_KV * h + i] * LOG2E
            c0 = 2 * i * w
            c1 = c0 + w
            p0, l0 = softmax_col([s[0:w, c0:c1] + bias_first[t], s[w:2 * w, c0:c1], s[2 * w:3 * w, c0:c1] + tri_le],
                                 sink)
            p1, l1 = softmax_col([s[w:2 * w, c1:c1 + w] + tri_ge, s[2 * w:3 * w, c1:c1 + w],
                                  s[3 * w:, c1:c1 + w] + bias_last[t]], sink)
            cols.append(jnp.concatenate([p.astype(_bf16) for p in p0] + [zero_blk], axis=0))
            cols.append(jnp.concatenate([zero_blk] + [p.astype(_bf16) for p in p1], axis=0))
            dens += [l0, l1]
        pt = jnp.concatenate(cols, axis=1)
        kb0 = qbs[t] * (Q_TILE // WINDOW)
        vth = jnp.concatenate([vtb[kb0 + j, h * HEAD_DIM:(h + 1) * HEAD_DIM, :]
                               for j in range(KEY_SPAN // WINDOW)], axis=1)
        o_t = jnp.dot(vth, pt, preferred_element_type=_f32) / jnp.concatenate(dens, axis=1)
        for i in range(Q_PER_KV):
            g = Q_PER_KV * h + i
            ot[t, g * HEAD_DIM:(g + 1) * HEAD_DIM, :] = o_t[:, i * Q_TILE:(i + 1) * Q_TILE]
        if h == N_KV_HEADS - 1:
            for c in range(ATTN_WIDTH // LANES):
                o_ref[t * Q_TILE:(t + 1) * Q_TILE, c * LANES:(c + 1) * LANES] = (
                    ot[t, c * LANES:(c + 1) * LANES, :].T.astype(_bf16))
        s = s_next


def _attention(seq, q, k, v, sinks):
    t_tokens = q.shape[0]
    rows = ATTN_TILES * Q_TILE
    steps = seq // rows
    return pl.pallas_call(
        _attn_kernel,
        out_shape=jax.ShapeDtypeStruct((t_tokens, ATTN_WIDTH), _bf16),
        grid=(t_tokens // seq, steps),
        in_specs=[pl.BlockSpec(memory_space=pltpu.SMEM),
                  pl.BlockSpec((rows, ATTN_WIDTH), lambda b, j: (b * steps + j, 0)),
                  pl.BlockSpec((seq, KV_WIDTH), lambda b, j: (b, 0)),
                  pl.BlockSpec((seq, KV_WIDTH), lambda b, j: (b, 0))],
        out_specs=pl.BlockSpec((rows, ATTN_WIDTH), lambda b, j: (b * steps + j, 0)),
        scratch_shapes=[pltpu.VMEM((N_KV_HEADS, seq + 2 * WINDOW, HEAD_DIM), _bf16),
                        pltpu.VMEM((seq // WINDOW + 2, KV_WIDTH, WINDOW), _bf16),
                        pltpu.VMEM((ATTN_TILES, ATTN_WIDTH, Q_TILE), _f32)],
        compiler_params=pltpu.CompilerParams(dimension_semantics=("parallel", "arbitrary"),
                                             vmem_limit_bytes=VMEM_LIMIT),
        name="attention",
    )(sinks, q, k, v)


def _outproj_kernel(x_ref, mf_ref, at_ref, g1_ref, sh_ref, sc_ref, ng_ref, wof_ref, woa_ref, bo_ref,
                    wr_ref, br_ref, tri_ref, x1_ref, hp_ref, rt_ref, cnt_ref, carry_ref):
    i = pl.program_id(0)

    @pl.when(i == 0)
    def _():
        carry_ref[...] = jnp.zeros_like(carry_ref)

    def project(rows):
        mf = jnp.concatenate([mf_ref[c, rows, :] for c in range(FOURIER_WIDTH // LANES)], axis=1)
        return (jnp.dot(mf.astype(_bf16), wof_ref[...], preferred_element_type=_f32)
                + jnp.dot(at_ref[rows, :], woa_ref[...], preferred_element_type=_f32) + bo_ref[...])

    tiles = _sub_tiles(x_ref.shape[0])
    mix = project(tiles[0])
    logit_cols = []
    for n, rows in enumerate(tiles):
        mix_next = project(tiles[n + 1]) if n + 1 < len(tiles) else None
        x1 = x_ref[rows, :] + g1_ref[0] * mix
        x1_ref[rows, :] = x1
        h = _rms(x1, ng_ref[...]) * (1.0 + sc_ref[0]) + sh_ref[0]
        hp_ref[rows, :] = _pack_halves(h[:, :HALF], h[:, HALF:])
        logit_cols.append(lax.dot_general(wr_ref[...], h.astype(_bf16), (((1,), (1,)), ((), ())),
                                          preferred_element_type=_f32))
        mix = mix_next
    logits = jnp.concatenate(logit_cols, axis=1) + br_ref[...]
    tm = logits.shape[1]
    erow = lax.broadcasted_iota(jnp.int32, logits.shape, 0)
    work = logits
    hots, vals, idxs = [], [], []
    for _k in range(TOP_K):
        mx = jnp.max(work, axis=0, keepdims=True)
        ix = jnp.min(jnp.where(work == mx, erow, N_EXPERTS), axis=0, keepdims=True)
        hot = erow == ix
        work = jnp.where(hot, -jnp.inf, work)
        hots.append(hot)
        vals.append(mx)
        idxs.append(ix)
    es = [jnp.exp(v - vals[0]) for v in vals]
    den = es[0] + es[1] + es[2] + es[3]
    member = jnp.zeros(logits.shape, _f32)
    for hot in hots:
        member = member + hot.astype(_f32)
    carry = carry_ref[...]
    parts = []
    for c0 in range(0, tm, RANK_TILE):
        m = member[:, c0:c0 + RANK_TILE]
        parts.append(jnp.dot(m.astype(_bf16), tri_ref[...], preferred_element_type=_f32)
                     + jnp.concatenate([carry] * (RANK_TILE // LANES), axis=1))
        carry = carry + jnp.broadcast_to(jnp.sum(m, axis=1, keepdims=True), carry.shape)
    before = jnp.concatenate(parts, axis=1)
    r16 = lax.broadcasted_iota(jnp.int32, (ROUTE_ROWS, tm), 0)
    slab = jnp.zeros((ROUTE_ROWS, tm), jnp.int32)
    for k in range(TOP_K):
        rank = jnp.sum(jnp.where(hots[k], before, 0.0), axis=0, keepdims=True).astype(jnp.int32)
        slab = jnp.where(r16 == k, idxs[k], slab)
        slab = jnp.where(r16 == TOP_K + k, rank, slab)
        slab = jnp.where(r16 == 2 * TOP_K + k, lax.bitcast_convert_type(es[k] / den, jnp.int32), slab)
    rt_ref[...] = slab
    carry_ref[...] = carry
    cnt_ref[...] = carry.astype(jnp.int32)


def _outproj(seq, x, mf, attn, g1, sh2, sc2, norm_g, wo_f, wo_a, b_o, w_r, b_r, tri):
    t_tokens = x.shape[0]
    tm = TOKEN_TILE
    tiles_per_seq = seq // tm
    row = lambda i: (i, 0)
    per_seq = lambda i: (i // tiles_per_seq, 0, 0)
    const = lambda i: (0, 0)
    return pl.pallas_call(
        _outproj_kernel,
        out_shape=[jax.ShapeDtypeStruct((t_tokens, D_MODEL), _f32),
                   jax.ShapeDtypeStruct((t_tokens, HALF), jnp.int32),
                   jax.ShapeDtypeStruct((ROUTE_ROWS, t_tokens), jnp.int32),
                   jax.ShapeDtypeStruct((N_EXPERTS, LANES), jnp.int32)],
        grid=(t_tokens // tm,),
        in_specs=[pl.BlockSpec((tm, D_MODEL), row),
                  pl.BlockSpec((FOURIER_WIDTH // LANES, tm, LANES), lambda i: (0, i, 0)),
                  pl.BlockSpec((tm, ATTN_WIDTH), row),
                  pl.BlockSpec((1, 1, D_MODEL), per_seq), pl.BlockSpec((1, 1, D_MODEL), per_seq),
                  pl.BlockSpec((1, 1, D_MODEL), per_seq),
                  pl.BlockSpec((1, D_MODEL), const),
                  pl.BlockSpec((FOURIER_WIDTH, D_MODEL), const), pl.BlockSpec((ATTN_WIDTH, D_MODEL), const),
                  pl.BlockSpec((1, D_MODEL), const),
                  pl.BlockSpec((N_EXPERTS, D_MODEL), const), pl.BlockSpec((N_EXPERTS, tm), const),
                  pl.BlockSpec((RANK_TILE, RANK_TILE), const)],
        out_specs=[pl.BlockSpec((tm, D_MODEL), row), pl.BlockSpec((tm, HALF), row),
                   pl.BlockSpec((ROUTE_ROWS, tm), lambda i: (0, i)),
                   pl.BlockSpec((N_EXPERTS, LANES), const)],
        scratch_shapes=[pltpu.VMEM((N_EXPERTS, LANES), _f32)],
        compiler_params=pltpu.CompilerParams(dimension_semantics=("arbitrary",), vmem_limit_bytes=VMEM_LIMIT),
        name="outproj",
    )(x, mf, attn, g1, sh2, sc2, norm_g, wo_f, wo_a, b_o, w_r, b_r, tri)


def _expert_kernel(blk_ref, bexp_ref, active_ref, x_ref, wgu_ref, bgu_ref, wd_ref, bd_ref, y_ref, wgu_bf, wd_bf):
    i = pl.program_id(0)
    prev = bexp_ref[jnp.maximum(i - 1, 0)]

    @pl.when((i == 0) | (bexp_ref[i] != prev))
    def _():
        wgu_bf[...] = wgu_ref[...].astype(_bf16)
        wd_bf[...] = wd_ref[...].astype(_bf16)

    def gate_up(rows):
        xa, xb = _unpack_halves(x_ref[rows, :])
        return (jnp.dot(xa.astype(_bf16), wgu_bf[:HALF, :], preferred_element_type=_f32)
                + jnp.dot(xb.astype(_bf16), wgu_bf[HALF:, :], preferred_element_type=_f32) + bgu_ref[...])

    def activate(gu):
        g = jnp.minimum(gu[:, :D_FF], SWIGLU_LIMIT)
        u = jnp.clip(gu[:, D_FF:], -SWIGLU_LIMIT, SWIGLU_LIMIT)
        return ((u + 1.0) * (g * (1.0 / (1.0 + jnp.exp(-SWIGLU_ALPHA * g))))).astype(_bf16)

    def down(rows, act):
        y = jnp.dot(act, wd_bf[...], preferred_element_type=_f32) + bd_ref[...]
        y_ref[rows, :] = _pack_halves(y[:, :HALF], y[:, HALF:])

    @pl.when(active_ref[i] > 0)
    def _():
        tiles = [pl.ds(r, EXPERT_SUB) for r in range(0, EXPERT_BLOCK, EXPERT_SUB)]
        gu = gate_up(tiles[0])
        for n, rows in enumerate(tiles):
            gu_next = gate_up(tiles[n + 1]) if n + 1 < len(tiles) else None
            down(rows, activate(gu))
            gu = gu_next


def _experts(layer, xs, blk, bexp, active, w_gu, b_gu, w_down, b_down):
    p_rows = xs.shape[0]
    nblk = p_rows // EXPERT_BLOCK
    grid_spec = pltpu.PrefetchScalarGridSpec(
        num_scalar_prefetch=3,
        grid=(nblk,),
        in_specs=[pl.BlockSpec((EXPERT_BLOCK, HALF), lambda i, blk, be, ac: (blk[i], 0)),
                  pl.BlockSpec((None, None, D_MODEL, 2 * D_FF), lambda i, blk, be, ac: (layer, be[i], 0, 0)),
                  pl.BlockSpec((None, 1, 2 * D_FF), lambda i, blk, be, ac: (be[i], 0, 0)),
                  pl.BlockSpec((None, None, D_FF, D_MODEL), lambda i, blk, be, ac: (layer, be[i], 0, 0)),
                  pl.BlockSpec((None, 1, D_MODEL), lambda i, blk, be, ac: (be[i], 0, 0))],
        out_specs=pl.BlockSpec((EXPERT_BLOCK, HALF), lambda i, blk, be, ac: (blk[i], 0)),
        scratch_shapes=[pltpu.VMEM((D_MODEL, 2 * D_FF), _bf16), pltpu.VMEM((D_FF, D_MODEL), _bf16)])
    return pl.pallas_call(
        _expert_kernel,
        out_shape=jax.ShapeDtypeStruct((p_rows, HALF), jnp.int32),
        grid_spec=grid_spec,
        compiler_params=pltpu.CompilerParams(dimension_semantics=("arbitrary",), vmem_limit_bytes=VMEM_LIMIT),
        name="experts",
    )(blk, bexp, active, xs, w_gu, b_gu.reshape(N_EXPERTS, 1, 2 * D_FF), w_down,
      b_down.reshape(N_EXPERTS, 1, D_MODEL))


def _sc_mesh():
    return plsc.VectorSubcoreMesh(core_axis_name="c", subcore_axis_name="s")


def _sc_worker():
    return lax.axis_index("s") * SC_CORES + lax.axis_index("c")


def _dispatch(hp, dest, p_rows):
    t_tokens = hp.shape[0]
    per_worker = t_tokens // (SC_CHUNK * SC_WORKERS)

    @functools.partial(
        pl.kernel, mesh=_sc_mesh(),
        out_type=jax.ShapeDtypeStruct((p_rows, HALF), jnp.int32),
        scratch_types=[pltpu.VMEM((TOP_K, SC_CHUNK), jnp.int32),
                       pltpu.VMEM((SC_CHUNK, HALF), jnp.int32),
                       pltpu.SemaphoreType.DMA],
        name="dispatch")
    def run(hp_hbm, dest_hbm, xs_hbm, idx_v, rows_v, sem):
        base = _sc_worker() * per_worker

        @pl.loop(0, per_worker)
        def _(j):
            chunk = base + j
            for k in range(TOP_K):
                pltpu.sync_copy(dest_hbm.at[pl.ds(k * t_tokens + chunk * SC_CHUNK, SC_CHUNK)], idx_v.at[k])
            pltpu.sync_copy(hp_hbm.at[pl.ds(chunk * SC_CHUNK, SC_CHUNK)], rows_v)
            copies = [pltpu.async_copy(rows_v, xs_hbm.at[idx_v.at[k]], sem) for k in range(TOP_K)]
            for cp in copies:
                cp.wait()

    return run(hp, dest)


def _combine_rows(y, dest, w, x1, g2, seq):
    t_tokens = x1.shape[0]
    cc = COMBINE_CHUNK
    n_chunks = t_tokens // cc
    per_worker = n_chunks // SC_WORKERS
    own = per_worker * cc
    assert per_worker % 2 == 0 and seq % own == 0
    n_vec = HALF // SC_LANES
    row_buf = pltpu.VMEM((TOP_K, cc, HALF), jnp.int32)
    x_buf = pltpu.VMEM((cc, D_MODEL), _f32)

    @functools.partial(
        pl.kernel, mesh=_sc_mesh(),
        out_type=jax.ShapeDtypeStruct((t_tokens, D_MODEL), _f32),
        scratch_types=[pltpu.VMEM((TOP_K * own,), jnp.int32),
                       pltpu.VMEM((TOP_K * own,), _f32),
                       row_buf, row_buf, x_buf, x_buf,
                       pltpu.VMEM((D_MODEL,), _f32),
                       pltpu.SemaphoreType.DMA, pltpu.SemaphoreType.DMA,
                       pltpu.SemaphoreType.DMA, pltpu.SemaphoreType.DMA],
        compiler_params=pltpu.CompilerParams(needs_layout_passes=False),
        name="combine_rows")
    def run(y_hbm, dest_hbm, w_hbm, x1_hbm, g2_hbm, x2_hbm,
            idx_v, w_v, rows0, rows1, xb0, xb1, g_v, in0, in1, out0, out1):
        base = _sc_worker() * per_worker
        bufs = ((rows0, xb0, in0, out0), (rows1, xb1, in1, out1))
        for k in range(TOP_K):
            pltpu.sync_copy(dest_hbm.at[pl.ds(k * t_tokens + base * cc, own)], idx_v.at[pl.ds(k * own, own)])
            pltpu.sync_copy(w_hbm.at[pl.ds(k * t_tokens + base * cc, own)], w_v.at[pl.ds(k * own, own)])
        pltpu.sync_copy(g2_hbm.at[(base * cc) // seq], g_v)

        def loads(c, b):
            rows, xb, isem, _ = bufs[b]
            cps = [pltpu.make_async_copy(y_hbm.at[idx_v.at[pl.ds(k * own + c * cc, cc)]], rows.at[k], isem)
                   for k in range(TOP_K)]
            cps.append(pltpu.make_async_copy(x1_hbm.at[pl.ds((base + c) * cc, cc)], xb, isem))
            return cps

        def store(c, b):
            _, xb, _, osem = bufs[b]
            return pltpu.make_async_copy(xb, x2_hbm.at[pl.ds((base + c) * cc, cc)], osem)

        def combine(c, b):
            rows, xb, _, _ = bufs[b]

            @pl.loop(0, cc)
            def _(t):
                ws = [plsc.load_gather(w_v, [jnp.full((SC_LANES,), k * own + c * cc + t, jnp.int32)])
                      for k in range(TOP_K)]

                @plsc.parallel_loop(0, n_vec)
                def _(j):
                    lo = pl.ds(j * SC_LANES, SC_LANES)
                    hi = pl.ds(HALF + j * SC_LANES, SC_LANES)
                    acc_a = None
                    acc_b = None
                    for k in range(TOP_K):
                        p = rows[k, t, lo]
                        a = lax.bitcast_convert_type(p & HI_MASK, _f32) * ws[k]
                        b_ = lax.bitcast_convert_type(p << HALF_WORD, _f32) * ws[k]
                        acc_a = a if acc_a is None else acc_a + a
                        acc_b = b_ if acc_b is None else acc_b + b_
                    xb[t, lo] = xb[t, lo] + g_v[lo] * acc_a
                    xb[t, hi] = xb[t, hi] + g_v[hi] * acc_b

        for cp in loads(0, 0):
            cp.start()

        @pl.loop(0, per_worker // 2)
        def _(c2):
            for b in range(2):
                c = 2 * c2 + b

                @pl.when(c >= 1)
                def _():
                    store(c - 1, 1 - b).wait()

                @pl.when(c + 1 < per_worker)
                def _():
                    for cp in loads(c + 1, 1 - b):
                        cp.start()

                for cp in loads(c, b):
                    cp.wait()
                combine(c, b)
                store(c, b).start()

        store(per_worker - 1, 1).wait()

    return run(y, dest, w, x1, g2)


def _final_kernel(x_ref, ng_ref, o_ref):
    o_ref[...] = _rms(x_ref[...], ng_ref[...])


def _final(x, final_g):
    t_tokens = x.shape[0]
    tm = TOKEN_TILE
    row = lambda i: (i, 0)
    return pl.pallas_call(
        _final_kernel,
        out_shape=jax.ShapeDtypeStruct((t_tokens, D_MODEL), _f32),
        grid=(t_tokens // tm,),
        in_specs=[pl.BlockSpec((tm, D_MODEL), row), pl.BlockSpec((1, D_MODEL), lambda i: (0, 0))],
        out_specs=pl.BlockSpec((tm, D_MODEL), row),
        compiler_params=pltpu.CompilerParams(dimension_semantics=("parallel",), vmem_limit_bytes=VMEM_LIMIT),
        name="final",
    )(x, final_g)


def _plan(route, counts, nblk):
    cnt = counts[:, 0]
    padded = ((cnt + EXPERT_BLOCK - 1) // EXPERT_BLOCK) * EXPERT_BLOCK
    pad_end = jnp.cumsum(padded)
    pad_start = pad_end - padded
    expert = jnp.arange(N_EXPERTS, dtype=jnp.int32)[:, None, None]
    start_of = jnp.sum(jnp.where(route[None, :TOP_K] == expert, pad_start[:, None, None], 0), axis=0)
    dest = start_of + route[TOP_K:2 * TOP_K]
    nact = jnp.maximum(pad_end[-1] // EXPERT_BLOCK, 1)
    step = jnp.arange(nblk, dtype=jnp.int32)
    blk = jnp.minimum(step, nact - 1)
    bexp = jnp.minimum(jnp.sum(pad_end[None, :] <= (blk * EXPERT_BLOCK)[:, None], axis=1), N_EXPERTS - 1)
    active = step < pad_end[-1] // EXPERT_BLOCK
    w = lax.bitcast_convert_type(route[2 * TOP_K:3 * TOP_K], _f32)
    return (dest.reshape(-1), w.reshape(-1), blk.astype(jnp.int32), bexp.astype(jnp.int32),
            active.astype(jnp.int32))


def _trunk(x_all, c_all, norm1_g, ada_w, ada_b, w_in, b_in, w_fmix, sinks, w_o, b_o,
           norm2_g, w_router, b_router, w_gu, b_gu, w_down, b_down, final_g):
    nb, seq, _ = x_all.shape
    t_tokens = nb * seq
    assert seq % TOKEN_TILE == 0 and seq % (ATTN_TILES * Q_TILE) == 0 and seq >= KEY_SPAN
    assert t_tokens % (SC_CHUNK * SC_WORKERS) == 0 and seq % COMBINE_CHUNK == 0
    nblk = (t_tokens * TOP_K) // EXPERT_BLOCK + N_EXPERTS
    p_rows = nblk * EXPERT_BLOCK

    mod = _modulation(c_all, ada_w, ada_b)
    mod = mod.reshape(DEPTH, nb, 6, 1, D_MODEL)
    rope_tabs = _rope_tables(seq)
    dft = _dft_tables(seq)
    tri = jnp.triu(jnp.ones((RANK_TILE, RANK_TILE), _f32), 1).astype(_bf16)

    x = x_all.reshape(t_tokens, D_MODEL)
    for l in range(DEPTH):
        sh1, sc1, g1, sh2, sc2, g2 = (mod[l, :, j] for j in range(6))
        ab = _channel_dft_fold(w_fmix[l], seq)
        y, q, k, v = _inproj(seq, x, sh1, sc1, norm1_g[l].reshape(1, D_MODEL), w_in[l].astype(_bf16),
                             b_in[l].reshape(1, IN_WIDTH), ab, rope_tabs)
        mf = _fourier(seq, y, dft)
        attn = _attention(seq, q, k, v, sinks[l])
        w_o_bf = w_o[l].astype(_bf16)
        w_r = w_router[l].T.astype(_bf16)
        b_r = jnp.broadcast_to(b_router[l][:, None], (N_EXPERTS, TOKEN_TILE))
        x1, hp, route, counts = _outproj(seq, x, mf, attn, g1, sh2, sc2, norm2_g[l].reshape(1, D_MODEL),
                                         w_o_bf[:FOURIER_WIDTH], w_o_bf[FOURIER_WIDTH:],
                                         b_o[l].reshape(1, D_MODEL), w_r, b_r, tri)
        dest, w, blk, bexp, active = _plan(route, counts, nblk)
        xs = _dispatch(hp, dest, p_rows)
        ys = _experts(l, xs, blk, bexp, active, w_gu, b_gu[l], w_down, b_down[l])
        x = _combine_rows(ys, dest, w, x1, g2.reshape(nb, D_MODEL), seq)
    out = _final(x, final_g.reshape(1, D_MODEL))
    return out.reshape(nb, seq, D_MODEL)


def kernel(x_prompt, x_sample, c_prompt, c_sample, norm1_g, ada_w, ada_b, w_in, b_in, w_fmix, sinks, w_o, b_o,
           norm2_g, w_router, b_router, w_gu, b_gu, w_down, b_down, final_g):
    ws = (norm1_g, ada_w, ada_b, w_in, b_in, w_fmix, sinks, w_o, b_o,
          norm2_g, w_router, b_router, w_gu, b_gu, w_down, b_down, final_g)
    return _trunk(x_prompt, c_prompt, *ws), _trunk(x_sample, c_sample, *ws)
```

```python
import functools

import jax
import jax.numpy as jnp
from jax import lax
from jax.experimental import pallas as pl
from jax.experimental.pallas import tpu as pltpu
from jax.experimental.pallas import tpu_sc as plsc

D_MODEL = 1024
DEPTH = 4
FOURIER_WIDTH = 256
N_FGROUPS = 4
FGROUP_DIM = 64
HEAD_DIM = 64
N_Q_HEADS = 12
N_KV_HEADS = 4
Q_PER_KV = 3
ATTN_WIDTH = N_Q_HEADS * HEAD_DIM
KV_WIDTH = N_KV_HEADS * HEAD_DIM
IN_WIDTH = FOURIER_WIDTH + ATTN_WIDTH + 2 * KV_WIDTH
WINDOW = 128
ROPE_THETA = 500000.0
ROT_DIM = 16
N_EXPERTS = 32
TOP_K = 4
D_FF = 512
EXPERT_BLOCK = 1024
EXPERT_SUB = 512
SWIGLU_LIMIT = 7.0
SWIGLU_ALPHA = 1.702
NORM_EPS = 1e-5

LANES = 128
HALF = D_MODEL // 2
TOKEN_TILE = 1024
INPROJ_TILE = 2048
RANK_TILE = 512
SUB_TILE = 256
Q_TILE = 256
ATTN_TILES = 8
KEY_SPAN = Q_TILE + 2 * WINDOW
SC_CHUNK = 128
SC_CORES = 2
SC_WORKERS = SC_CORES * 16
SC_LANES = 16
COMBINE_CHUNK = 16
VMEM_LIMIT = 56 * 1024 * 1024
NEG_BIG = -1e30
LOG2E = 1.4426950408889634
Q_SCALE = HEAD_DIM ** -0.5 * LOG2E
HALF_WORD = 16
HI_MASK = -(1 << HALF_WORD)
ROUTE_ROWS = 16

_f32 = jnp.float32
_bf16 = jnp.bfloat16


def _pack_halves(a, b):
    ua = lax.bitcast_convert_type(a.astype(_bf16).astype(_f32), jnp.int32)
    ub = lax.bitcast_convert_type(b.astype(_bf16).astype(_f32), jnp.int32)
    return ua | lax.shift_right_logical(ub, HALF_WORD)


def _unpack_halves(p):
    a = lax.bitcast_convert_type(p & HI_MASK, _f32)
    b = lax.bitcast_convert_type(lax.shift_left(p, HALF_WORD), _f32)
    return a, b


def _rms(x, g):
    ms = jnp.mean(x * x, axis=-1, keepdims=True)
    return x * lax.rsqrt(ms + NORM_EPS) * g


def _mod_kernel(c_ref, w_ref, b_ref, o_ref):
    c = c_ref[...]
    s = c * (1.0 / (1.0 + jnp.exp(-c)))
    o_ref[0] = jnp.dot(s.astype(_bf16), w_ref[0].astype(_bf16), preferred_element_type=_f32) + b_ref[0]


def _modulation(c_all, ada_w, ada_b):
    nb = c_all.shape[0]
    ncol = ada_w.shape[2] // D_MODEL
    return pl.pallas_call(
        _mod_kernel,
        out_shape=jax.ShapeDtypeStruct((DEPTH, nb, 6 * D_MODEL), _f32),
        grid=(DEPTH, ncol),
        in_specs=[
            pl.BlockSpec((nb, D_MODEL), lambda l, j: (0, 0)),
            pl.BlockSpec((1, D_MODEL, D_MODEL), lambda l, j: (l, 0, j)),
            pl.BlockSpec((1, 1, D_MODEL), lambda l, j: (l, 0, j)),
        ],
        out_specs=pl.BlockSpec((1, nb, D_MODEL), lambda l, j: (l, 0, j)),
        name="modulation",
    )(c_all, ada_w, ada_b.reshape(DEPTH, 1, 6 * D_MODEL))


def _inproj_matmul(x, sh_ref, sc_ref, ng_ref, w_ref, b_ref, *_):
    h = _rms(x, ng_ref[...]) * (1.0 + sc_ref[0]) + sh_ref[0]
    return jnp.dot(h.astype(_bf16), w_ref[...], preferred_element_type=_f32) + b_ref[...]


def _inproj_finish(rows, z, sh_ref, sc_ref, ng_ref, w_ref, b_ref, ab_ref, rc_ref, rs1_ref, rs2_ref,
                   y_ref, q_ref, k_ref, v_ref):
    f = z[:, :FOURIER_WIDTH]
    y_ref[rows, :] = jnp.dot(f.astype(_bf16), ab_ref[...], preferred_element_type=_f32).astype(_bf16)
    rc = rc_ref[rows, :]
    rs1 = rs1_ref[rows, :]
    rs2 = rs2_ref[rows, :]

    def rope(t):
        return t * rc + pltpu.roll(t, LANES - ROT_DIM // 2, 1) * rs1 + pltpu.roll(t, ROT_DIM // 2, 1) * rs2

    q0 = FOURIER_WIDTH
    for c in range(ATTN_WIDTH // LANES):
        t = z[:, q0 + c * LANES:q0 + (c + 1) * LANES]
        q_ref[rows, c * LANES:(c + 1) * LANES] = (rope(t) * Q_SCALE).astype(_bf16)
    k0 = q0 + ATTN_WIDTH
    for c in range(KV_WIDTH // LANES):
        t = z[:, k0 + c * LANES:k0 + (c + 1) * LANES]
        k_ref[rows, c * LANES:(c + 1) * LANES] = rope(t).astype(_bf16)
    v_ref[rows, :] = z[:, k0 + KV_WIDTH:].astype(_bf16)


def _sub_tiles(n_rows):
    return [pl.ds(r, SUB_TILE) for r in range(0, n_rows, SUB_TILE)]


def _inproj_kernel(x_ref, *refs):
    tiles = _sub_tiles(x_ref.shape[0])
    z = _inproj_matmul(x_ref[tiles[0], :], *refs)
    for n, rows in enumerate(tiles):
        z_next = _inproj_matmul(x_ref[tiles[n + 1], :], *refs) if n + 1 < len(tiles) else None
        _inproj_finish(rows, z, *refs)
        z = z_next


def _inproj(seq, x, sh1, sc1, norm_g, w_in, b_in, ab, rope_tabs):
    t_tokens = x.shape[0]
    tm = INPROJ_TILE
    tiles_per_seq = seq // tm
    row = lambda i: (i, 0)
    per_seq = lambda i: (i // tiles_per_seq, 0, 0)
    const = lambda i: (0, 0)
    pos = lambda i: (i % tiles_per_seq, 0)
    return pl.pallas_call(
        _inproj_kernel,
        out_shape=[jax.ShapeDtypeStruct((t_tokens, 2 * FOURIER_WIDTH), _bf16),
                   jax.ShapeDtypeStruct((t_tokens, ATTN_WIDTH), _bf16),
                   jax.ShapeDtypeStruct((t_tokens, KV_WIDTH), _bf16),
                   jax.ShapeDtypeStruct((t_tokens, KV_WIDTH), _bf16)],
        grid=(t_tokens // tm,),
        in_specs=[pl.BlockSpec((tm, D_MODEL), row),
                  pl.BlockSpec((1, 1, D_MODEL), per_seq), pl.BlockSpec((1, 1, D_MODEL), per_seq),
                  pl.BlockSpec((1, D_MODEL), const),
                  pl.BlockSpec((D_MODEL, IN_WIDTH), const), pl.BlockSpec((1, IN_WIDTH), const),
                  pl.BlockSpec((FOURIER_WIDTH, 2 * FOURIER_WIDTH), const),
                  pl.BlockSpec((tm, LANES), pos), pl.BlockSpec((tm, LANES), pos), pl.BlockSpec((tm, LANES), pos)],
        out_specs=[pl.BlockSpec((tm, 2 * FOURIER_WIDTH), row), pl.BlockSpec((tm, ATTN_WIDTH), row),
                   pl.BlockSpec((tm, KV_WIDTH), row), pl.BlockSpec((tm, KV_WIDTH), row)],
        compiler_params=pltpu.CompilerParams(dimension_semantics=("parallel",), vmem_limit_bytes=VMEM_LIMIT),
        name="inproj",
    )(x, sh1, sc1, norm_g, w_in, b_in, ab, *rope_tabs)


def _fourier_kernel(y_ref, *refs):
    tabs, o_ref = refs[:-1], refs[-1]
    quarter = y_ref.shape[0] // 4
    w = FOURIER_WIDTH
    a = [y_ref[q * quarter:(q + 1) * quarter, :w].astype(_f32) for q in range(4)]
    b = [y_ref[q * quarter:(q + 1) * quarter, w:].astype(_f32) for q in range(4)]
    a_s, a_d, a_s13, a_d13 = a[0] + a[2], a[0] - a[2], a[1] + a[3], a[1] - a[3]
    b_s, b_d, b_s13, b_d13 = b[0] + b[2], b[0] - b[2], b[1] + b[3], b[1] - b[3]
    w_re = [a_s + a_s13, a_d + b_d13, a_s - a_s13, a_d - b_d13]
    w_im = [b_s + b_s13, b_d - a_d13, b_s - b_s13, b_d + a_d13]
    for r in range(4):
        out = (jnp.dot(tabs[2 * r][...], w_re[r].astype(_bf16), preferred_element_type=_f32)
               + jnp.dot(tabs[2 * r + 1][...], w_im[r].astype(_bf16), preferred_element_type=_f32))
        for c in range(FOURIER_WIDTH // LANES):
            o_ref[c, pl.ds(r, quarter, stride=4), :] = out[:, c * LANES:(c + 1) * LANES]


def _fourier(seq, y, dft):
    t_tokens = y.shape[0]
    quarter = seq // 4
    const = lambda b: (0, 0)
    return pl.pallas_call(
        _fourier_kernel,
        out_shape=jax.ShapeDtypeStruct((FOURIER_WIDTH // LANES, t_tokens, LANES), _f32),
        grid=(t_tokens // seq,),
        in_specs=[pl.BlockSpec((seq, 2 * FOURIER_WIDTH), lambda b: (b, 0))]
        + [pl.BlockSpec((quarter, quarter), const)] * 8,
        out_specs=pl.BlockSpec((FOURIER_WIDTH // LANES, seq, LANES), lambda b: (0, b, 0)),
        compiler_params=pltpu.CompilerParams(dimension_semantics=("parallel",), vmem_limit_bytes=VMEM_LIMIT),
        name="fourier",
    )(y, *dft)


def _dft_tables(seq):
    quarter = seq // 4
    j = jnp.arange(quarter, dtype=jnp.int32)[:, None]
    m = jnp.arange(quarter, dtype=jnp.int32)[None, :]
    tabs = []
    for r in range(4):
        ang = (((4 * j + r) * m) % seq).astype(_f32) * (2.0 * jnp.pi / seq)
        tabs += [jnp.cos(ang).astype(_bf16), jnp.sin(ang).astype(_bf16)]
    return tuple(tabs)


def _channel_dft_fold(w_fmix, seq):
    c = jnp.arange(FGROUP_DIM, dtype=jnp.int32)
    ang = ((c[:, None] * c[None, :]) % FGROUP_DIM).astype(_f32) * (2.0 * jnp.pi / FGROUP_DIM)
    scale = (seq * FGROUP_DIM) ** -0.5
    hp = lax.Precision.HIGHEST
    a = jnp.einsum('cm,gmd->gcd', jnp.cos(ang) * scale, w_fmix, precision=hp)
    b = jnp.einsum('cm,gmd->gcd', -jnp.sin(ang) * scale, w_fmix, precision=hp)
    eye = jnp.eye(N_FGROUPS, dtype=_f32)
    bd = lambda m: jnp.einsum('gcd,gh->gchd', m, eye).reshape(FOURIER_WIDTH, FOURIER_WIDTH)
    return jnp.concatenate([bd(a), bd(b)], axis=1).astype(_bf16)


def _rope_tables(seq):
    half = ROT_DIM // 2
    inv_freq = jnp.power(ROPE_THETA, -jnp.arange(0, ROT_DIM, 2, dtype=_f32) / ROT_DIM)
    ang = jnp.arange(seq, dtype=_f32)[:, None] * inv_freq[None, :]
    cos, sin = jnp.cos(ang), jnp.sin(ang)
    ones = jnp.ones((seq, HEAD_DIM - ROT_DIM), _f32)
    zeros = jnp.zeros((seq, HEAD_DIM - ROT_DIM), _f32)
    zh = jnp.zeros((seq, half), _f32)
    c1 = jnp.concatenate([cos, cos, ones], axis=1)
    s1 = jnp.concatenate([-sin, zh, zeros], axis=1)
    s2 = jnp.concatenate([zh, sin, zeros], axis=1)
    rep = LANES // HEAD_DIM
    return tuple(jnp.tile(t, (1, rep)) for t in (c1, s1, s2))


def _attn_kernel(sink_ref, q_ref, k_ref, v_ref, o_ref, kpad, vtb, ot):
    seq = k_ref.shape[0]
    nqb = seq // Q_TILE
    nkb = seq // WINDOW

    @pl.when(pl.program_id(1) == 0)
    def _():
        zk = jnp.zeros((WINDOW, HEAD_DIM), _bf16)
        for h in range(N_KV_HEADS):
            kpad[h, :WINDOW, :] = zk
            kpad[h, WINDOW + seq:, :] = zk
            kpad[h, WINDOW:WINDOW + seq, :] = k_ref[:, h * HEAD_DIM:(h + 1) * HEAD_DIM]
        zv = jnp.zeros((KV_WIDTH, WINDOW), _bf16)
        vtb[0] = zv
        vtb[nkb + 1] = zv
        for j in range(nkb):
            vtb[j + 1] = v_ref[j * WINDOW:(j + 1) * WINDOW, :].astype(_f32).T.astype(_bf16)

    ka = lax.broadcasted_iota(jnp.int32, (WINDOW, WINDOW), 0)
    qc = lax.broadcasted_iota(jnp.int32, (WINDOW, WINDOW), 1)
    tri_ge = jnp.where(ka >= qc, 0.0, NEG_BIG).astype(_f32)
    tri_le = jnp.where(ka <= qc, 0.0, NEG_BIG).astype(_f32)
    zero_blk = jnp.zeros((WINDOW, WINDOW), _bf16)
    n_tiles = q_ref.shape[0] // Q_TILE
    qbs = [pl.program_id(1) * n_tiles + t for t in range(n_tiles)]
    bias_first = [jnp.where(qb == 0, NEG_BIG, tri_ge) for qb in qbs]
    bias_last = [jnp.where(qb == nqb - 1, NEG_BIG, tri_le) for qb in qbs]

    def softmax_col(blocks, sink):
        m = blocks[0]
        for b in blocks[1:]:
            m = jnp.maximum(m, b)
        m = jnp.maximum(jnp.max(m, axis=0, keepdims=True), sink)
        ps = [jnp.exp2(b - m) for b in blocks]
        tot = ps[0]
        for p in ps[1:]:
            tot = tot + p
        return ps, jnp.sum(tot, axis=0, keepdims=True) + jnp.exp2(sink - m)

    w = WINDOW

    def scores(t, h):
        r0 = pl.multiple_of(qbs[t] * Q_TILE, Q_TILE)
        kh = kpad[h, pl.ds(r0, KEY_SPAN), :]
        q3 = jnp.concatenate([q_ref[t * Q_TILE:(t + 1) * Q_TILE, g * HEAD_DIM:(g + 1) * HEAD_DIM]
                              for g in range(Q_PER_KV * h, Q_PER_KV * (h + 1))], axis=0)
        return lax.dot_general(kh, q3, (((1,), (1,)), ((), ())), preferred_element_type=_f32)

    work = [(t, h) for t in range(n_tiles) for h in range(N_KV_HEADS)]
    s = scores(*work[0])
    for n, (t, h) in enumerate(work):
        s_next = scores(*work[n + 1]) if n + 1 < len(work) else None
        cols, dens = [], []
        for i in range(Q_PER_KV):
            sink = sink_ref[Q_PER_KV * h + i] * LOG2E
            c0 = 2 * i * w
            c1 = c0 + w
            p0, l0 = softmax_col([s[0:w, c0:c1] + bias_first[t], s[w:2 * w, c0:c1], s[2 * w:3 * w, c0:c1] + tri_le],
                                 sink)
            p1, l1 = softmax_col([s[w:2 * w, c1:c1 + w] + tri_ge, s[2 * w:3 * w, c1:c1 + w],
                                  s[3 * w:, c1:c1 + w] + bias_last[t]], sink)
            cols.append(jnp.concatenate([p.astype(_bf16) for p in p0] + [zero_blk], axis=0))
            cols.append(jnp.concatenate([zero_blk] + [p.astype(_bf16) for p in p1], axis=0))
            dens += [l0, l1]
        pt = jnp.concatenate(cols, axis=1)
        kb0 = qbs[t] * (Q_TILE // WINDOW)
        vth = jnp.concatenate([vtb[kb0 + j, h * HEAD_DIM:(h + 1) * HEAD_DIM, :]
                               for j in range(KEY_SPAN // WINDOW)], axis=1)
        o_t = jnp.dot(vth, pt, preferred_element_type=_f32) / jnp.concatenate(dens, axis=1)
        for i in range(Q_PER_KV):
            g = Q_PER_KV * h + i
            ot[t, g * HEAD_DIM:(g + 1) * HEAD_DIM, :] = o_t[:, i * Q_TILE:(i + 1) * Q_TILE]
        if h == N_KV_HEADS - 1:
            for c in range(ATTN_WIDTH // LANES):
                o_ref[t * Q_TILE:(t + 1) * Q_TILE, c * LANES:(c + 1) * LANES] = (
                    ot[t, c * LANES:(c + 1) * LANES, :].T.astype(_bf16))
        s = s_next


def _attention(seq, q, k, v, sinks):
    t_tokens = q.shape[0]
    rows = ATTN_TILES * Q_TILE
    steps = seq // rows
    return pl.pallas_call(
        _attn_kernel,
        out_shape=jax.ShapeDtypeStruct((t_tokens, ATTN_WIDTH), _bf16),
        grid=(t_tokens // seq, steps),
        in_specs=[pl.BlockSpec(memory_space=pltpu.SMEM),
                  pl.BlockSpec((rows, ATTN_WIDTH), lambda b, j: (b * steps + j, 0)),
                  pl.BlockSpec((seq, KV_WIDTH), lambda b, j: (b, 0)),
                  pl.BlockSpec((seq, KV_WIDTH), lambda b, j: (b, 0))],
        out_specs=pl.BlockSpec((rows, ATTN_WIDTH), lambda b, j: (b * steps + j, 0)),
        scratch_shapes=[pltpu.VMEM((N_KV_HEADS, seq + 2 * WINDOW, HEAD_DIM), _bf16),
                        pltpu.VMEM((seq // WINDOW + 2, KV_WIDTH, WINDOW), _bf16),
                        pltpu.VMEM((ATTN_TILES, ATTN_WIDTH, Q_TILE), _f32)],
        compiler_params=pltpu.CompilerParams(dimension_semantics=("parallel", "arbitrary"),
                                             vmem_limit_bytes=VMEM_LIMIT),
        name="attention",
    )(sinks, q, k, v)


def _outproj_kernel(x_ref, mf_ref, at_ref, g1_ref, sh_ref, sc_ref, ng_ref, wof_ref, woa_ref, bo_ref,
                    wr_ref, br_ref, tri_ref, x1_ref, hp_ref, rt_ref, cnt_ref, carry_ref):
    i = pl.program_id(0)

    @pl.when(i == 0)
    def _():
        carry_ref[...] = jnp.zeros_like(carry_ref)

    def project(rows):
        mf = jnp.concatenate([mf_ref[c, rows, :] for c in range(FOURIER_WIDTH // LANES)], axis=1)
        return (jnp.dot(mf.astype(_bf16), wof_ref[...], preferred_element_type=_f32)
                + jnp.dot(at_ref[rows, :], woa_ref[...], preferred_element_type=_f32) + bo_ref[...])

    tiles = _sub_tiles(x_ref.shape[0])
    mix = project(tiles[0])
    logit_cols = []
    for n, rows in enumerate(tiles):
        mix_next = project(tiles[n + 1]) if n + 1 < len(tiles) else None
        x1 = x_ref[rows, :] + g1_ref[0] * mix
        x1_ref[rows, :] = x1
        h = _rms(x1, ng_ref[...]) * (1.0 + sc_ref[0]) + sh_ref[0]
        hp_ref[rows, :] = _pack_halves(h[:, :HALF], h[:, HALF:])
        logit_cols.append(lax.dot_general(wr_ref[...], h.astype(_bf16), (((1,), (1,)), ((), ())),
                                          preferred_element_type=_f32))
        mix = mix_next
    logits = jnp.concatenate(logit_cols, axis=1) + br_ref[...]
    tm = logits.shape[1]
    erow = lax.broadcasted_iota(jnp.int32, logits.shape, 0)
    work = logits
    hots, vals, idxs = [], [], []
    for _k in range(TOP_K):
        mx = jnp.max(work, axis=0, keepdims=True)
        ix = jnp.min(jnp.where(work == mx, erow, N_EXPERTS), axis=0, keepdims=True)
        hot = erow == ix
        work = jnp.where(hot, -jnp.inf, work)
        hots.append(hot)
        vals.append(mx)
        idxs.append(ix)
    es = [jnp.exp(v - vals[0]) for v in vals]
    den = es[0] + es[1] + es[2] + es[3]
    member = jnp.zeros(logits.shape, _f32)
    for hot in hots:
        member = member + hot.astype(_f32)
    carry = carry_ref[...]
    parts = []
    for c0 in range(0, tm, RANK_TILE):
        m = member[:, c0:c0 + RANK_TILE]
        parts.append(jnp.dot(m.astype(_bf16), tri_ref[...], preferred_element_type=_f32)
                     + jnp.concatenate([carry] * (RANK_TILE // LANES), axis=1))
        carry = carry + jnp.broadcast_to(jnp.sum(m, axis=1, keepdims=True), carry.shape)
    before = jnp.concatenate(parts, axis=1)
    r16 = lax.broadcasted_iota(jnp.int32, (ROUTE_ROWS, tm), 0)
    slab = jnp.zeros((ROUTE_ROWS, tm), jnp.int32)
    for k in range(TOP_K):
        rank = jnp.sum(jnp.where(hots[k], before, 0.0), axis=0, keepdims=True).astype(jnp.int32)
        slab = jnp.where(r16 == k, idxs[k], slab)
        slab = jnp.where(r16 == TOP_K + k, rank, slab)
        slab = jnp.where(r16 == 2 * TOP_K + k, lax.bitcast_convert_type(es[k] / den, jnp.int32), slab)
    rt_ref[...] = slab
    carry_ref[...] = carry
    cnt_ref[...] = carry.astype(jnp.int32)


def _outproj(seq, x, mf, attn, g1, sh2, sc2, norm_g, wo_f, wo_a, b_o, w_r, b_r, tri):
    t_tokens = x.shape[0]
    tm = TOKEN_TILE
    tiles_per_seq = seq // tm
    row = lambda i: (i, 0)
    per_seq = lambda i: (i // tiles_per_seq, 0, 0)
    const = lambda i: (0, 0)
    return pl.pallas_call(
        _outproj_kernel,
        out_shape=[jax.ShapeDtypeStruct((t_tokens, D_MODEL), _f32),
                   jax.ShapeDtypeStruct((t_tokens, HALF), jnp.int32),
                   jax.ShapeDtypeStruct((ROUTE_ROWS, t_tokens), jnp.int32),
                   jax.ShapeDtypeStruct((N_EXPERTS, LANES), jnp.int32)],
        grid=(t_tokens // tm,),
        in_specs=[pl.BlockSpec((tm, D_MODEL), row),
                  pl.BlockSpec((FOURIER_WIDTH // LANES, tm, LANES), lambda i: (0, i, 0)),
                  pl.BlockSpec((tm, ATTN_WIDTH), row),
                  pl.BlockSpec((1, 1, D_MODEL), per_seq), pl.BlockSpec((1, 1, D_MODEL), per_seq),
                  pl.BlockSpec((1, 1, D_MODEL), per_seq),
                  pl.BlockSpec((1, D_MODEL), const),
                  pl.BlockSpec((FOURIER_WIDTH, D_MODEL), const), pl.BlockSpec((ATTN_WIDTH, D_MODEL), const),
                  pl.BlockSpec((1, D_MODEL), const),
                  pl.BlockSpec((N_EXPERTS, D_MODEL), const), pl.BlockSpec((N_EXPERTS, tm), const),
                  pl.BlockSpec((RANK_TILE, RANK_TILE), const)],
        out_specs=[pl.BlockSpec((tm, D_MODEL), row), pl.BlockSpec((tm, HALF), row),
                   pl.BlockSpec((ROUTE_ROWS, tm), lambda i: (0, i)),
                   pl.BlockSpec((N_EXPERTS, LANES), const)],
        scratch_shapes=[pltpu.VMEM((N_EXPERTS, LANES), _f32)],
        compiler_params=pltpu.CompilerParams(dimension_semantics=("arbitrary",), vmem_limit_bytes=VMEM_LIMIT),
        name="outproj",
    )(x, mf, attn, g1, sh2, sc2, norm_g, wo_f, wo_a, b_o, w_r, b_r, tri)


def _expert_kernel(blk_ref, bexp_ref, active_ref, x_ref, wgu_ref, bgu_ref, wd_ref, bd_ref, y_ref, wgu_bf, wd_bf):
    i = pl.program_id(0)
    prev = bexp_ref[jnp.maximum(i - 1, 0)]

    @pl.when((i == 0) | (bexp_ref[i] != prev))
    def _():
        wgu_bf[...] = wgu_ref[...].astype(_bf16)
        wd_bf[...] = wd_ref[...].astype(_bf16)

    def gate_up(rows):
        xa, xb = _unpack_halves(x_ref[rows, :])
        return (jnp.dot(xa.astype(_bf16), wgu_bf[:HALF, :], preferred_element_type=_f32)
                + jnp.dot(xb.astype(_bf16), wgu_bf[HALF:, :], preferred_element_type=_f32) + bgu_ref[...])

    def activate(gu):
        g = jnp.minimum(gu[:, :D_FF], SWIGLU_LIMIT)
        u = jnp.clip(gu[:, D_FF:], -SWIGLU_LIMIT, SWIGLU_LIMIT)
        return ((u + 1.0) * (g * (1.0 / (1.0 + jnp.exp(-SWIGLU_ALPHA * g))))).astype(_bf16)

    def down(rows, act):
        y = jnp.dot(act, wd_bf[...], preferred_element_type=_f32) + bd_ref[...]
        y_ref[rows, :] = _pack_halves(y[:, :HALF], y[:, HALF:])

    @pl.when(active_ref[i] > 0)
    def _():
        tiles = [pl.ds(r, EXPERT_SUB) for r in range(0, EXPERT_BLOCK, EXPERT_SUB)]
        gu = gate_up(tiles[0])
        for n, rows in enumerate(tiles):
            gu_next = gate_up(tiles[n + 1]) if n + 1 < len(tiles) else None
            down(rows, activate(gu))
            gu = gu_next


def _experts(layer, xs, blk, bexp, active, w_gu, b_gu, w_down, b_down):
    p_rows = xs.shape[0]
    nblk = p_rows // EXPERT_BLOCK
    grid_spec = pltpu.PrefetchScalarGridSpec(
        num_scalar_prefetch=3,
        grid=(nblk,),
        in_specs=[pl.BlockSpec((EXPERT_BLOCK, HALF), lambda i, blk, be, ac: (blk[i], 0)),
                  pl.BlockSpec((None, None, D_MODEL, 2 * D_FF), lambda i, blk, be, ac: (layer, be[i], 0, 0)),
                  pl.BlockSpec((None, 1, 2 * D_FF), lambda i, blk, be, ac: (be[i], 0, 0)),
                  pl.BlockSpec((None, None, D_FF, D_MODEL), lambda i, blk, be, ac: (layer, be[i], 0, 0)),
                  pl.BlockSpec((None, 1, D_MODEL), lambda i, blk, be, ac: (be[i], 0, 0))],
        out_specs=pl.BlockSpec((EXPERT_BLOCK, HALF), lambda i, blk, be, ac: (blk[i], 0)),
        scratch_shapes=[pltpu.VMEM((D_MODEL, 2 * D_FF), _bf16), pltpu.VMEM((D_FF, D_MODEL), _bf16)])
    return pl.pallas_call(
        _expert_kernel,
        out_shape=jax.ShapeDtypeStruct((p_rows, HALF), jnp.int32),
        grid_spec=grid_spec,
        compiler_params=pltpu.CompilerParams(dimension_semantics=("arbitrary",), vmem_limit_bytes=VMEM_LIMIT),
        name="experts",
    )(blk, bexp, active, xs, w_gu, b_gu.reshape(N_EXPERTS, 1, 2 * D_FF), w_down,
      b_down.reshape(N_EXPERTS, 1, D_MODEL))


def _sc_mesh():
    return plsc.VectorSubcoreMesh(core_axis_name="c", subcore_axis_name="s")


def _sc_worker():
    return lax.axis_index("s") * SC_CORES + lax.axis_index("c")


def _dispatch(hp, dest, p_rows):
    t_tokens = hp.shape[0]
    per_worker = t_tokens // (SC_CHUNK * SC_WORKERS)

    @functools.partial(
        pl.kernel, mesh=_sc_mesh(),
        out_type=jax.ShapeDtypeStruct((p_rows, HALF), jnp.int32),
        scratch_types=[pltpu.VMEM((TOP_K, SC_CHUNK), jnp.int32),
                       pltpu.VMEM((SC_CHUNK, HALF), jnp.int32),
                       pltpu.SemaphoreType.DMA],
        name="dispatch")
    def run(hp_hbm, dest_hbm, xs_hbm, idx_v, rows_v, sem):
        base = _sc_worker() * per_worker

        @pl.loop(0, per_worker)
        def _(j):
            chunk = base + j
            for k in range(TOP_K):
                pltpu.sync_copy(dest_hbm.at[pl.ds(k * t_tokens + chunk * SC_CHUNK, SC_CHUNK)], idx_v.at[k])
            pltpu.sync_copy(hp_hbm.at[pl.ds(chunk * SC_CHUNK, SC_CHUNK)], rows_v)
            copies = [pltpu.async_copy(rows_v, xs_hbm.at[idx_v.at[k]], sem) for k in range(TOP_K)]
            for cp in copies:
                cp.wait()

    return run(hp, dest)


def _combine_rows(y, dest, w, x1, g2, seq):
    t_tokens = x1.shape[0]
    cc = COMBINE_CHUNK
    n_chunks = t_tokens // cc
    per_worker = n_chunks // SC_WORKERS
    own = per_worker * cc
    assert per_worker % 2 == 0 and seq % own == 0
    n_vec = HALF // SC_LANES
    row_buf = pltpu.VMEM((TOP_K, cc, HALF), jnp.int32)
    x_buf = pltpu.VMEM((cc, D_MODEL), _f32)

    @functools.partial(
        pl.kernel, mesh=_sc_mesh(),
        out_type=jax.ShapeDtypeStruct((t_tokens, D_MODEL), _f32),
        scratch_types=[pltpu.VMEM((TOP_K * own,), jnp.int32),
                       pltpu.VMEM((TOP_K * own,), _f32),
                       row_buf, row_buf, x_buf, x_buf,
                       pltpu.VMEM((D_MODEL,), _f32),
                       pltpu.SemaphoreType.DMA, pltpu.SemaphoreType.DMA,
                       pltpu.SemaphoreType.DMA, pltpu.SemaphoreType.DMA],
        compiler_params=pltpu.CompilerParams(needs_layout_passes=False),
        name="combine_rows")
    def run(y_hbm, dest_hbm, w_hbm, x1_hbm, g2_hbm, x2_hbm,
            idx_v, w_v, rows0, rows1, xb0, xb1, g_v, in0, in1, out0, out1):
        base = _sc_worker() * per_worker
        bufs = ((rows0, xb0, in0, out0), (rows1, xb1, in1, out1))
        for k in range(TOP_K):
            pltpu.sync_copy(dest_hbm.at[pl.ds(k * t_tokens + base * cc, own)], idx_v.at[pl.ds(k * own, own)])
            pltpu.sync_copy(w_hbm.at[pl.ds(k * t_tokens + base * cc, own)], w_v.at[pl.ds(k * own, own)])
        pltpu.sync_copy(g2_hbm.at[(base * cc) // seq], g_v)

        def loads(c, b):
            rows, xb, isem, _ = bufs[b]
            cps = [pltpu.make_async_copy(y_hbm.at[idx_v.at[pl.ds(k * own + c * cc, cc)]], rows.at[k], isem)
                   for k in range(TOP_K)]
            cps.append(pltpu.make_async_copy(x1_hbm.at[pl.ds((base + c) * cc, cc)], xb, isem))
            return cps

        def store(c, b):
            _, xb, _, osem = bufs[b]
            return pltpu.make_async_copy(xb, x2_hbm.at[pl.ds((base + c) * cc, cc)], osem)

        def combine(c, b):
            rows, xb, _, _ = bufs[b]

            @pl.loop(0, cc)
            def _(t):
                ws = [plsc.load_gather(w_v, [jnp.full((SC_LANES,), k * own + c * cc + t, jnp.int32)])
                      for k in range(TOP_K)]

                @plsc.parallel_loop(0, n_vec)
                def _(j):
                    lo = pl.ds(j * SC_LANES, SC_LANES)
                    hi = pl.ds(HALF + j * SC_LANES, SC_LANES)
                    acc_a = None
                    acc_b = None
                    for k in range(TOP_K):
                        p = rows[k, t, lo]
                        a = lax.bitcast_convert_type(p & HI_MASK, _f32) * ws[k]
                        b_ = lax.bitcast_convert_type(p << HALF_WORD, _f32) * ws[k]
                        acc_a = a if acc_a is None else acc_a + a
                        acc_b = b_ if acc_b is None else acc_b + b_
                    xb[t, lo] = xb[t, lo] + g_v[lo] * acc_a
                    xb[t, hi] = xb[t, hi] + g_v[hi] * acc_b

        for cp in loads(0, 0):
            cp.start()

        @pl.loop(0, per_worker // 2)
        def _(c2):
            for b in range(2):
                c = 2 * c2 + b

                @pl.when(c >= 1)
                def _():
                    store(c - 1, 1 - b).wait()

                @pl.when(c + 1 < per_worker)
                def _():
                    for cp in loads(c + 1, 1 - b):
                        cp.start()

                for cp in loads(c, b):
                    cp.wait()
                combine(c, b)
                store(c, b).start()

        store(per_worker - 1, 1).wait()

    return run(y, dest, w, x1, g2)


def _final_kernel(x_ref, ng_ref, o_ref):
    o_ref[...] = _rms(x_ref[...], ng_ref[...])


def _final(x, final_g):
    t_tokens = x.shape[0]
    tm = TOKEN_TILE
    row = lambda i: (i, 0)
    return pl.pallas_call(
        _final_kernel,
        out_shape=jax.ShapeDtypeStruct((t_tokens, D_MODEL), _f32),
        grid=(t_tokens // tm,),
        in_specs=[pl.BlockSpec((tm, D_MODEL), row), pl.BlockSpec((1, D_MODEL), lambda i: (0, 0))],
        out_specs=pl.BlockSpec((tm, D_MODEL), row),
        compiler_params=pltpu.CompilerParams(dimension_semantics=("parallel",), vmem_limit_bytes=VMEM_LIMIT),
        name="final",
    )(x, final_g)


def _plan(route, counts, nblk):
    cnt = counts[:, 0]
    padded = ((cnt + EXPERT_BLOCK - 1) // EXPERT_BLOCK) * EXPERT_BLOCK
    pad_end = jnp.cumsum(padded)
    pad_start = pad_end - padded
    expert = jnp.arange(N_EXPERTS, dtype=jnp.int32)[:, None, None]
    start_of = jnp.sum(jnp.where(route[None, :TOP_K] == expert, pad_start[:, None, None], 0), axis=0)
    dest = start_of + route[TOP_K:2 * TOP_K]
    nact = jnp.maximum(pad_end[-1] // EXPERT_BLOCK, 1)
    step = jnp.arange(nblk, dtype=jnp.int32)
    blk = jnp.minimum(step, nact - 1)
    bexp = jnp.minimum(jnp.sum(pad_end[None, :] <= (blk * EXPERT_BLOCK)[:, None], axis=1), N_EXPERTS - 1)
    active = step < pad_end[-1] // EXPERT_BLOCK
    w = lax.bitcast_convert_type(route[2 * TOP_K:3 * TOP_K], _f32)
    return (dest.reshape(-1), w.reshape(-1), blk.astype(jnp.int32), bexp.astype(jnp.int32),
            active.astype(jnp.int32))


def _trunk(x_all, c_all, norm1_g, ada_w, ada_b, w_in, b_in, w_fmix, sinks, w_o, b_o,
           norm2_g, w_router, b_router, w_gu, b_gu, w_down, b_down, final_g):
    nb, seq, _ = x_all.shape
    t_tokens = nb * seq
    assert seq % TOKEN_TILE == 0 and seq % INPROJ_TILE == 0 and seq % (ATTN_TILES * Q_TILE) == 0 and seq >= KEY_SPAN
    assert t_tokens % (SC_CHUNK * SC_WORKERS) == 0 and seq % COMBINE_CHUNK == 0
    nblk = (t_tokens * TOP_K) // EXPERT_BLOCK + N_EXPERTS
    p_rows = nblk * EXPERT_BLOCK

    mod = _modulation(c_all, ada_w, ada_b)
    mod = mod.reshape(DEPTH, nb, 6, 1, D_MODEL)
    rope_tabs = _rope_tables(seq)
    dft = _dft_tables(seq)
    tri = jnp.triu(jnp.ones((RANK_TILE, RANK_TILE), _f32), 1).astype(_bf16)

    x = x_all.reshape(t_tokens, D_MODEL)
    for l in range(DEPTH):
        sh1, sc1, g1, sh2, sc2, g2 = (mod[l, :, j] for j in range(6))
        ab = _channel_dft_fold(w_fmix[l], seq)
        y, q, k, v = _inproj(seq, x, sh1, sc1, norm1_g[l].reshape(1, D_MODEL), w_in[l].astype(_bf16),
                             b_in[l].reshape(1, IN_WIDTH), ab, rope_tabs)
        mf = _fourier(seq, y, dft)
        attn = _attention(seq, q, k, v, sinks[l])
        w_o_bf = w_o[l].astype(_bf16)
        w_r = w_router[l].T.astype(_bf16)
        b_r = jnp.broadcast_to(b_router[l][:, None], (N_EXPERTS, TOKEN_TILE))
        x1, hp, route, counts = _outproj(seq, x, mf, attn, g1, sh2, sc2, norm2_g[l].reshape(1, D_MODEL),
                                         w_o_bf[:FOURIER_WIDTH], w_o_bf[FOURIER_WIDTH:],
                                         b_o[l].reshape(1, D_MODEL), w_r, b_r, tri)
        dest, w, blk, bexp, active = _plan(route, counts, nblk)
        xs = _dispatch(hp, dest, p_rows)
        ys = _experts(l, xs, blk, bexp, active, w_gu, b_gu[l], w_down, b_down[l])
        x = _combine_rows(ys, dest, w, x1, g2.reshape(nb, D_MODEL), seq)
    out = _final(x, final_g.reshape(1, D_MODEL))
    return out.reshape(nb, seq, D_MODEL)


def kernel(x_prompt, x_sample, c_prompt, c_sample, norm1_g, ada_w, ada_b, w_in, b_in, w_fmix, sinks, w_o, b_o,
           norm2_g, w_router, b_router, w_gu, b_gu, w_down, b_down, final_g):
    ws = (norm1_g, ada_w, ada_b, w_in, b_in, w_fmix, sinks, w_o, b_o,
          norm2_g, w_router, b_router, w_gu, b_gu, w_down, b_down, final_g)
    return _trunk(x_prompt, c_prompt, *ws), _trunk(x_sample, c_sample, *ws)
```
